```python
import math
import jax, jax.numpy as jnp
from jax import lax
import numpy as np

D_MODEL = 4096
BATCH = 4
SEQ = 2048
DEPTH = 4

N_FOURIER_GROUPS = 4
FOURIER_GROUP_DIM = D_MODEL // 16
D_FOURIER = N_FOURIER_GROUPS * FOURIER_GROUP_DIM
ATTN_HEAD_DIM = 128
N_ATTN_HEADS = (D_MODEL - D_FOURIER) // (2 * ATTN_HEAD_DIM)
D_ATTN = N_ATTN_HEADS * 2 * ATTN_HEAD_DIM
D_IN = D_FOURIER + 3 * D_ATTN
N_BRANCHES = 2
D_FF = 2 * D_MODEL
CONV_WIDTH = 3
N_REL_BUCKETS = 32
REL_MAX_DISTANCE = 128
Q_BLOCK = 128
EPS = 1e-6

kernel_name = "hybrid_fnet_diffattn_convglu_encoder"


def rms_norm(x, gain):
    xf = x.astype(jnp.float32)
    y = xf * lax.rsqrt(jnp.mean(xf * xf, axis=-1, keepdims=True) + EPS)
    return (y * gain.astype(jnp.float32)).astype(x.dtype)


def rel_bucket(rel):
    n = -rel
    half = N_REL_BUCKETS // 2
    ret = (n < 0).astype(jnp.int32) * half
    n = jnp.abs(n)
    max_exact = half // 2
    is_small = n < max_exact
    nf = jnp.maximum(n, 1).astype(jnp.float32)
    large = max_exact + (jnp.log(nf / max_exact) / math.log(REL_MAX_DISTANCE / max_exact)
                         * (half - max_exact)).astype(jnp.int32)
    large = jnp.minimum(large, half - 1)
    return ret + jnp.where(is_small, n, large)


def fourier_mix(u):
    b, s, _ = u.shape
    ug = u.reshape(b, s, N_FOURIER_GROUPS, FOURIER_GROUP_DIM).astype(jnp.float32)
    y = jnp.fft.fft2(ug, axes=(1, 3), norm="ortho").real
    return y.reshape(b, s, D_FOURIER).astype(u.dtype)


def diff_attention(q, k, v, rel_table, lam, sub_gain, lam_init):
    b, s, h, _, d = q.shape
    nblk = s // Q_BLOCK
    q = q * (d ** -0.5)
    qb = q.reshape(b, nblk, Q_BLOCK, h, 2, d).transpose(1, 0, 2, 3, 4, 5)
    key_pos = jnp.arange(s, dtype=jnp.int32)
    starts = jnp.arange(nblk, dtype=jnp.int32) * Q_BLOCK

    def block(args):
        q_blk, start = args
        q_pos = start + jnp.arange(Q_BLOCK, dtype=jnp.int32)
        bias = rel_table[rel_bucket(key_pos[None, :] - q_pos[:, None])]
        bias = bias.transpose(2, 0, 1).astype(jnp.float32)
        logits = jnp.einsum('bqhmd,bkhmd->bmhqk', q_blk, k,
                            preferred_element_type=jnp.float32) + bias[None, None]
        p = jax.nn.softmax(logits, axis=-1)
        attn = p[:, 0] - lam * p[:, 1]
        return jnp.einsum('bhqk,bkhe->bqhe', attn.astype(v.dtype), v)

    out = lax.map(block, (qb, starts))
    out = out.transpose(1, 0, 2, 3, 4).reshape(b, s, h, 2 * d)
    out = rms_norm(out, sub_gain) * (1.0 - lam_init)
    return out.reshape(b, s, h * 2 * d)


def conv_glu(h, w_up, conv_w, conv_b, w_down):
    s = h.shape[1]
    a = h @ w_up
    gate, val = a[..., :D_FF], a[..., D_FF:]
    pad = CONV_WIDTH // 2
    gp = jnp.pad(gate, ((0, 0), (pad, pad), (0, 0)))
    gate = sum(gp[:, j:j + s] * conv_w[j] for j in range(CONV_WIDTH)) + conv_b
    return (jax.nn.gelu(gate, approximate=False) * val) @ w_down


def setup_inputs(seed: int = 0) -> dict:
    key = jax.random.key(seed)
    ks = jax.random.split(key, 20)
    f32 = jnp.float32
    nrm = lambda k, shape, scale: jax.random.normal(k, shape, f32) * scale
    return {
        "x": nrm(ks[0], (BATCH, SEQ, D_MODEL), 1.0),
        "norm1_gain": 1.0 + nrm(ks[1], (DEPTH, D_MODEL), 0.02),
        "w_in": nrm(ks[2], (DEPTH, D_MODEL, D_IN), D_MODEL ** -0.5),
        "w_fourier_out": nrm(ks[3], (DEPTH, D_FOURIER, D_MODEL), D_FOURIER ** -0.5),
        "lambdas": nrm(ks[4], (DEPTH, 4, ATTN_HEAD_DIM), 0.1),
        "subln_gain": 1.0 + nrm(ks[5], (DEPTH, 2 * ATTN_HEAD_DIM), 0.02),
        "rel_bias_table": nrm(ks[6], (N_REL_BUCKETS, N_ATTN_HEADS), 0.5),
        "w_attn_out": nrm(ks[7], (DEPTH, D_ATTN, D_MODEL), D_ATTN ** -0.5),
        "w_gate": nrm(ks[8], (DEPTH, D_MODEL, N_BRANCHES * D_MODEL), D_MODEL ** -0.5),
        "b_gate": nrm(ks[9], (DEPTH, N_BRANCHES * D_MODEL), 0.02),
        "w_o": nrm(ks[10], (DEPTH, D_MODEL, D_MODEL), D_MODEL ** -0.5),
        "norm2_gain": 1.0 + nrm(ks[11], (DEPTH, D_MODEL), 0.02),
        "w_up": nrm(ks[12], (DEPTH, D_MODEL, 2 * D_FF), D_MODEL ** -0.5),
        "conv_w": nrm(ks[13], (DEPTH, CONV_WIDTH, D_FF), CONV_WIDTH ** -0.5),
        "conv_b": nrm(ks[14], (DEPTH, D_FF), 0.02),
        "w_down": nrm(ks[15], (DEPTH, D_FF, D_MODEL), D_FF ** -0.5),
        "final_norm_gain": 1.0 + nrm(ks[16], (D_MODEL,), 0.02),
    }


def reference(x, norm1_gain, w_in, w_fourier_out, lambdas, subln_gain, rel_bias_table, w_attn_out,
              w_gate, b_gate, w_o, norm2_gain, w_up, conv_w, conv_b, w_down, final_norm_gain):
    b, s, _ = x.shape
    for l in range(DEPTH):
        lam_init = 0.8 - 0.6 * math.exp(-0.3 * l)
        h = rms_norm(x, norm1_gain[l])
        u = h @ w_in[l]
        u_f = u[..., :D_FOURIER]
        q = u[..., D_FOURIER:D_FOURIER + D_ATTN].reshape(b, s, N_ATTN_HEADS, 2, ATTN_HEAD_DIM)
        k = u[..., D_FOURIER + D_ATTN:D_FOURIER + 2 * D_ATTN].reshape(b, s, N_ATTN_HEADS, 2, ATTN_HEAD_DIM)
        v = u[..., D_FOURIER + 2 * D_ATTN:].reshape(b, s, N_ATTN_HEADS, 2 * ATTN_HEAD_DIM)
        y_f = fourier_mix(u_f) @ w_fourier_out[l]
        lam_p = lambdas[l].astype(jnp.float32)
        lam = (jnp.exp(jnp.sum(lam_p[0] * lam_p[1])) - jnp.exp(jnp.sum(lam_p[2] * lam_p[3]))
               + lam_init)
        y_a = diff_attention(q, k, v, rel_bias_table, lam, subln_gain[l], lam_init) @ w_attn_out[l]
        g = jax.nn.sigmoid(h @ w_gate[l] + b_gate[l]).reshape(b, s, N_BRANCHES, D_MODEL)
        mixed = g[:, :, 0] * y_f + g[:, :, 1] * y_a
        x = x + mixed @ w_o[l]
        h2 = rms_norm(x, norm2_gain[l])
        x = x + conv_glu(h2, w_up[l], conv_w[l], conv_b[l], w_down[l])
    return rms_norm(x, final_norm_gain)
```

```python
import functools
import math

import numpy as np
import jax
import jax.numpy as jnp
from jax import lax
from jax.experimental import pallas as pl
from jax.experimental.pallas import tpu as pltpu

F32 = jnp.float32
BF16 = jnp.bfloat16

EPS = 1e-6
N_FOURIER_GROUPS = 4
FOURIER_GROUP_DIM = 256
D_FOURIER = N_FOURIER_GROUPS * FOURIER_GROUP_DIM
ATTN_HEAD_DIM = 128
HEAD_WIDTH = 2 * ATTN_HEAD_DIM
N_REL_BUCKETS = 32
CONV_WIDTH = 3

MIB = 1024 * 1024


def _params(vmem_mib, semantics):
    return pltpu.CompilerParams(dimension_semantics=semantics, vmem_limit_bytes=vmem_mib * MIB)


def _rmsnorm_kernel(x_ref, g_ref, o_ref):
    x = x_ref[...]
    ms = jnp.mean(x * x, axis=-1, keepdims=True)
    o_ref[...] = (x * lax.rsqrt(ms + EPS) * g_ref[...]).astype(o_ref.dtype)


def rmsnorm(x, gain, out_dtype, bm=256):
    m, d = x.shape
    return pl.pallas_call(
        _rmsnorm_kernel,
        grid=(m // bm,),
        in_specs=[pl.BlockSpec((bm, d), lambda i: (i, 0)),
                  pl.BlockSpec((1, d), lambda i: (0, 0))],
        out_specs=pl.BlockSpec((bm, d), lambda i: (i, 0)),
        out_shape=jax.ShapeDtypeStruct((m, d), out_dtype),
        compiler_params=_params(40, ("parallel",)),
        name="rmsnorm",
    )(x, gain.reshape(1, d))


def _in_proj_kernel(h_ref, w_ref, s_ref, o_ref):
    acc = jnp.dot(h_ref[...], w_ref[...], preferred_element_type=F32)
    o_ref[...] = (acc * s_ref[...]).astype(o_ref.dtype)


def in_proj(h, w, col_scale, bm=1024, bn=1024):
    m, k = h.shape
    n = w.shape[1]
    return pl.pallas_call(
        _in_proj_kernel,
        grid=(m // bm, n // bn),
        in_specs=[pl.BlockSpec((bm, k), lambda i, j: (i, 0)),
                  pl.BlockSpec((k, bn), lambda i, j: (0, j)),
                  pl.BlockSpec((1, bn), lambda i, j: (0, j))],
        out_specs=pl.BlockSpec((bm, bn), lambda i, j: (i, j)),
        out_shape=jax.ShapeDtypeStruct((m, n), BF16),
        compiler_params=_params(48, ("parallel", "arbitrary")),
        name="in_proj",
    )(h, w, col_scale)


def _dft_constants(seq, dim):
    def cos_sin(n):
        idx = np.arange(n, dtype=np.int64)
        ang = 2.0 * np.pi * ((idx[:, None] * idx[None, :]) % n).astype(np.float64) / n
        return np.cos(ang), np.sin(ang)
    cs, ss = cos_sin(seq)
    cc, sc = cos_sin(dim)
    chan = np.concatenate([cc, sc], axis=1)
    return (jnp.asarray(cs, BF16), jnp.asarray(-ss, BF16), jnp.asarray(chan, BF16))


def _fourier_chan_kernel(u_ref, c_ref, t1_ref, t2_ref):
    c = c_ref[...]
    gd = FOURIER_GROUP_DIM
    for g in range(N_FOURIER_GROUPS):
        t = jnp.dot(u_ref[:, g * gd:(g + 1) * gd], c, preferred_element_type=F32)
        t1_ref[:, g * gd:(g + 1) * gd] = t[:, :gd].astype(t1_ref.dtype)
        t2_ref[:, g * gd:(g + 1) * gd] = t[:, gd:].astype(t2_ref.dtype)


def fourier_chan(u, chan, bm=1024):
    m = u.shape[0]
    gd = FOURIER_GROUP_DIM
    out = jax.ShapeDtypeStruct((m, D_FOURIER), BF16)
    return pl.pallas_call(
        _fourier_chan_kernel,
        grid=(m // bm,),
        in_specs=[pl.BlockSpec((bm, D_FOURIER), lambda i: (i, 0)),
                  pl.BlockSpec((gd, 2 * gd), lambda i: (0, 0))],
        out_specs=[pl.BlockSpec((bm, D_FOURIER), lambda i: (i, 0)),
                   pl.BlockSpec((bm, D_FOURIER), lambda i: (i, 0))],
        out_shape=[out, out],
        compiler_params=_params(32, ("parallel",)),
        name="fourier_chan",
    )(u, chan)


def _fourier_seq_kernel(cs_ref, sn_ref, t1_ref, t2_ref, o_ref, *, scale):
    acc = jnp.dot(cs_ref[...], t1_ref[...], preferred_element_type=F32)
    acc += jnp.dot(sn_ref[...], t2_ref[...], preferred_element_type=F32)
    o_ref[...] = (acc * scale).astype(o_ref.dtype)


def fourier_seq(cs, sn, t1, t2, seq, bm=512):
    m, n = t1.shape
    nb = m // seq
    ni = seq // bm
    scale = 1.0 / math.sqrt(seq * FOURIER_GROUP_DIM)
    return pl.pallas_call(
        functools.partial(_fourier_seq_kernel, scale=scale),
        grid=(nb, ni),
        in_specs=[pl.BlockSpec((bm, seq), lambda b, i: (i, 0)),
                  pl.BlockSpec((bm, seq), lambda b, i: (i, 0)),
                  pl.BlockSpec((seq, n), lambda b, i: (b, 0)),
                  pl.BlockSpec((seq, n), lambda b, i: (b, 0))],
        out_specs=pl.BlockSpec((bm, n), lambda b, i: (b * ni + i, 0)),
        out_shape=jax.ShapeDtypeStruct((m, n), BF16),
        compiler_params=_params(40, ("parallel", "arbitrary")),
        name="fourier_seq",
    )(cs, sn, t1, t2)


def _bias_window_kernel(tab_ref, o_ref, *, tq, nq):
    h = pl.program_id(0)
    shape = o_ref.shape[1:]
    i = lax.broadcasted_iota(jnp.int32, shape, 0)
    c = lax.broadcasted_iota(jnp.int32, shape, 1)
    n = i + (nq - 1) * tq - c
    half = N_REL_BUCKETS // 2
    max_exact = half // 2
    ret = jnp.where(n < 0, half, 0)
    n = jnp.abs(n)
    n2 = n * n
    large = jnp.full(shape, max_exact, jnp.int32)
    for j in range(1, half - max_exact):
        large += (n2 >= (max_exact * max_exact) * (2 ** j)).astype(jnp.int32)
    bucket = ret + jnp.where(n < max_exact, n, large)
    acc = jnp.zeros(shape, F32)
    for b in range(N_REL_BUCKETS):
        acc = jnp.where(bucket == b, tab_ref[b, h], acc)
    o_ref[0] = acc


def bias_window(rel_table, seq, tq):
    nq = seq // tq
    nh = rel_table.shape[1]
    width = 2 * seq - tq
    return pl.pallas_call(
        functools.partial(_bias_window_kernel, tq=tq, nq=nq),
        grid=(nh,),
        in_specs=[pl.BlockSpec(memory_space=pltpu.SMEM)],
        out_specs=pl.BlockSpec((1, tq, width), lambda h: (h, 0, 0)),
        out_shape=jax.ShapeDtypeStruct((nh, tq, width), F32),
        compiler_params=_params(40, ("parallel",)),
        name="bias_window",
    )(rel_table)


def _attn_kernel(q_ref, k_ref, v_ref, wb_ref, lam_ref, sg_ref, o_ref, *, lam_init, tq, nq, seq):
    qi = pl.program_id(2)
    d = ATTN_HEAD_DIM
    off = pl.multiple_of((nq - 1 - qi) * tq, tq)
    bias = wb_ref[0, :, pl.ds(off, seq)]

    lam_p = lam_ref[...]
    lam = (jnp.exp(jnp.sum(lam_p[0:1] * lam_p[1:2], axis=-1, keepdims=True))
           - jnp.exp(jnp.sum(lam_p[2:3] * lam_p[3:4], axis=-1, keepdims=True)) + lam_init)

    def probs(m):
        s = lax.dot_general(q_ref[:, m * d:(m + 1) * d], k_ref[:, m * d:(m + 1) * d],
                            (((1,), (1,)), ((), ())), preferred_element_type=F32) + bias
        e = jnp.exp(s - jnp.max(s, axis=-1, keepdims=True))
        return e, jnp.sum(e, axis=-1, keepdims=True)

    e0, l0 = probs(0)
    e1, l1 = probs(1)
    attn = e0 * (1.0 / l0) - e1 * (lam / l1)
    o = jnp.dot(attn.astype(BF16), v_ref[...], preferred_element_type=F32)
    ms = jnp.mean(o * o, axis=-1, keepdims=True)
    o = o * lax.rsqrt(ms + EPS) * sg_ref[...] * (1.0 - lam_init)
    o_ref[...] = o.astype(o_ref.dtype)


def diff_attention(u, wb, lambdas, sub_gain, lam_init, seq, n_heads, tq):
    m = u.shape[0]
    nb = m // seq
    nq = seq // tq
    hw = HEAD_WIDTH
    q_blk = D_FOURIER // hw
    k_blk = q_blk + n_heads
    v_blk = k_blk + n_heads
    return pl.pallas_call(
        functools.partial(_attn_kernel, lam_init=lam_init, tq=tq, nq=nq, seq=seq),
        grid=(nb, n_heads, nq),
        in_specs=[pl.BlockSpec((tq, hw), lambda b, h, i: (b * nq + i, q_blk + h)),
                  pl.BlockSpec((seq, hw), lambda b, h, i: (b, k_blk + h)),
                  pl.BlockSpec((seq, hw), lambda b, h, i: (b, v_blk + h)),
                  pl.BlockSpec((1, tq, 2 * seq - tq), lambda b, h, i: (h, 0, 0)),
                  pl.BlockSpec((4, ATTN_HEAD_DIM), lambda b, h, i: (0, 0)),
                  pl.BlockSpec((1, hw), lambda b, h, i: (0, 0))],
        out_specs=pl.BlockSpec((tq, hw), lambda b, h, i: (b * nq + i, h)),
        out_shape=jax.ShapeDtypeStruct((m, n_heads * hw), BF16),
        compiler_params=_params(48, ("parallel", "parallel", "arbitrary")),
        name="diff_attention",
    )(u, u, u, wb, lambdas, sub_gain.reshape(1, hw))


def _mix_kernel(h_ref, f_ref, a_ref, wg0_ref, wg1_ref, wf_ref, wa_ref, b0_ref, b1_ref, o_ref):
    h = h_ref[...]
    g0 = jax.nn.sigmoid(jnp.dot(h, wg0_ref[...], preferred_element_type=F32) + b0_ref[...])
    g1 = jax.nn.sigmoid(jnp.dot(h, wg1_ref[...], preferred_element_type=F32) + b1_ref[...])
    y_f = jnp.dot(f_ref[...], wf_ref[...], preferred_element_type=F32)
    y_a = jnp.dot(a_ref[...], wa_ref[...], preferred_element_type=F32)
    o_ref[...] = (g0 * y_f + g1 * y_a).astype(o_ref.dtype)


def gated_mix(h, f, a, w_gate, w_f, w_a, b_gate, bm=1024, bn=256):
    m, d = h.shape
    nj = d // bn
    b_gate = b_gate.reshape(1, 2 * d)
    return pl.pallas_call(
        _mix_kernel,
        grid=(m // bm, nj),
        in_specs=[pl.BlockSpec((bm, d), lambda i, j: (i, 0)),
                  pl.BlockSpec((bm, f.shape[1]), lambda i, j: (i, 0)),
                  pl.BlockSpec((bm, a.shape[1]), lambda i, j: (i, 0)),
                  pl.BlockSpec((d, bn), lambda i, j: (0, j)),
                  pl.BlockSpec((d, bn), lambda i, j: (0, j + nj)),
                  pl.BlockSpec((f.shape[1], bn), lambda i, j: (0, j)),
                  pl.BlockSpec((a.shape[1], bn), lambda i, j: (0, j)),
                  pl.BlockSpec((1, bn), lambda i, j: (0, j)),
                  pl.BlockSpec((1, bn), lambda i, j: (0, j + nj))],
        out_specs=pl.BlockSpec((bm, bn), lambda i, j: (i, j)),
        out_shape=jax.ShapeDtypeStruct((m, d), BF16),
        compiler_params=_params(56, ("parallel", "arbitrary")),
        name="gated_mix",
    )(h, f, a, w_gate, w_gate, w_f, w_a, b_gate, b_gate)


def _proj_residual_kernel(a_ref, w_ref, r_ref, o_ref):
    o_ref[...] = r_ref[...] + jnp.dot(a_ref[...], w_ref[...], preferred_element_type=F32)


def proj_residual(a, w, res, bm, bn):
    m, k = a.shape
    n = w.shape[1]
    return pl.pallas_call(
        _proj_residual_kernel,
        grid=(m // bm, n // bn),
        in_specs=[pl.BlockSpec((bm, k), lambda i, j: (i, 0)),
                  pl.BlockSpec((k, bn), lambda i, j: (0, j)),
                  pl.BlockSpec((bm, bn), lambda i, j: (i, j))],
        out_specs=pl.BlockSpec((bm, bn), lambda i, j: (i, j)),
        out_shape=jax.ShapeDtypeStruct((m, n), F32),
        compiler_params=_params(56, ("parallel", "arbitrary")),
        name="proj_residual",
    )(a, w, res)


def _up_proj_kernel(h_ref, wg_ref, wv_ref, g_ref, v_ref):
    h = h_ref[...]
    g_ref[...] = jnp.dot(h, wg_ref[...], preferred_element_type=F32)
    v_ref[...] = jnp.dot(h, wv_ref[...], preferred_element_type=F32).astype(v_ref.dtype)


def up_proj(h, w_up, bm=1024, bn=512):
    m, k = h.shape
    dff = w_up.shape[1] // 2
    nj = dff // bn
    return pl.pallas_call(
        _up_proj_kernel,
        grid=(m // bm, nj),
        in_specs=[pl.BlockSpec((bm, k), lambda i, j: (i, 0)),
                  pl.BlockSpec((k, bn), lambda i, j: (0, j)),
                  pl.BlockSpec((k, bn), lambda i, j: (0, j + nj))],
        out_specs=[pl.BlockSpec((bm, bn), lambda i, j: (i, j)),
                   pl.BlockSpec((bm, bn), lambda i, j: (i, j))],
        out_shape=[jax.ShapeDtypeStruct((m, dff), F32), jax.ShapeDtypeStruct((m, dff), BF16)],
        compiler_params=_params(56, ("parallel", "arbitrary")),
        name="up_proj",
    )(h, w_up, w_up)


def _gelu_exact(x):
    return 0.5 * x * (1.0 + lax.erf(x * math.sqrt(0.5)))


def _conv_act_kernel(g_ref, gp_ref, gn_ref, v_ref, cw_ref, cb_ref, o_ref, *, seq):
    bm = g_ref.shape[0]
    g = g_ref[...]
    row = lax.broadcasted_iota(jnp.int32, (bm, 1), 0)
    pos = (pl.program_id(0) * bm + row) % seq
    halo = gp_ref.shape[0]
    g_prev = jnp.where(row == 0, gp_ref[halo - 1:halo, :], pltpu.roll(g, 1, axis=0))
    g_prev = jnp.where(pos == 0, 0.0, g_prev)
    g_next = jnp.where(row == bm - 1, gn_ref[0:1, :], pltpu.roll(g, bm - 1, axis=0))
    g_next = jnp.where(pos == seq - 1, 0.0, g_next)
    cw = cw_ref[...]
    conv = g_prev * cw[0:1] + g * cw[1:2] + g_next * cw[2:3] + cb_ref[...]
    o_ref[...] = (_gelu_exact(conv) * v_ref[...].astype(F32)).astype(o_ref.dtype)


def conv_act(gate, val, conv_w, conv_b, seq, bm=256, bn=1024, halo=8):
    m, dff = gate.shape
    rb = bm // halo
    last = m // halo - 1
    return pl.pallas_call(
        functools.partial(_conv_act_kernel, seq=seq),
        grid=(m // bm, dff // bn),
        in_specs=[pl.BlockSpec((bm, bn), lambda i, j: (i, j)),
                  pl.BlockSpec((halo, bn), lambda i, j: (jnp.maximum(i * rb - 1, 0), j)),
                  pl.BlockSpec((halo, bn), lambda i, j: (jnp.minimum((i + 1) * rb, last), j)),
                  pl.BlockSpec((bm, bn), lambda i, j: (i, j)),
                  pl.BlockSpec((CONV_WIDTH, bn), lambda i, j: (0, j)),
                  pl.BlockSpec((1, bn), lambda i, j: (0, j))],
        out_specs=pl.BlockSpec((bm, bn), lambda i, j: (i, j)),
        out_shape=jax.ShapeDtypeStruct((m, dff), BF16),
        compiler_params=_params(40, ("parallel", "parallel")),
        name="conv_act",
    )(gate, gate, gate, val, conv_w, conv_b.reshape(1, dff))


def kernel(x, norm1_gain, w_in, w_fourier_out, lambdas, subln_gain, rel_bias_table, w_attn_out,
           w_gate, b_gate, w_o, norm2_gain, w_up, conv_w, conv_b, w_down, final_norm_gain):
    batch, seq, d_model = x.shape
    depth = w_in.shape[0]
    n_heads = rel_bias_table.shape[1]
    d_attn = n_heads * HEAD_WIDTH
    d_in = w_in.shape[2]
    tq = 256

    cs, sn, chan = _dft_constants(seq, FOURIER_GROUP_DIM)
    col = np.ones((1, d_in), np.float32)
    col[:, D_FOURIER:D_FOURIER + d_attn] = ATTN_HEAD_DIM ** -0.5
    col_scale = jnp.asarray(col)
    wb = bias_window(rel_bias_table, seq, tq)

    x = x.reshape(batch * seq, d_model)
    for l in range(depth):
        lam_init = 0.8 - 0.6 * math.exp(-0.3 * l)
        h = rmsnorm(x, norm1_gain[l], BF16)
        u = in_proj(h, w_in[l].astype(BF16), col_scale)
        t1, t2 = fourier_chan(u, chan)
        f = fourier_seq(cs, sn, t1, t2, seq)
        a = diff_attention(u, wb, lambdas[l], subln_gain[l], lam_init, seq, n_heads, tq)
        mixed = gated_mix(h, f, a, w_gate[l].astype(BF16), w_fourier_out[l].astype(BF16),
                          w_attn_out[l].astype(BF16), b_gate[l])
        x = proj_residual(mixed, w_o[l].astype(BF16), x, bm=1024, bn=1024)
        h2 = rmsnorm(x, norm2_gain[l], BF16)
        gate, val = up_proj(h2, w_up[l].astype(BF16))
        act = conv_act(gate, val, conv_w[l], conv_b[l], seq)
        x = proj_residual(act, w_down[l].astype(BF16), x, bm=512, bn=512)
    out = rmsnorm(x, final_norm_gain, F32)
    return out.reshape(batch, seq, d_model)
```

```python
import functools
import math

import numpy as np
import jax
import jax.numpy as jnp
from jax import lax
from jax.experimental import pallas as pl
from jax.experimental.pallas import tpu as pltpu

F32 = jnp.float32
BF16 = jnp.bfloat16

EPS = 1e-6
N_FOURIER_GROUPS = 4
FOURIER_GROUP_DIM = 256
D_FOURIER = N_FOURIER_GROUPS * FOURIER_GROUP_DIM
ATTN_HEAD_DIM = 128
HEAD_WIDTH = 2 * ATTN_HEAD_DIM
N_REL_BUCKETS = 32
REL_MAX_DISTANCE = 128
LOG2_E = math.log2(math.e)
ATTN_KEY_CHUNK = 512
CONV_WIDTH = 3
BF16_SUBLANES = 16
F32_SUBLANES = 8
PIPELINE_CHUNKS = 16

MIB = 1024 * 1024


def _params(vmem_mib, semantics):
    return pltpu.CompilerParams(dimension_semantics=semantics, vmem_limit_bytes=vmem_mib * MIB)


def _resident(shape, index_map):
    return pl.BlockSpec(shape, index_map, pipeline_mode=pl.Buffered(1))


def _rmsnorm_kernel(x_ref, g_ref, o_ref):
    x = x_ref[...]
    ms = jnp.mean(x * x, axis=-1, keepdims=True)
    o_ref[...] = (x * lax.rsqrt(ms + EPS) * g_ref[...]).astype(o_ref.dtype)


def rmsnorm(x, gain, out_dtype, bm=256):
    m, d = x.shape
    return pl.pallas_call(
        _rmsnorm_kernel,
        grid=(m // bm,),
        in_specs=[pl.BlockSpec((bm, d), lambda i: (i, 0)),
                  pl.BlockSpec((1, d), lambda i: (0, 0))],
        out_specs=pl.BlockSpec((bm, d), lambda i: (i, 0)),
        out_shape=jax.ShapeDtypeStruct((m, d), out_dtype),
        compiler_params=_params(40, ("parallel",)),
        name="rmsnorm",
    )(x, gain.reshape(1, d))


def _in_proj_kernel(h_ref, w_ref, s_ref, o_ref):
    acc = jnp.dot(h_ref[...], w_ref[...].astype(BF16), preferred_element_type=F32)
    o_ref[...] = (acc * s_ref[...]).astype(o_ref.dtype)


def in_proj(h, w, l, col_scale, bm=1024, bn=512):
    m, k = h.shape
    n = w.shape[2]
    return pl.pallas_call(
        _in_proj_kernel,
        grid=(m // bm, n // bn),
        in_specs=[pl.BlockSpec((bm, k), lambda i, j: (i, 0)),
                  pl.BlockSpec((None, k, bn), lambda i, j: (l, 0, j)),
                  pl.BlockSpec((1, bn), lambda i, j: (0, j))],
        out_specs=pl.BlockSpec((bm, bn), lambda i, j: (i, j)),
        out_shape=jax.ShapeDtypeStruct((m, n), BF16),
        compiler_params=_params(48, ("parallel", "arbitrary")),
        name="in_proj",
    )(h, w, col_scale)


def _dft_constants(seq, dim):
    def cos_sin(n):
        idx = np.arange(n, dtype=np.int64)
        ang = 2.0 * np.pi * ((idx[:, None] * idx[None, :]) % n).astype(np.float64) / n
        return np.cos(ang), np.sin(ang)
    cs, ss = cos_sin(seq)
    cc, sc = cos_sin(dim)
    chan = np.concatenate([cc, sc], axis=1)
    return (jnp.asarray(cs, BF16), jnp.asarray(-ss, BF16), jnp.asarray(chan, BF16))


def _fourier_chan_kernel(u_ref, c_ref, t1_ref, t2_ref):
    c = c_ref[...]
    gd = FOURIER_GROUP_DIM
    for g in range(N_FOURIER_GROUPS):
        t = jnp.dot(u_ref[:, g * gd:(g + 1) * gd], c, preferred_element_type=F32)
        t1_ref[:, g * gd:(g + 1) * gd] = t[:, :gd].astype(t1_ref.dtype)
        t2_ref[:, g * gd:(g + 1) * gd] = t[:, gd:].astype(t2_ref.dtype)


def fourier_chan(u, chan, bm=1024):
    m = u.shape[0]
    gd = FOURIER_GROUP_DIM
    out = jax.ShapeDtypeStruct((m, D_FOURIER), BF16)
    return pl.pallas_call(
        _fourier_chan_kernel,
        grid=(m // bm,),
        in_specs=[pl.BlockSpec((bm, D_FOURIER), lambda i: (i, 0)),
                  pl.BlockSpec((gd, 2 * gd), lambda i: (0, 0))],
        out_specs=[pl.BlockSpec((bm, D_FOURIER), lambda i: (i, 0)),
                   pl.BlockSpec((bm, D_FOURIER), lambda i: (i, 0))],
        out_shape=[out, out],
        compiler_params=_params(32, ("parallel",)),
        name="fourier_chan",
    )(u, chan)


def _fourier_seq_kernel(cs_ref, sn_ref, t1_ref, t2_ref, o_ref, *, scale):
    acc = jnp.dot(cs_ref[...], t1_ref[...], preferred_element_type=F32)
    acc += jnp.dot(sn_ref[...], t2_ref[...], preferred_element_type=F32)
    o_ref[...] = (acc * scale).astype(o_ref.dtype)


def fourier_seq(cs, sn, t1, t2, seq, bm=512):
    m, n = t1.shape
    nb = m // seq
    ni = seq // bm
    scale = 1.0 / math.sqrt(seq * FOURIER_GROUP_DIM)
    return pl.pallas_call(
        functools.partial(_fourier_seq_kernel, scale=scale),
        grid=(nb, ni),
        in_specs=[pl.BlockSpec((bm, seq), lambda b, i: (i, 0)),
                  pl.BlockSpec((bm, seq), lambda b, i: (i, 0)),
                  pl.BlockSpec((seq, n), lambda b, i: (b, 0)),
                  pl.BlockSpec((seq, n), lambda b, i: (b, 0))],
        out_specs=pl.BlockSpec((bm, n), lambda b, i: (b * ni + i, 0)),
        out_shape=jax.ShapeDtypeStruct((m, n), BF16),
        compiler_params=_params(40, ("parallel", "arbitrary")),
        name="fourier_seq",
    )(cs, sn, t1, t2)


def _bias_window_kernel(tab_ref, o_ref, *, tq, nq):
    h = pl.program_id(0)
    half = N_REL_BUCKETS // 2
    max_exact = half // 2
    width = o_ref.shape[2]
    lo = (nq - 1) * tq - REL_MAX_DISTANCE
    hi = nq * tq + REL_MAX_DISTANCE
    o_ref[0, :, :lo] = jnp.full((tq, lo), tab_ref[half - 1, h] * LOG2_E, F32)
    o_ref[0, :, hi:] = jnp.full((tq, width - hi), tab_ref[N_REL_BUCKETS - 1, h] * LOG2_E, F32)

    shape = (tq, hi - lo)
    i = lax.broadcasted_iota(jnp.int32, shape, 0)
    c = lax.broadcasted_iota(jnp.int32, shape, 1) + lo
    n = i + (nq - 1) * tq - c
    ret = jnp.where(n < 0, half, 0)
    n = jnp.abs(n)
    n2 = n * n
    large = jnp.full(shape, max_exact, jnp.int32)
    for j in range(1, half - max_exact):
        large += (n2 >= (max_exact * max_exact) * (2 ** j)).astype(jnp.int32)
    bucket = ret + jnp.where(n < max_exact, n, large)
    acc = jnp.zeros(shape, F32)
    for b in range(N_REL_BUCKETS):
        acc = jnp.where(bucket == b, tab_ref[b, h] * LOG2_E, acc)
    o_ref[0, :, lo:hi] = acc


def bias_window(rel_table, seq, tq):
    nq = seq // tq
    nh = rel_table.shape[1]
    width = 2 * seq - tq
    assert tq >= REL_MAX_DISTANCE and nq >= 2
    return pl.pallas_call(
        functools.partial(_bias_window_kernel, tq=tq, nq=nq),
        grid=(nh,),
        in_specs=[pl.BlockSpec(memory_space=pltpu.SMEM)],
        out_specs=pl.BlockSpec((1, tq, width), lambda h: (h, 0, 0)),
        out_shape=jax.ShapeDtypeStruct((nh, tq, width), F32),
        compiler_params=_params(40, ("parallel",)),
        name="bias_window",
    )(rel_table)


def _lane_tile_reduce(x, op):
    lanes = 128
    out = x[:, 0:lanes]
    for t in range(1, x.shape[1] // lanes):
        out = op(out, x[:, t * lanes:(t + 1) * lanes])
    return out


def _attn_kernel(q_ref, k_ref, v_ref, wb_ref, lam_ref, sg_ref, o_ref, s0_ref, s1_ref, mx0_ref, mx1_ref,
                 *, lam_init, tq, nq, seq):
    i = pl.program_id(2)
    d = ATTN_HEAD_DIM
    ck = ATTN_KEY_CHUNK
    nc = seq // ck
    s_scr = (s0_ref, s1_ref)
    mx_scr = (mx0_ref, mx1_ref)
    chunks = [(m, c) for m in range(2) for c in range(nc)]

    def logits_chunk(p, m, c, run_max):
        off = pl.multiple_of((nq - 1 - i) * tq + c * ck, tq)
        s = lax.dot_general(q_ref[:, m * d:(m + 1) * d], k_ref[c * ck:(c + 1) * ck, m * d:(m + 1) * d],
                            (((1,), (1,)), ((), ())), preferred_element_type=F32)
        s = s + wb_ref[0, :, pl.ds(off, ck)]
        s_scr[p][m, :, c * ck:(c + 1) * ck] = s
        part = _lane_tile_reduce(s, jnp.maximum)
        run_max = part if c == 0 else jnp.maximum(run_max, part)
        if c == nc - 1:
            mx_scr[p][m] = run_max
        return run_max

    def probs_chunk(p, m, c, state):
        row_max, row_sum, out = state
        if c == 0:
            row_max = jnp.max(mx_scr[p][m], axis=-1, keepdims=True)
        e = jnp.exp2(s_scr[p][m, :, c * ck:(c + 1) * ck] - row_max)
        part = _lane_tile_reduce(e, jnp.add)
        pv = jnp.dot(e.astype(BF16), v_ref[c * ck:(c + 1) * ck, :], preferred_element_type=F32)
        return (row_max, part if c == 0 else row_sum + part, pv if c == 0 else out + pv)

    def finish(results):
        (_, sum0, out0), (_, sum1, out1) = results
        lam_p = lam_ref[...]
        lam = (jnp.exp(jnp.sum(lam_p[0:1] * lam_p[1:2], axis=-1, keepdims=True))
               - jnp.exp(jnp.sum(lam_p[2:3] * lam_p[3:4], axis=-1, keepdims=True)) + lam_init)
        l0 = jnp.sum(sum0, axis=-1, keepdims=True)
        l1 = jnp.sum(sum1, axis=-1, keepdims=True)
        o = out0 * (1.0 / l0) - out1 * (lam / l1)
        ms = jnp.mean(o * o, axis=-1, keepdims=True)
        o = o * lax.rsqrt(ms + EPS) * sg_ref[...] * (1.0 - lam_init)
        o_ref[...] = o.astype(o_ref.dtype)

    def stage(p, do_logits, do_probs):
        run_max = None
        state = (None, None, None)
        results = []
        for m, c in chunks:
            if do_logits:
                run_max = logits_chunk(p, m, c, run_max)
            if do_probs:
                state = probs_chunk(1 - p, m, c, state)
                if c == nc - 1:
                    results.append(state)
        if do_probs:
            finish(results)

    pl.when(i == 0)(lambda: stage(0, True, False))
    steady = jnp.logical_and(i >= 1, i < nq)
    pl.when(jnp.logical_and(steady, i % 2 == 0))(lambda: stage(0, True, True))
    pl.when(jnp.logical_and(steady, i % 2 == 1))(lambda: stage(1, True, True))
    pl.when(i == nq)(lambda: stage(nq % 2, False, True))


def diff_attention(u, wb, lambdas, sub_gain, lam_init, seq, n_heads, tq):
    m = u.shape[0]
    nb = m // seq
    nq = seq // tq
    hw = HEAD_WIDTH
    q_blk = D_FOURIER // hw
    k_blk = q_blk + n_heads
    v_blk = k_blk + n_heads
    assert ATTN_KEY_CHUNK % tq == 0 and seq % ATTN_KEY_CHUNK == 0
    logits = pltpu.VMEM((2, tq, seq), F32)
    maxima = pltpu.VMEM((2, tq, 128), F32)
    return pl.pallas_call(
        functools.partial(_attn_kernel, lam_init=lam_init, tq=tq, nq=nq, seq=seq),
        grid=(nb, n_heads, nq + 1),
        in_specs=[pl.BlockSpec((tq, hw), lambda b, h, i: (b * nq + jnp.minimum(i, nq - 1), q_blk + h)),
                  pl.BlockSpec((seq, hw), lambda b, h, i: (b, k_blk + h)),
                  pl.BlockSpec((seq, hw), lambda b, h, i: (b, v_blk + h)),
                  pl.BlockSpec((1, tq, 2 * seq - tq), lambda b, h, i: (h, 0, 0)),
                  pl.BlockSpec((4, ATTN_HEAD_DIM), lambda b, h, i: (0, 0)),
                  pl.BlockSpec((1, hw), lambda b, h, i: (0, 0))],
        out_specs=pl.BlockSpec((tq, hw), lambda b, h, i: (b * nq + jnp.maximum(i - 1, 0), h)),
        out_shape=jax.ShapeDtypeStruct((m, n_heads * hw), BF16),
        scratch_shapes=[logits, logits, maxima, maxima],
        compiler_params=_params(48, ("parallel", "parallel", "arbitrary")),
        name="diff_attention",
    )(u, u, u, wb, lambdas, sub_gain.reshape(1, hw))


def _mix_kernel(h_ref, f_ref, a_ref, wg0_ref, wg1_ref, wf_ref, wa_ref, b0_ref, b1_ref, o_ref):
    h = h_ref[...]
    g0 = jax.nn.sigmoid(jnp.dot(h, wg0_ref[...].astype(BF16), preferred_element_type=F32) + b0_ref[...])
    g1 = jax.nn.sigmoid(jnp.dot(h, wg1_ref[...].astype(BF16), preferred_element_type=F32) + b1_ref[...])
    y_f = jnp.dot(f_ref[...], wf_ref[...].astype(BF16), preferred_element_type=F32)
    y_a = jnp.dot(a_ref[...], wa_ref[...].astype(BF16), preferred_element_type=F32)
    o_ref[...] = (g0 * y_f + g1 * y_a).astype(o_ref.dtype)


def gated_mix(h, f, a, w_gate, w_f, w_a, l, b_gate, bm=1024, bn=256):
    m, d = h.shape
    nj = d // bn
    b_gate = b_gate.reshape(1, 2 * d)
    return pl.pallas_call(
        _mix_kernel,
        grid=(m // bm, nj),
        in_specs=[_resident((bm, d), lambda i, j: (i, 0)),
                  _resident((bm, f.shape[1]), lambda i, j: (i, 0)),
                  _resident((bm, a.shape[1]), lambda i, j: (i, 0)),
                  pl.BlockSpec((None, d, bn), lambda i, j: (l, 0, j)),
                  pl.BlockSpec((None, d, bn), lambda i, j: (l, 0, j + nj)),
                  pl.BlockSpec((None, f.shape[1], bn), lambda i, j: (l, 0, j)),
                  pl.BlockSpec((None, a.shape[1], bn), lambda i, j: (l, 0, j)),
                  pl.BlockSpec((1, bn), lambda i, j: (0, j)),
                  pl.BlockSpec((1, bn), lambda i, j: (0, j + nj))],
        out_specs=pl.BlockSpec((bm, bn), lambda i, j: (i, j)),
        out_shape=jax.ShapeDtypeStruct((m, d), BF16),
        compiler_params=_params(56, ("parallel", "arbitrary")),
        name="gated_mix",
    )(h, f, a, w_gate, w_gate, w_f, w_a, b_gate, b_gate)


def _proj_residual_kernel(a_ref, w_ref, r_ref, o_ref):
    o_ref[...] = r_ref[...] + jnp.dot(a_ref[...], w_ref[...].astype(BF16), preferred_element_type=F32)


def proj_residual(a, w, l, res, bm, bn, resident_lhs=False):
    m, k = a.shape
    n = w.shape[2]
    lhs_spec = _resident if resident_lhs else pl.BlockSpec
    return pl.pallas_call(
        _proj_residual_kernel,
        grid=(m // bm, n // bn),
        in_specs=[lhs_spec((bm, k), lambda i, j: (i, 0)),
                  pl.BlockSpec((None, k, bn), lambda i, j: (l, 0, j)),
                  pl.BlockSpec((bm, bn), lambda i, j: (i, j))],
        out_specs=pl.BlockSpec((bm, bn), lambda i, j: (i, j)),
        out_shape=jax.ShapeDtypeStruct((m, n), F32),
        compiler_params=_params(56, ("parallel", "arbitrary")),
        name="proj_residual",
    )(a, w, res)


def _gelu_exact(x):
    return 0.5 * x * (1.0 + lax.erf(x * math.sqrt(0.5)))


def _conv_glu_up_kernel(h_ref, hp_ref, hn_ref, wg_ref, wv_ref, cw_ref, cb_ref, o_ref,
                        ext_ref, wgb0, wgb1, wvb0, wvb1, gate0, gate1, val0, val1, *, seq, nj):
    j = pl.program_id(1)
    bm, k = h_ref.shape
    halo = BF16_SUBLANES
    wgb, wvb, gate_s, val_s = (wgb0, wgb1), (wvb0, wvb1), (gate0, gate1), (val0, val1)

    @pl.when(j == 0)
    def _():
        ext_ref[0:halo, :] = hp_ref[...]
        ext_ref[halo:halo + bm, :] = h_ref[...]
        ext_ref[halo + bm:, :] = hn_ref[...]

    kc = k // PIPELINE_CHUNKS
    rc = bm // PIPELINE_CHUNKS
    pad = F32_SUBLANES

    def cast_chunk(p, c):
        ks = slice(c * kc, (c + 1) * kc)
        wgb[p][ks, :] = wg_ref[ks, :].astype(BF16)
        wvb[p][ks, :] = wv_ref[ks, :].astype(BF16)

    def epilogue_chunk(p, c):
        r0 = halo + c * rc
        g = gate_s[p][r0 - pad:r0 + rc + pad, :]
        row = lax.broadcasted_iota(jnp.int32, (rc, 1), 0) + c * rc
        pos = (pl.program_id(0) * bm + row) % seq
        g_prev = jnp.where(pos == 0, 0.0, pltpu.roll(g, 1, axis=0)[pad:pad + rc])
        g_next = jnp.where(pos == seq - 1, 0.0, pltpu.roll(g, rc + 2 * pad - 1, axis=0)[pad:pad + rc])
        cw = cw_ref[...]
        conv = g_prev * cw[0:1] + g[pad:pad + rc] * cw[1:2] + g_next * cw[2:3] + cb_ref[...]
        rs = slice(c * rc, (c + 1) * rc)
        o_ref[rs, :] = (_gelu_exact(conv) * val_s[p][rs, :]).astype(o_ref.dtype)

    def stage(p, do_cast, do_matmul, do_epilogue):
        q = 1 - p
        gate_acc = val_acc = None
        for c in range(PIPELINE_CHUNKS):
            if do_matmul:
                ks = slice(c * kc, (c + 1) * kc)
                gd = jnp.dot(ext_ref[:, ks], wgb[q][ks, :], preferred_element_type=F32)
                vd = jnp.dot(ext_ref[halo:halo + bm, ks], wvb[q][ks, :], preferred_element_type=F32)
                gate_acc = gd if c == 0 else gate_acc + gd
                val_acc = vd if c == 0 else val_acc + vd
            if do_cast:
                cast_chunk(p, c)
            if do_epilogue:
                epilogue_chunk(p, c)
        if do_matmul:
            gate_s[q][...] = gate_acc
            val_s[q][...] = val_acc

    pl.when(j == 0)(lambda: stage(0, True, False, False))
    pl.when(j == 1)(lambda: stage(1, True, True, False))
    steady = jnp.logical_and(j >= 2, j < nj)
    pl.when(jnp.logical_and(steady, j % 2 == 0))(lambda: stage(0, True, True, True))
    pl.when(jnp.logical_and(steady, j % 2 == 1))(lambda: stage(1, True, True, True))
    pl.when(j == nj)(lambda: stage(nj % 2, False, True, True))
    pl.when(j == nj + 1)(lambda: stage((nj + 1) % 2, False, False, True))


def conv_glu_up(h, w_up, l, conv_w, conv_b, seq, bm=1024, bn=256):
    m, k = h.shape
    dff = w_up.shape[2] // 2
    nj = dff // bn
    halo = BF16_SUBLANES
    rb = bm // halo
    last = m // halo - 1
    assert nj >= 2
    w_tile = lambda j: jnp.minimum(j, nj - 1)
    o_tile = lambda j: jnp.clip(j - 2, 0, nj - 1)
    wb = pltpu.VMEM((k, bn), BF16)
    return pl.pallas_call(
        functools.partial(_conv_glu_up_kernel, seq=seq, nj=nj),
        grid=(m // bm, nj + 2),
        in_specs=[_resident((bm, k), lambda i, j: (i, 0)),
                  _resident((halo, k), lambda i, j: (jnp.maximum(i * rb - 1, 0), 0)),
                  _resident((halo, k), lambda i, j: (jnp.minimum((i + 1) * rb, last), 0)),
                  pl.BlockSpec((None, k, bn), lambda i, j: (l, 0, w_tile(j))),
                  pl.BlockSpec((None, k, bn), lambda i, j: (l, 0, w_tile(j) + nj)),
                  pl.BlockSpec((CONV_WIDTH, bn), lambda i, j: (0, o_tile(j))),
                  pl.BlockSpec((1, bn), lambda i, j: (0, o_tile(j)))],
        out_specs=pl.BlockSpec((bm, bn), lambda i, j: (i, o_tile(j))),
        out_shape=jax.ShapeDtypeStruct((m, dff), BF16),
        scratch_shapes=[pltpu.VMEM((bm + 2 * halo, k), BF16), wb, wb, wb, wb,
                        pltpu.VMEM((bm + 2 * halo, bn), F32), pltpu.VMEM((bm + 2 * halo, bn), F32),
                        pltpu.VMEM((bm, bn), F32), pltpu.VMEM((bm, bn), F32)],
        compiler_params=_params(56, ("parallel", "arbitrary")),
        name="conv_glu_up",
    )(h, h, h, w_up, w_up, conv_w, conv_b.reshape(1, dff))


def kernel(x, norm1_gain, w_in, w_fourier_out, lambdas, subln_gain, rel_bias_table, w_attn_out,
           w_gate, b_gate, w_o, norm2_gain, w_up, conv_w, conv_b, w_down, final_norm_gain):
    batch, seq, d_model = x.shape
    depth = w_in.shape[0]
    n_heads = rel_bias_table.shape[1]
    d_attn = n_heads * HEAD_WIDTH
    d_in = w_in.shape[2]
    tq = 256

    cs, sn, chan = _dft_constants(seq, FOURIER_GROUP_DIM)
    col = np.ones((1, d_in), np.float32)
    col[:, D_FOURIER:D_FOURIER + d_attn] = ATTN_HEAD_DIM ** -0.5 * LOG2_E
    col_scale = jnp.asarray(col)
    wb = bias_window(rel_bias_table, seq, tq)

    x = x.reshape(batch * seq, d_model)
    for l in range(depth):
        lam_init = 0.8 - 0.6 * math.exp(-0.3 * l)
        h = rmsnorm(x, norm1_gain[l], BF16)
        u = in_proj(h, w_in, l, col_scale)
        t1, t2 = fourier_chan(u, chan)
        f = fourier_seq(cs, sn, t1, t2, seq)
        a = diff_attention(u, wb, lambdas[l], subln_gain[l], lam_init, seq, n_heads, tq)
        mixed = gated_mix(h, f, a, w_gate, w_fourier_out, w_attn_out, l, b_gate[l])
        x = proj_residual(mixed, w_o, l, x, bm=1024, bn=512)
        h2 = rmsnorm(x, norm2_gain[l], BF16)
        act = conv_glu_up(h2, w_up, l, conv_w[l], conv_b[l], seq)
        x = proj_residual(act, w_down, l, x, bm=1024, bn=256, resident_lhs=True)
    out = rmsnorm(x, final_norm_gain, F32)
    return out.reshape(batch, seq, d_model)
```

```python
import functools
import math

import numpy as np
import jax
import jax.numpy as jnp
from jax import lax
from jax.experimental import pallas as pl
from jax.experimental.pallas import tpu as pltpu

F32 = jnp.float32
BF16 = jnp.bfloat16

EPS = 1e-6
N_FOURIER_GROUPS = 4
FOURIER_GROUP_DIM = 256
D_FOURIER = N_FOURIER_GROUPS * FOURIER_GROUP_DIM
ATTN_HEAD_DIM = 128
HEAD_WIDTH = 2 * ATTN_HEAD_DIM
N_REL_BUCKETS = 32
REL_MAX_DISTANCE = 128
LOG2_E = math.log2(math.e)
ATTN_KEY_CHUNK = 512
CONV_WIDTH = 3
BF16_SUBLANES = 16
F32_SUBLANES = 8
PIPELINE_CHUNKS = 16
MXU_DEPTH = 256

MIB = 1024 * 1024


def _params(vmem_mib, semantics):
    return pltpu.CompilerParams(dimension_semantics=semantics, vmem_limit_bytes=vmem_mib * MIB)


def _resident(shape, index_map):
    return pl.BlockSpec(shape, index_map, pipeline_mode=pl.Buffered(1))


def _rmsnorm_kernel(x_ref, g_ref, o_ref):
    x = x_ref[...]
    ms = jnp.mean(x * x, axis=-1, keepdims=True)
    o_ref[...] = (x * lax.rsqrt(ms + EPS) * g_ref[...]).astype(o_ref.dtype)


def rmsnorm(x, gain, out_dtype, bm=256):
    m, d = x.shape
    return pl.pallas_call(
        _rmsnorm_kernel,
        grid=(m // bm,),
        in_specs=[pl.BlockSpec((bm, d), lambda i: (i, 0)),
                  pl.BlockSpec((1, d), lambda i: (0, 0))],
        out_specs=pl.BlockSpec((bm, d), lambda i: (i, 0)),
        out_shape=jax.ShapeDtypeStruct((m, d), out_dtype),
        compiler_params=_params(40, ("parallel",)),
        name="rmsnorm",
    )(x, gain.reshape(1, d))


HEAD_ROWS = 1024


def _cast_dot(lhs_ref, w_ref, wb_ref):
    kc = MXU_DEPTH
    acc = None
    for c in range(w_ref.shape[0] // kc):
        ks = slice(c * kc, (c + 1) * kc)
        wb_ref[ks, :] = w_ref[ks, :].astype(BF16)
        part = jnp.dot(lhs_ref[:, ks], wb_ref[ks, :], preferred_element_type=F32)
        acc = part if c == 0 else acc + part
    return acc


def _tail_steps(i, copy_fn, compute_fn):
    pl.when(i == 0)(copy_fn)
    pl.when(i > 0)(compute_fn)


def _head_cols(i, j, nj):
    return jnp.where(i == 0, j, nj - 1)


def _tail_cols(i, j):
    return jnp.where(i == 0, 0, j)


def _in_proj_head_kernel(h_ref, w_ref, s_ref, o_ref, wb_ref):
    o_ref[...] = (_cast_dot(h_ref, w_ref, wb_ref) * s_ref[...]).astype(o_ref.dtype)


def _in_proj_tail_kernel(h_ref, wb_ref, s_ref, head_ref, o_ref):
    def copy():
        o_ref[...] = head_ref[...]

    def compute():
        acc = jnp.dot(h_ref[...], wb_ref[...], preferred_element_type=F32)
        o_ref[...] = (acc * s_ref[...]).astype(o_ref.dtype)

    _tail_steps(pl.program_id(0), copy, compute)


def in_proj(h, w, l, col_scale, bm=1024, bn_head=512, bn=1024):
    m, k = h.shape
    n = w.shape[2]
    head, wb = pl.pallas_call(
        _in_proj_head_kernel,
        grid=(n // bn_head,),
        in_specs=[_resident((HEAD_ROWS, k), lambda j: (0, 0)),
                  pl.BlockSpec((None, k, bn_head), lambda j: (l, 0, j)),
                  pl.BlockSpec((1, bn_head), lambda j: (0, j))],
        out_specs=[pl.BlockSpec((HEAD_ROWS, bn_head), lambda j: (0, j)),
                   pl.BlockSpec((k, bn_head), lambda j: (0, j))],
        out_shape=[jax.ShapeDtypeStruct((HEAD_ROWS, n), BF16), jax.ShapeDtypeStruct((k, n), BF16)],
        compiler_params=_params(48, ("arbitrary",)),
        name="in_proj_head",
    )(h, w, col_scale)
    assert bm == HEAD_ROWS
    nj = n // bn
    return pl.pallas_call(
        _in_proj_tail_kernel,
        grid=(m // bm, nj),
        in_specs=[pl.BlockSpec((bm, k), lambda i, j: (i, 0)),
                  pl.BlockSpec((k, bn), lambda i, j: (0, _tail_cols(i, j))),
                  pl.BlockSpec((1, bn), lambda i, j: (0, j)),
                  pl.BlockSpec((bm, bn), lambda i, j: (0, _head_cols(i, j, nj)))],
        out_specs=pl.BlockSpec((bm, bn), lambda i, j: (i, j)),
        out_shape=jax.ShapeDtypeStruct((m, n), BF16),
        compiler_params=_params(48, ("parallel", "arbitrary")),
        name="in_proj",
    )(h, wb, col_scale, head)


def _dft_constants(seq, dim):
    def cos_sin(n):
        idx = np.arange(n, dtype=np.int64)
        ang = 2.0 * np.pi * ((idx[:, None] * idx[None, :]) % n).astype(np.float64) / n
        return np.cos(ang), np.sin(ang)
    cs, ss = cos_sin(seq)
    cc, sc = cos_sin(dim)
    chan = np.concatenate([cc, sc], axis=1)
    return (jnp.asarray(cs, BF16), jnp.asarray(-ss, BF16), jnp.asarray(chan, BF16))


def _fourier_chan_kernel(u_ref, c_ref, t1_ref, t2_ref):
    c = c_ref[...]
    gd = FOURIER_GROUP_DIM
    for g in range(N_FOURIER_GROUPS):
        t = jnp.dot(u_ref[:, g * gd:(g + 1) * gd], c, preferred_element_type=F32)
        t1_ref[:, g * gd:(g + 1) * gd] = t[:, :gd].astype(t1_ref.dtype)
        t2_ref[:, g * gd:(g + 1) * gd] = t[:, gd:].astype(t2_ref.dtype)


def fourier_chan(u, chan, bm=1024):
    m = u.shape[0]
    gd = FOURIER_GROUP_DIM
    out = jax.ShapeDtypeStruct((m, D_FOURIER), BF16)
    return pl.pallas_call(
        _fourier_chan_kernel,
        grid=(m // bm,),
        in_specs=[pl.BlockSpec((bm, D_FOURIER), lambda i: (i, 0)),
                  pl.BlockSpec((gd, 2 * gd), lambda i: (0, 0))],
        out_specs=[pl.BlockSpec((bm, D_FOURIER), lambda i: (i, 0)),
                   pl.BlockSpec((bm, D_FOURIER), lambda i: (i, 0))],
        out_shape=[out, out],
        compiler_params=_params(32, ("parallel",)),
        name="fourier_chan",
    )(u, chan)


def _fourier_seq_kernel(cs_ref, sn_ref, t1_ref, t2_ref, o_ref, *, scale):
    acc = jnp.dot(cs_ref[...], t1_ref[...], preferred_element_type=F32)
    acc += jnp.dot(sn_ref[...], t2_ref[...], preferred_element_type=F32)
    o_ref[...] = (acc * scale).astype(o_ref.dtype)


def fourier_seq(cs, sn, t1, t2, seq, bm=512):
    m, n = t1.shape
    nb = m // seq
    ni = seq // bm
    scale = 1.0 / math.sqrt(seq * FOURIER_GROUP_DIM)
    return pl.pallas_call(
        functools.partial(_fourier_seq_kernel, scale=scale),
        grid=(nb, ni),
        in_specs=[pl.BlockSpec((bm, seq), lambda b, i: (i, 0)),
                  pl.BlockSpec((bm, seq), lambda b, i: (i, 0)),
                  pl.BlockSpec((seq, n), lambda b, i: (b, 0)),
                  pl.BlockSpec((seq, n), lambda b, i: (b, 0))],
        out_specs=pl.BlockSpec((bm, n), lambda b, i: (b * ni + i, 0)),
        out_shape=jax.ShapeDtypeStruct((m, n), BF16),
        compiler_params=_params(40, ("parallel", "arbitrary")),
        name="fourier_seq",
    )(cs, sn, t1, t2)


def _bias_window_kernel(tab_ref, o_ref, *, tq, nq):
    h = pl.program_id(0)
    half = N_REL_BUCKETS // 2
    max_exact = half // 2
    width = o_ref.shape[2]
    lo = (nq - 1) * tq - REL_MAX_DISTANCE
    hi = nq * tq + REL_MAX_DISTANCE
    o_ref[0, :, :lo] = jnp.full((tq, lo), tab_ref[half - 1, h] * LOG2_E, F32)
    o_ref[0, :, hi:] = jnp.full((tq, width - hi), tab_ref[N_REL_BUCKETS - 1, h] * LOG2_E, F32)

    shape = (tq, hi - lo)
    i = lax.broadcasted_iota(jnp.int32, shape, 0)
    c = lax.broadcasted_iota(jnp.int32, shape, 1) + lo
    n = i + (nq - 1) * tq - c
    ret = jnp.where(n < 0, half, 0)
    n = jnp.abs(n)
    n2 = n * n
    large = jnp.full(shape, max_exact, jnp.int32)
    for j in range(1, half - max_exact):
        large += (n2 >= (max_exact * max_exact) * (2 ** j)).astype(jnp.int32)
    bucket = ret + jnp.where(n < max_exact, n, large)
    acc = jnp.zeros(shape, F32)
    for b in range(N_REL_BUCKETS):
        acc = jnp.where(bucket == b, tab_ref[b, h] * LOG2_E, acc)
    o_ref[0, :, lo:hi] = acc


def bias_window(rel_table, seq, tq):
    nq = seq // tq
    nh = rel_table.shape[1]
    width = 2 * seq - tq
    assert tq >= REL_MAX_DISTANCE and nq >= 2
    return pl.pallas_call(
        functools.partial(_bias_window_kernel, tq=tq, nq=nq),
        grid=(nh,),
        in_specs=[pl.BlockSpec(memory_space=pltpu.SMEM)],
        out_specs=pl.BlockSpec((1, tq, width), lambda h: (h, 0, 0)),
        out_shape=jax.ShapeDtypeStruct((nh, tq, width), F32),
        compiler_params=_params(40, ("parallel",)),
        name="bias_window",
    )(rel_table)


def _lane_tile_reduce(x, op):
    lanes = 128
    out = x[:, 0:lanes]
    for t in range(1, x.shape[1] // lanes):
        out = op(out, x[:, t * lanes:(t + 1) * lanes])
    return out


def _attn_kernel(q_ref, k_ref, v_ref, wb_ref, lam_ref, sg_ref, o_ref, s0_ref, s1_ref, mx0_ref, mx1_ref,
                 *, lam_init, tq, nq, seq):
    i = pl.program_id(2)
    d = ATTN_HEAD_DIM
    ck = ATTN_KEY_CHUNK
    nc = seq // ck
    s_scr = (s0_ref, s1_ref)
    mx_scr = (mx0_ref, mx1_ref)
    chunks = [(m, c) for m in range(2) for c in range(nc)]

    def logits_chunk(p, m, c, run_max):
        off = pl.multiple_of((nq - 1 - i) * tq + c * ck, tq)
        s = lax.dot_general(q_ref[:, m * d:(m + 1) * d], k_ref[c * ck:(c + 1) * ck, m * d:(m + 1) * d],
                            (((1,), (1,)), ((), ())), preferred_element_type=F32)
        s = s + wb_ref[0, :, pl.ds(off, ck)]
        s_scr[p][m, :, c * ck:(c + 1) * ck] = s
        part = _lane_tile_reduce(s, jnp.maximum)
        run_max = part if c == 0 else jnp.maximum(run_max, part)
        if c == nc - 1:
            mx_scr[p][m] = run_max
        return run_max

    def probs_chunk(p, m, c, state):
        row_max, row_sum, out = state
        if c == 0:
            row_max = jnp.max(mx_scr[p][m], axis=-1, keepdims=True)
        e = jnp.exp2(s_scr[p][m, :, c * ck:(c + 1) * ck] - row_max)
        part = _lane_tile_reduce(e, jnp.add)
        pv = jnp.dot(e.astype(BF16), v_ref[c * ck:(c + 1) * ck, :], preferred_element_type=F32)
        return (row_max, part if c == 0 else row_sum + part, pv if c == 0 else out + pv)

    def finish(results):
        (_, sum0, out0), (_, sum1, out1) = results
        lam_p = lam_ref[...]
        lam = (jnp.exp(jnp.sum(lam_p[0:1] * lam_p[1:2], axis=-1, keepdims=True))
               - jnp.exp(jnp.sum(lam_p[2:3] * lam_p[3:4], axis=-1, keepdims=True)) + lam_init)
        l0 = jnp.sum(sum0, axis=-1, keepdims=True)
        l1 = jnp.sum(sum1, axis=-1, keepdims=True)
        o = out0 * (1.0 / l0) - out1 * (lam / l1)
        ms = jnp.mean(o * o, axis=-1, keepdims=True)
        o = o * lax.rsqrt(ms + EPS) * sg_ref[...] * (1.0 - lam_init)
        o_ref[...] = o.astype(o_ref.dtype)

    def stage(p, do_logits, do_probs):
        run_max = None
        state = (None, None, None)
        results = []
        for m, c in chunks:
            if do_logits:
                run_max = logits_chunk(p, m, c, run_max)
            if do_probs:
                state = probs_chunk(1 - p, m, c, state)
                if c == nc - 1:
                    results.append(state)
        if do_probs:
            finish(results)

    pl.when(i == 0)(lambda: stage(0, True, False))
    steady = jnp.logical_and(i >= 1, i < nq)
    pl.when(jnp.logical_and(steady, i % 2 == 0))(lambda: stage(0, True, True))
    pl.when(jnp.logical_and(steady, i % 2 == 1))(lambda: stage(1, True, True))
    pl.when(i == nq)(lambda: stage(nq % 2, False, True))


def diff_attention(u, wb, lambdas, sub_gain, lam_init, seq, n_heads, tq):
    m = u.shape[0]
    nb = m // seq
    nq = seq // tq
    hw = HEAD_WIDTH
    q_blk = D_FOURIER // hw
    k_blk = q_blk + n_heads
    v_blk = k_blk + n_heads
    assert ATTN_KEY_CHUNK % tq == 0 and seq % ATTN_KEY_CHUNK == 0
    logits = pltpu.VMEM((2, tq, seq), F32)
    maxima = pltpu.VMEM((2, tq, 128), F32)
    return pl.pallas_call(
        functools.partial(_attn_kernel, lam_init=lam_init, tq=tq, nq=nq, seq=seq),
        grid=(nb, n_heads, nq + 1),
        in_specs=[pl.BlockSpec((tq, hw), lambda b, h, i: (b * nq + jnp.minimum(i, nq - 1), q_blk + h)),
                  pl.BlockSpec((seq, hw), lambda b, h, i: (b, k_blk + h)),
                  pl.BlockSpec((seq, hw), lambda b, h, i: (b, v_blk + h)),
                  pl.BlockSpec((1, tq, 2 * seq - tq), lambda b, h, i: (h, 0, 0)),
                  pl.BlockSpec((4, ATTN_HEAD_DIM), lambda b, h, i: (0, 0)),
                  pl.BlockSpec((1, hw), lambda b, h, i: (0, 0))],
        out_specs=pl.BlockSpec((tq, hw), lambda b, h, i: (b * nq + jnp.maximum(i - 1, 0), h)),
        out_shape=jax.ShapeDtypeStruct((m, n_heads * hw), BF16),
        scratch_shapes=[logits, logits, maxima, maxima],
        compiler_params=_params(48, ("parallel", "parallel", "arbitrary")),
        name="diff_attention",
    )(u, u, u, wb, lambdas, sub_gain.reshape(1, hw))


def _mix_combine(g0, g1, y_f, y_a, b0_ref, b1_ref, o_ref):
    g0 = jax.nn.sigmoid(g0 + b0_ref[...])
    g1 = jax.nn.sigmoid(g1 + b1_ref[...])
    o_ref[...] = (g0 * y_f + g1 * y_a).astype(o_ref.dtype)


def _mix_head_kernel(h_ref, f_ref, a_ref, wg0_ref, wg1_ref, wf_ref, wa_ref, b0_ref, b1_ref,
                     o_ref, wg0b_ref, wg1b_ref, wfb_ref, wab_ref):
    _mix_combine(_cast_dot(h_ref, wg0_ref, wg0b_ref), _cast_dot(h_ref, wg1_ref, wg1b_ref),
                 _cast_dot(f_ref, wf_ref, wfb_ref), _cast_dot(a_ref, wa_ref, wab_ref), b0_ref, b1_ref, o_ref)


def _mix_tail_kernel(h_ref, f_ref, a_ref, wg0_ref, wg1_ref, wf_ref, wa_ref, b0_ref, b1_ref, head_ref, o_ref):
    def copy():
        o_ref[...] = head_ref[...]

    def compute():
        h = h_ref[...]
        dot = functools.partial(jnp.dot, preferred_element_type=F32)
        _mix_combine(dot(h, wg0_ref[...]), dot(h, wg1_ref[...]), dot(f_ref[...], wf_ref[...]),
                     dot(a_ref[...], wa_ref[...]), b0_ref, b1_ref, o_ref)

    _tail_steps(pl.program_id(0), copy, compute)


def gated_mix(h, f, a, w_gate, w_f, w_a, l, b_gate, bm=1024, bn=256):
    m, d = h.shape
    df, da = f.shape[1], a.shape[1]
    nj = d // bn
    b_gate = b_gate.reshape(1, 2 * d)
    head, wg0b, wg1b, wfb, wab = pl.pallas_call(
        _mix_head_kernel,
        grid=(nj,),
        in_specs=[_resident((HEAD_ROWS, d), lambda j: (0, 0)),
                  _resident((HEAD_ROWS, df), lambda j: (0, 0)),
                  _resident((HEAD_ROWS, da), lambda j: (0, 0)),
                  pl.BlockSpec((None, d, bn), lambda j: (l, 0, j)),
                  pl.BlockSpec((None, d, bn), lambda j: (l, 0, j + nj)),
                  pl.BlockSpec((None, df, bn), lambda j: (l, 0, j)),
                  pl.BlockSpec((None, da, bn), lambda j: (l, 0, j)),
                  pl.BlockSpec((1, bn), lambda j: (0, j)),
                  pl.BlockSpec((1, bn), lambda j: (0, j + nj))],
        out_specs=[pl.BlockSpec((HEAD_ROWS, bn), lambda j: (0, j)),
                   pl.BlockSpec((d, bn), lambda j: (0, j)),
                   pl.BlockSpec((d, bn), lambda j: (0, j)),
                   pl.BlockSpec((df, bn), lambda j: (0, j)),
                   pl.BlockSpec((da, bn), lambda j: (0, j))],
        out_shape=[jax.ShapeDtypeStruct((HEAD_ROWS, d), BF16), jax.ShapeDtypeStruct((d, d), BF16),
                   jax.ShapeDtypeStruct((d, d), BF16), jax.ShapeDtypeStruct((df, d), BF16),
                   jax.ShapeDtypeStruct((da, d), BF16)],
        compiler_params=_params(56, ("arbitrary",)),
        name="gated_mix_head",
    )(h, f, a, w_gate, w_gate, w_f, w_a, b_gate, b_gate)
    assert bm == HEAD_ROWS
    wcol = lambda i, j: (0, _tail_cols(i, j))
    return pl.pallas_call(
        _mix_tail_kernel,
        grid=(m // bm, nj),
        in_specs=[pl.BlockSpec((bm, d), lambda i, j: (i, 0)),
                  pl.BlockSpec((bm, df), lambda i, j: (i, 0)),
                  pl.BlockSpec((bm, da), lambda i, j: (i, 0)),
                  pl.BlockSpec((d, bn), wcol),
                  pl.BlockSpec((d, bn), wcol),
                  pl.BlockSpec((df, bn), wcol),
                  pl.BlockSpec((da, bn), wcol),
                  pl.BlockSpec((1, bn), lambda i, j: (0, j)),
                  pl.BlockSpec((1, bn), lambda i, j: (0, j + nj)),
                  pl.BlockSpec((bm, bn), lambda i, j: (0, _head_cols(i, j, nj)))],
        out_specs=pl.BlockSpec((bm, bn), lambda i, j: (i, j)),
        out_shape=jax.ShapeDtypeStruct((m, d), BF16),
        compiler_params=_params(56, ("parallel", "arbitrary")),
        name="gated_mix",
    )(h, f, a, wg0b, wg1b, wfb, wab, b_gate, b_gate, head)


def _proj_residual_head_kernel(a_ref, w_ref, r_ref, o_ref, wb_ref):
    o_ref[...] = r_ref[...] + _cast_dot(a_ref, w_ref, wb_ref)


def _proj_residual_tail_kernel(a_ref, wb_ref, r_ref, head_ref, o_ref):
    def copy():
        o_ref[...] = head_ref[...]

    def compute():
        o_ref[...] = r_ref[...] + jnp.dot(a_ref[...], wb_ref[...], preferred_element_type=F32)

    _tail_steps(pl.program_id(0), copy, compute)


def proj_residual(a, w, l, res, bn_head, bm, bn):
    m, k = a.shape
    n = w.shape[2]
    head, wb = pl.pallas_call(
        _proj_residual_head_kernel,
        grid=(n // bn_head,),
        in_specs=[_resident((HEAD_ROWS, k), lambda j: (0, 0)),
                  pl.BlockSpec((None, k, bn_head), lambda j: (l, 0, j)),
                  pl.BlockSpec((HEAD_ROWS, bn_head), lambda j: (0, j))],
        out_specs=[pl.BlockSpec((HEAD_ROWS, bn_head), lambda j: (0, j)),
                   pl.BlockSpec((k, bn_head), lambda j: (0, j))],
        out_shape=[jax.ShapeDtypeStruct((HEAD_ROWS, n), F32), jax.ShapeDtypeStruct((k, n), BF16)],
        compiler_params=_params(56, ("arbitrary",)),
        name="proj_residual_head",
    )(a, w, res)
    assert bm == HEAD_ROWS
    nj = n // bn
    return pl.pallas_call(
        _proj_residual_tail_kernel,
        grid=(m // bm, nj),
        in_specs=[pl.BlockSpec((bm, k), lambda i, j: (i, 0)),
                  pl.BlockSpec((k, bn), lambda i, j: (0, _tail_cols(i, j))),
                  pl.BlockSpec((bm, bn), lambda i, j: (i, j)),
                  pl.BlockSpec((bm, bn), lambda i, j: (0, _head_cols(i, j, nj)))],
        out_specs=pl.BlockSpec((bm, bn), lambda i, j: (i, j)),
        out_shape=jax.ShapeDtypeStruct((m, n), F32),
        compiler_params=_params(56, ("parallel", "arbitrary")),
        name="proj_residual",
    )(a, wb, res, head)


def _gelu_exact(x):
    return 0.5 * x * (1.0 + lax.erf(x * math.sqrt(0.5)))


def _build_halo_tile(ext_ref, h_ref, hp_ref, hn_ref):
    bm = h_ref.shape[0]
    halo = BF16_SUBLANES
    ext_ref[0:halo, :] = hp_ref[...]
    ext_ref[halo:halo + bm, :] = h_ref[...]
    ext_ref[halo + bm:, :] = hn_ref[...]


def _conv_glu_matmul_chunk(ext_ref, wg_ref, wv_ref, c, acc):
    halo = BF16_SUBLANES
    bm = ext_ref.shape[0] - 2 * halo
    kc = ext_ref.shape[1] // PIPELINE_CHUNKS
    ks = slice(c * kc, (c + 1) * kc)
    gd = jnp.dot(ext_ref[:, ks], wg_ref[ks, :], preferred_element_type=F32)
    vd = jnp.dot(ext_ref[halo:halo + bm, ks], wv_ref[ks, :], preferred_element_type=F32)
    return (gd, vd) if c == 0 else (acc[0] + gd, acc[1] + vd)


def _conv_glu_epilogue_chunk(gate_ref, val_ref, cw_ref, cb_ref, o_ref, c, row0, seq):
    halo = BF16_SUBLANES
    pad = F32_SUBLANES
    bm = val_ref.shape[0]
    rc = bm // PIPELINE_CHUNKS
    r0 = halo + c * rc
    g = gate_ref[r0 - pad:r0 + rc + pad, :]
    row = lax.broadcasted_iota(jnp.int32, (rc, 1), 0) + c * rc
    pos = (row0 + row) % seq
    g_prev = jnp.where(pos == 0, 0.0, pltpu.roll(g, 1, axis=0)[pad:pad + rc])
    g_next = jnp.where(pos == seq - 1, 0.0, pltpu.roll(g, rc + 2 * pad - 1, axis=0)[pad:pad + rc])
    cw = cw_ref[...]
    conv = g_prev * cw[0:1] + g[pad:pad + rc] * cw[1:2] + g_next * cw[2:3] + cb_ref[...]
    rs = slice(c * rc, (c + 1) * rc)
    o_ref[rs, :] = (_gelu_exact(conv) * val_ref[rs, :]).astype(o_ref.dtype)


def _conv_glu_head_kernel(h_ref, hp_ref, hn_ref, wg_ref, wv_ref, cw_ref, cb_ref, o_ref, wgo_ref, wvo_ref,
                          ext_ref, wgb0, wgb1, wvb0, wvb1, gate0, gate1, val0, val1, *, seq, nj):
    j = pl.program_id(0)
    kc = h_ref.shape[1] // PIPELINE_CHUNKS
    wgb, wvb, gate_s, val_s = (wgb0, wgb1), (wvb0, wvb1), (gate0, gate1), (val0, val1)

    pl.when(j == 0)(lambda: _build_halo_tile(ext_ref, h_ref, hp_ref, hn_ref))

    def stage(p, do_cast, do_matmul, do_epilogue):
        q = 1 - p
        acc = None
        for c in range(PIPELINE_CHUNKS):
            if do_matmul:
                acc = _conv_glu_matmul_chunk(ext_ref, wgb[q], wvb[q], c, acc)
            if do_cast:
                ks = slice(c * kc, (c + 1) * kc)
                wg = wg_ref[ks, :].astype(BF16)
                wv = wv_ref[ks, :].astype(BF16)
                wgb[p][ks, :] = wg
                wvb[p][ks, :] = wv
                wgo_ref[ks, :] = wg
                wvo_ref[ks, :] = wv
            if do_epilogue:
                _conv_glu_epilogue_chunk(gate_s[p], val_s[p], cw_ref, cb_ref, o_ref, c, 0, seq)
        if do_matmul:
            gate_s[q][...], val_s[q][...] = acc

    pl.when(j == 0)(lambda: stage(0, True, False, False))
    pl.when(j == 1)(lambda: stage(1, True, True, False))
    steady = jnp.logical_and(j >= 2, j < nj)
    pl.when(jnp.logical_and(steady, j % 2 == 0))(lambda: stage(0, True, True, True))
    pl.when(jnp.logical_and(steady, j % 2 == 1))(lambda: stage(1, True, True, True))
    pl.when(j == nj)(lambda: stage(nj % 2, False, True, True))
    pl.when(j == nj + 1)(lambda: stage((nj + 1) % 2, False, False, True))


def _conv_glu_tail_kernel(h_ref, hp_ref, hn_ref, wg_ref, wv_ref, cw_ref, cb_ref, head_ref, o_ref,
                          ext_ref, gate0, gate1, val0, val1, *, seq, nj):
    i = pl.program_id(0)
    j = pl.program_id(1)
    bm = h_ref.shape[0]
    gate_s, val_s = (gate0, gate1), (val0, val1)

    pl.when(jnp.logical_and(i > 0, j == 0))(lambda: _build_halo_tile(ext_ref, h_ref, hp_ref, hn_ref))

    def copy():
        o_ref[...] = head_ref[...]

    def stage(p, do_matmul, do_epilogue):
        q = 1 - p
        acc = None
        for c in range(PIPELINE_CHUNKS):
            if do_matmul:
                acc = _conv_glu_matmul_chunk(ext_ref, wg_ref, wv_ref, c, acc)
            if do_epilogue:
                _conv_glu_epilogue_chunk(gate_s[q], val_s[q], cw_ref, cb_ref, o_ref, c, i * bm, seq)
        if do_matmul:
            gate_s[p][...], val_s[p][...] = acc

    pl.when(jnp.logical_and(i == 0, j >= 1))(copy)
    compute = i > 0
    pl.when(jnp.logical_and(compute, j == 0))(lambda: stage(0, True, False))
    steady = jnp.logical_and(compute, jnp.logical_and(j >= 1, j < nj))
    pl.when(jnp.logical_and(steady, j % 2 == 0))(lambda: stage(0, True, True))
    pl.when(jnp.logical_and(steady, j % 2 == 1))(lambda: stage(1, True, True))
    pl.when(jnp.logical_and(compute, j == nj))(lambda: stage(nj % 2, False, True))


def conv_glu_up(h, w_up, l, conv_w, conv_b, seq, bm=1024, bn_head=256, bn=256):
    m, k = h.shape
    dff = w_up.shape[2] // 2
    halo = BF16_SUBLANES
    rb = bm // halo
    last = m // halo - 1
    conv_b = conv_b.reshape(1, dff)
    assert bm == HEAD_ROWS
    prev_tile = lambda i: jnp.maximum(i * rb - 1, 0)
    next_tile = lambda i: jnp.minimum((i + 1) * rb, last)
    ext = pltpu.VMEM((bm + 2 * halo, k), BF16)

    nj = dff // bn_head
    assert nj >= 2
    w_tile = lambda j: jnp.minimum(j, nj - 1)
    o_tile = lambda j: jnp.clip(j - 2, 0, nj - 1)
    wb = pltpu.VMEM((k, bn_head), BF16)
    gate = pltpu.VMEM((bm + 2 * halo, bn_head), F32)
    val = pltpu.VMEM((bm, bn_head), F32)
    wb_out = jax.ShapeDtypeStruct((k, dff), BF16)
    head, wgb, wvb = pl.pallas_call(
        functools.partial(_conv_glu_head_kernel, seq=seq, nj=nj),
        grid=(nj + 2,),
        in_specs=[_resident((bm, k), lambda j: (0, 0)),
                  _resident((halo, k), lambda j: (prev_tile(0), 0)),
                  _resident((halo, k), lambda j: (next_tile(0), 0)),
                  pl.BlockSpec((None, k, bn_head), lambda j: (l, 0, w_tile(j))),
                  pl.BlockSpec((None, k, bn_head), lambda j: (l, 0, w_tile(j) + nj)),
                  pl.BlockSpec((CONV_WIDTH, bn_head), lambda j: (0, o_tile(j))),
                  pl.BlockSpec((1, bn_head), lambda j: (0, o_tile(j)))],
        out_specs=[pl.BlockSpec((bm, bn_head), lambda j: (0, o_tile(j))),
                   pl.BlockSpec((k, bn_head), lambda j: (0, w_tile(j))),
                   pl.BlockSpec((k, bn_head), lambda j: (0, w_tile(j)))],
        out_shape=[jax.ShapeDtypeStruct((bm, dff), BF16), wb_out, wb_out],
        scratch_shapes=[ext, wb, wb, wb, wb, gate, gate, val, val],
        compiler_params=_params(56, ("arbitrary",)),
        name="conv_glu_up_head",
    )(h, h, h, w_up, w_up, conv_w, conv_b)

    nj = dff // bn
    w_tile = lambda i, j: jnp.where(i == 0, 0, jnp.minimum(j, nj - 1))
    o_tile = lambda j: jnp.clip(j - 1, 0, nj - 1)
    gate = pltpu.VMEM((bm + 2 * halo, bn), F32)
    val = pltpu.VMEM((bm, bn), F32)
    return pl.pallas_call(
        functools.partial(_conv_glu_tail_kernel, seq=seq, nj=nj),
        grid=(m // bm, nj + 1),
        in_specs=[_resident((bm, k), lambda i, j: (i, 0)),
                  _resident((halo, k), lambda i, j: (prev_tile(i), 0)),
                  _resident((halo, k), lambda i, j: (next_tile(i), 0)),
                  pl.BlockSpec((k, bn), lambda i, j: (0, w_tile(i, j))),
                  pl.BlockSpec((k, bn), lambda i, j: (0, w_tile(i, j))),
                  pl.BlockSpec((CONV_WIDTH, bn), lambda i, j: (0, o_tile(j))),
                  pl.BlockSpec((1, bn), lambda i, j: (0, o_tile(j))),
                  pl.BlockSpec((bm, bn), lambda i, j: (0, jnp.where(i == 0, o_tile(j), nj - 1)))],
        out_specs=pl.BlockSpec((bm, bn), lambda i, j: (i, o_tile(j))),
        out_shape=jax.ShapeDtypeStruct((m, dff), BF16),
        scratch_shapes=[ext, gate, gate, val, val],
        compiler_params=_params(56, ("parallel", "arbitrary")),
        name="conv_glu_up",
    )(h, h, h, wgb, wvb, conv_w, conv_b, head)


def kernel(x, norm1_gain, w_in, w_fourier_out, lambdas, subln_gain, rel_bias_table, w_attn_out,
           w_gate, b_gate, w_o, norm2_gain, w_up, conv_w, conv_b, w_down, final_norm_gain):
    batch, seq, d_model = x.shape
    depth = w_in.shape[0]
    n_heads = rel_bias_table.shape[1]
    d_attn = n_heads * HEAD_WIDTH
    d_in = w_in.shape[2]
    tq = 256

    cs, sn, chan = _dft_constants(seq, FOURIER_GROUP_DIM)
    col = np.ones((1, d_in), np.float32)
    col[:, D_FOURIER:D_FOURIER + d_attn] = ATTN_HEAD_DIM ** -0.5 * LOG2_E
    col_scale = jnp.asarray(col)
    wb = bias_window(rel_bias_table, seq, tq)

    x = x.reshape(batch * seq, d_model)
    for l in range(depth):
        lam_init = 0.8 - 0.6 * math.exp(-0.3 * l)
        h = rmsnorm(x, norm1_gain[l], BF16)
        u = in_proj(h, w_in, l, col_scale)
        t1, t2 = fourier_chan(u, chan)
        f = fourier_seq(cs, sn, t1, t2, seq)
        a = diff_attention(u, wb, lambdas[l], subln_gain[l], lam_init, seq, n_heads, tq)
        mixed = gated_mix(h, f, a, w_gate, w_fourier_out, w_attn_out, l, b_gate[l])
        x = proj_residual(mixed, w_o, l, x, bn_head=512, bm=1024, bn=512)
        h2 = rmsnorm(x, norm2_gain[l], BF16)
        act = conv_glu_up(h2, w_up, l, conv_w[l], conv_b[l], seq)
        x = proj_residual(act, w_down, l, x, bn_head=256, bm=1024, bn=256)
    out = rmsnorm(x, final_norm_gain, F32)
    return out.reshape(batch, seq, d_model)
```

```python
import functools
import math

import numpy as np
import jax
import jax.numpy as jnp
from jax import lax
from jax.experimental import pallas as pl
from jax.experimental.pallas import tpu as pltpu

F32 = jnp.float32
BF16 = jnp.bfloat16

EPS = 1e-6
N_FOURIER_GROUPS = 4
FOURIER_GROUP_DIM = 256
D_FOURIER = N_FOURIER_GROUPS * FOURIER_GROUP_DIM
ATTN_HEAD_DIM = 128
HEAD_WIDTH = 2 * ATTN_HEAD_DIM
N_REL_BUCKETS = 32
REL_MAX_DISTANCE = 128
LOG2_E = math.log2(math.e)
ATTN_KEY_CHUNK = 512
CONV_WIDTH = 3
BF16_SUBLANES = 16
F32_SUBLANES = 8
PIPELINE_CHUNKS = 16
MXU_DEPTH = 256

MIB = 1024 * 1024


def _params(vmem_mib, semantics):
    return pltpu.CompilerParams(dimension_semantics=semantics, vmem_limit_bytes=vmem_mib * MIB)


def _resident(shape, index_map):
    return pl.BlockSpec(shape, index_map, pipeline_mode=pl.Buffered(1))


def _rmsnorm_kernel(x_ref, g_ref, o_ref):
    x = x_ref[...]
    ms = jnp.mean(x * x, axis=-1, keepdims=True)
    o_ref[...] = (x * lax.rsqrt(ms + EPS) * g_ref[...]).astype(o_ref.dtype)


def rmsnorm(x, gain, out_dtype, bm=256):
    m, d = x.shape
    return pl.pallas_call(
        _rmsnorm_kernel,
        grid=(m // bm,),
        in_specs=[pl.BlockSpec((bm, d), lambda i: (i, 0)),
                  pl.BlockSpec((1, d), lambda i: (0, 0))],
        out_specs=pl.BlockSpec((bm, d), lambda i: (i, 0)),
        out_shape=jax.ShapeDtypeStruct((m, d), out_dtype),
        compiler_params=_params(40, ("parallel",)),
        name="rmsnorm",
    )(x, gain.reshape(1, d))


HEAD_ROWS = 1024
_ALIASED = pl.BlockSpec(memory_space=pl.ANY)


def _cast_dot(lhs_ref, w_ref, wb_ref):
    kc = MXU_DEPTH
    acc = None
    for c in range(w_ref.shape[0] // kc):
        ks = slice(c * kc, (c + 1) * kc)
        wb_ref[ks, :] = w_ref[ks, :].astype(BF16)
        part = jnp.dot(lhs_ref[:, ks], wb_ref[ks, :], preferred_element_type=F32)
        acc = part if c == 0 else acc + part
    return acc


def _in_proj_head_kernel(h_ref, w_ref, s_ref, buf_ref, o_ref, wb_ref):
    o_ref[...] = (_cast_dot(h_ref, w_ref, wb_ref) * s_ref[...]).astype(o_ref.dtype)


def _in_proj_tail_kernel(h_ref, wb_ref, s_ref, buf_ref, o_ref):
    acc = jnp.dot(h_ref[...], wb_ref[...], preferred_element_type=F32)
    o_ref[...] = (acc * s_ref[...]).astype(o_ref.dtype)


def in_proj(h, w, l, col_scale, buf, bm=1024, bn_head=512, bn=1024):
    m, k = h.shape
    n = w.shape[2]
    assert buf.shape == (m, n) and buf.dtype == BF16
    buf, wb = pl.pallas_call(
        _in_proj_head_kernel,
        grid=(n // bn_head,),
        in_specs=[_resident((HEAD_ROWS, k), lambda j: (0, 0)),
                  pl.BlockSpec((None, k, bn_head), lambda j: (l, 0, j)),
                  pl.BlockSpec((1, bn_head), lambda j: (0, j)),
                  _ALIASED],
        out_specs=[pl.BlockSpec((HEAD_ROWS, bn_head), lambda j: (0, j)),
                   pl.BlockSpec((k, bn_head), lambda j: (0, j))],
        out_shape=[jax.ShapeDtypeStruct((m, n), BF16), jax.ShapeDtypeStruct((k, n), BF16)],
        input_output_aliases={3: 0},
        compiler_params=_params(48, ("arbitrary",)),
        name="in_proj_head",
    )(h, w, col_scale, buf)
    assert bm == HEAD_ROWS
    return pl.pallas_call(
        _in_proj_tail_kernel,
        grid=(m // bm - 1, n // bn),
        in_specs=[pl.BlockSpec((bm, k), lambda i, j: (i + 1, 0)),
                  pl.BlockSpec((k, bn), lambda i, j: (0, j)),
                  pl.BlockSpec((1, bn), lambda i, j: (0, j)),
                  _ALIASED],
        out_specs=pl.BlockSpec((bm, bn), lambda i, j: (i + 1, j)),
        out_shape=jax.ShapeDtypeStruct((m, n), BF16),
        input_output_aliases={3: 0},
        compiler_params=_params(48, ("parallel", "arbitrary")),
        name="in_proj",
    )(h, wb, col_scale, buf)


def _dft_constants(seq, dim):
    def cos_sin(n):
        idx = np.arange(n, dtype=np.int64)
        ang = 2.0 * np.pi * ((idx[:, None] * idx[None, :]) % n).astype(np.float64) / n
        return np.cos(ang), np.sin(ang)
    cs, ss = cos_sin(seq)
    cc, sc = cos_sin(dim)
    chan = np.concatenate([cc, sc], axis=1)
    return (jnp.asarray(cs, BF16), jnp.asarray(-ss, BF16), jnp.asarray(chan, BF16))


def _fourier_chan_kernel(u_ref, c_ref, t1_ref, t2_ref):
    c = c_ref[...]
    gd = FOURIER_GROUP_DIM
    for g in range(N_FOURIER_GROUPS):
        t = jnp.dot(u_ref[:, g * gd:(g + 1) * gd], c, preferred_element_type=F32)
        t1_ref[:, g * gd:(g + 1) * gd] = t[:, :gd].astype(t1_ref.dtype)
        t2_ref[:, g * gd:(g + 1) * gd] = t[:, gd:].astype(t2_ref.dtype)


def fourier_chan(u, chan, bm=1024):
    m = u.shape[0]
    gd = FOURIER_GROUP_DIM
    out = jax.ShapeDtypeStruct((m, D_FOURIER), BF16)
    return pl.pallas_call(
        _fourier_chan_kernel,
        grid=(m // bm,),
        in_specs=[pl.BlockSpec((bm, D_FOURIER), lambda i: (i, 0)),
                  pl.BlockSpec((gd, 2 * gd), lambda i: (0, 0))],
        out_specs=[pl.BlockSpec((bm, D_FOURIER), lambda i: (i, 0)),
                   pl.BlockSpec((bm, D_FOURIER), lambda i: (i, 0))],
        out_shape=[out, out],
        compiler_params=_params(32, ("parallel",)),
        name="fourier_chan",
    )(u, chan)


def _fourier_seq_kernel(cs_ref, sn_ref, t1_ref, t2_ref, o_ref, *, scale):
    acc = jnp.dot(cs_ref[...], t1_ref[...], preferred_element_type=F32)
    acc += jnp.dot(sn_ref[...], t2_ref[...], preferred_element_type=F32)
    o_ref[...] = (acc * scale).astype(o_ref.dtype)


def fourier_seq(cs, sn, t1, t2, seq, bm=512):
    m, n = t1.shape
    nb = m // seq
    ni = seq // bm
    scale = 1.0 / math.sqrt(seq * FOURIER_GROUP_DIM)
    return pl.pallas_call(
        functools.partial(_fourier_seq_kernel, scale=scale),
        grid=(nb, ni),
        in_specs=[pl.BlockSpec((bm, seq), lambda b, i: (i, 0)),
                  pl.BlockSpec((bm, seq), lambda b, i: (i, 0)),
                  pl.BlockSpec((seq, n), lambda b, i: (b, 0)),
                  pl.BlockSpec((seq, n), lambda b, i: (b, 0))],
        out_specs=pl.BlockSpec((bm, n), lambda b, i: (b * ni + i, 0)),
        out_shape=jax.ShapeDtypeStruct((m, n), BF16),
        compiler_params=_params(40, ("parallel", "arbitrary")),
        name="fourier_seq",
    )(cs, sn, t1, t2)


def _bias_window_kernel(tab_ref, o_ref, *, tq, nq):
    h = pl.program_id(0)
    half = N_REL_BUCKETS // 2
    max_exact = half // 2
    width = o_ref.shape[2]
    lo = (nq - 1) * tq - REL_MAX_DISTANCE
    hi = nq * tq + REL_MAX_DISTANCE
    o_ref[0, :, :lo] = jnp.full((tq, lo), tab_ref[half - 1, h] * LOG2_E, F32)
    o_ref[0, :, hi:] = jnp.full((tq, width - hi), tab_ref[N_REL_BUCKETS - 1, h] * LOG2_E, F32)

    shape = (tq, hi - lo)
    i = lax.broadcasted_iota(jnp.int32, shape, 0)
    c = lax.broadcasted_iota(jnp.int32, shape, 1) + lo
    n = i + (nq - 1) * tq - c
    ret = jnp.where(n < 0, half, 0)
    n = jnp.abs(n)
    n2 = n * n
    large = jnp.full(shape, max_exact, jnp.int32)
    for j in range(1, half - max_exact):
        large += (n2 >= (max_exact * max_exact) * (2 ** j)).astype(jnp.int32)
    bucket = ret + jnp.where(n < max_exact, n, large)
    acc = jnp.zeros(shape, F32)
    for b in range(N_REL_BUCKETS):
        acc = jnp.where(bucket == b, tab_ref[b, h] * LOG2_E, acc)
    o_ref[0, :, lo:hi] = acc


def bias_window(rel_table, seq, tq):
    nq = seq // tq
    nh = rel_table.shape[1]
    width = 2 * seq - tq
    assert tq >= REL_MAX_DISTANCE and nq >= 2
    return pl.pallas_call(
        functools.partial(_bias_window_kernel, tq=tq, nq=nq),
        grid=(nh,),
        in_specs=[pl.BlockSpec(memory_space=pltpu.SMEM)],
        out_specs=pl.BlockSpec((1, tq, width), lambda h: (h, 0, 0)),
        out_shape=jax.ShapeDtypeStruct((nh, tq, width), F32),
        compiler_params=_params(40, ("parallel",)),
        name="bias_window",
    )(rel_table)


def _lane_tile_reduce(x, op):
    lanes = 128
    out = x[:, 0:lanes]
    for t in range(1, x.shape[1] // lanes):
        out = op(out, x[:, t * lanes:(t + 1) * lanes])
    return out


def _attn_kernel(q_ref, k_ref, v_ref, wb_ref, lam_ref, sg_ref, o_ref, s0_ref, s1_ref, mx0_ref, mx1_ref,
                 *, lam_init, tq, nq, seq):
    i = pl.program_id(2)
    d = ATTN_HEAD_DIM
    ck = ATTN_KEY_CHUNK
    nc = seq // ck
    s_scr = (s0_ref, s1_ref)
    mx_scr = (mx0_ref, mx1_ref)
    chunks = [(m, c) for m in range(2) for c in range(nc)]

    def logits_chunk(p, m, c, run_max):
        off = pl.multiple_of((nq - 1 - i) * tq + c * ck, tq)
        s = lax.dot_general(q_ref[:, m * d:(m + 1) * d], k_ref[c * ck:(c + 1) * ck, m * d:(m + 1) * d],
                            (((1,), (1,)), ((), ())), preferred_element_type=F32)
        s = s + wb_ref[0, :, pl.ds(off, ck)]
        s_scr[p][m, :, c * ck:(c + 1) * ck] = s
        part = _lane_tile_reduce(s, jnp.maximum)
        run_max = part if c == 0 else jnp.maximum(run_max, part)
        if c == nc - 1:
            mx_scr[p][m] = run_max
        return run_max

    def probs_chunk(p, m, c, state):
        row_max, row_sum, out = state
        if c == 0:
            row_max = jnp.max(mx_scr[p][m], axis=-1, keepdims=True)
        e = jnp.exp2(s_scr[p][m, :, c * ck:(c + 1) * ck] - row_max)
        part = _lane_tile_reduce(e, jnp.add)
        pv = jnp.dot(e.astype(BF16), v_ref[c * ck:(c + 1) * ck, :], preferred_element_type=F32)
        return (row_max, part if c == 0 else row_sum + part, pv if c == 0 else out + pv)

    def finish(results):
        (_, sum0, out0), (_, sum1, out1) = results
        lam_p = lam_ref[...]
        lam = (jnp.exp(jnp.sum(lam_p[0:1] * lam_p[1:2], axis=-1, keepdims=True))
               - jnp.exp(jnp.sum(lam_p[2:3] * lam_p[3:4], axis=-1, keepdims=True)) + lam_init)
        l0 = jnp.sum(sum0, axis=-1, keepdims=True)
        l1 = jnp.sum(sum1, axis=-1, keepdims=True)
        o = out0 * (1.0 / l0) - out1 * (lam / l1)
        ms = jnp.mean(o * o, axis=-1, keepdims=True)
        o = o * lax.rsqrt(ms + EPS) * sg_ref[...] * (1.0 - lam_init)
        o_ref[...] = o.astype(o_ref.dtype)

    def stage(p, do_logits, do_probs):
        run_max = None
        state = (None, None, None)
        results = []
        for m, c in chunks:
            if do_logits:
                run_max = logits_chunk(p, m, c, run_max)
            if do_probs:
                state = probs_chunk(1 - p, m, c, state)
                if c == nc - 1:
                    results.append(state)
        if do_probs:
            finish(results)

    pl.when(i == 0)(lambda: stage(0, True, False))
    steady = jnp.logical_and(i >= 1, i < nq)
    pl.when(jnp.logical_and(steady, i % 2 == 0))(lambda: stage(0, True, True))
    pl.when(jnp.logical_and(steady, i % 2 == 1))(lambda: stage(1, True, True))
    pl.when(i == nq)(lambda: stage(nq % 2, False, True))


def diff_attention(u, wb, lambdas, sub_gain, lam_init, seq, n_heads, tq):
    m = u.shape[0]
    nb = m // seq
    nq = seq // tq
    hw = HEAD_WIDTH
    q_blk = D_FOURIER // hw
    k_blk = q_blk + n_heads
    v_blk = k_blk + n_heads
    assert ATTN_KEY_CHUNK % tq == 0 and seq % ATTN_KEY_CHUNK == 0
    logits = pltpu.VMEM((2, tq, seq), F32)
    maxima = pltpu.VMEM((2, tq, 128), F32)
    return pl.pallas_call(
        functools.partial(_attn_kernel, lam_init=lam_init, tq=tq, nq=nq, seq=seq),
        grid=(nb, n_heads, nq + 1),
        in_specs=[pl.BlockSpec((tq, hw), lambda b, h, i: (b * nq + jnp.minimum(i, nq - 1), q_blk + h)),
                  pl.BlockSpec((seq, hw), lambda b, h, i: (b, k_blk + h)),
                  pl.BlockSpec((seq, hw), lambda b, h, i: (b, v_blk + h)),
                  pl.BlockSpec((1, tq, 2 * seq - tq), lambda b, h, i: (h, 0, 0)),
                  pl.BlockSpec((4, ATTN_HEAD_DIM), lambda b, h, i: (0, 0)),
                  pl.BlockSpec((1, hw), lambda b, h, i: (0, 0))],
        out_specs=pl.BlockSpec((tq, hw), lambda b, h, i: (b * nq + jnp.maximum(i - 1, 0), h)),
        out_shape=jax.ShapeDtypeStruct((m, n_heads * hw), BF16),
        scratch_shapes=[logits, logits, maxima, maxima],
        compiler_params=_params(48, ("parallel", "parallel", "arbitrary")),
        name="diff_attention",
    )(u, u, u, wb, lambdas, sub_gain.reshape(1, hw))


def _mix_combine(g0, g1, y_f, y_a, b0_ref, b1_ref, o_ref):
    g0 = jax.nn.sigmoid(g0 + b0_ref[...])
    g1 = jax.nn.sigmoid(g1 + b1_ref[...])
    o_ref[...] = (g0 * y_f + g1 * y_a).astype(o_ref.dtype)


def _mix_head_kernel(h_ref, f_ref, a_ref, wg0_ref, wg1_ref, wf_ref, wa_ref, b0_ref, b1_ref, buf_ref,
                     o_ref, wg0b_ref, wg1b_ref, wfb_ref, wab_ref):
    _mix_combine(_cast_dot(h_ref, wg0_ref, wg0b_ref), _cast_dot(h_ref, wg1_ref, wg1b_ref),
                 _cast_dot(f_ref, wf_ref, wfb_ref), _cast_dot(a_ref, wa_ref, wab_ref), b0_ref, b1_ref, o_ref)


def _mix_tail_kernel(h_ref, f_ref, a_ref, wg0_ref, wg1_ref, wf_ref, wa_ref, b0_ref, b1_ref, buf_ref, o_ref):
    h = h_ref[...]
    dot = functools.partial(jnp.dot, preferred_element_type=F32)
    _mix_combine(dot(h, wg0_ref[...]), dot(h, wg1_ref[...]), dot(f_ref[...], wf_ref[...]),
                 dot(a_ref[...], wa_ref[...]), b0_ref, b1_ref, o_ref)


def gated_mix(h, f, a, w_gate, w_f, w_a, l, b_gate, buf, bm=1024, bn=256):
    m, d = h.shape
    df, da = f.shape[1], a.shape[1]
    nj = d // bn
    b_gate = b_gate.reshape(1, 2 * d)
    assert buf.shape == (m, d) and buf.dtype == BF16
    buf, wg0b, wg1b, wfb, wab = pl.pallas_call(
        _mix_head_kernel,
        grid=(nj,),
        in_specs=[_resident((HEAD_ROWS, d), lambda j: (0, 0)),
                  _resident((HEAD_ROWS, df), lambda j: (0, 0)),
                  _resident((HEAD_ROWS, da), lambda j: (0, 0)),
                  pl.BlockSpec((None, d, bn), lambda j: (l, 0, j)),
                  pl.BlockSpec((None, d, bn), lambda j: (l, 0, j + nj)),
                  pl.BlockSpec((None, df, bn), lambda j: (l, 0, j)),
                  pl.BlockSpec((None, da, bn), lambda j: (l, 0, j)),
                  pl.BlockSpec((1, bn), lambda j: (0, j)),
                  pl.BlockSpec((1, bn), lambda j: (0, j + nj)),
                  _ALIASED],
        out_specs=[pl.BlockSpec((HEAD_ROWS, bn), lambda j: (0, j)),
                   pl.BlockSpec((d, bn), lambda j: (0, j)),
                   pl.BlockSpec((d, bn), lambda j: (0, j)),
                   pl.BlockSpec((df, bn), lambda j: (0, j)),
                   pl.BlockSpec((da, bn), lambda j: (0, j))],
        out_shape=[jax.ShapeDtypeStruct((m, d), BF16), jax.ShapeDtypeStruct((d, d), BF16),
                   jax.ShapeDtypeStruct((d, d), BF16), jax.ShapeDtypeStruct((df, d), BF16),
                   jax.ShapeDtypeStruct((da, d), BF16)],
        input_output_aliases={9: 0},
        compiler_params=_params(56, ("arbitrary",)),
        name="gated_mix_head",
    )(h, f, a, w_gate, w_gate, w_f, w_a, b_gate, b_gate, buf)
    assert bm == HEAD_ROWS
    wcol = lambda i, j: (0, j)
    return pl.pallas_call(
        _mix_tail_kernel,
        grid=(m // bm - 1, nj),
        in_specs=[pl.BlockSpec((bm, d), lambda i, j: (i + 1, 0)),
                  pl.BlockSpec((bm, df), lambda i, j: (i + 1, 0)),
                  pl.BlockSpec((bm, da), lambda i, j: (i + 1, 0)),
                  pl.BlockSpec((d, bn), wcol),
                  pl.BlockSpec((d, bn), wcol),
                  pl.BlockSpec((df, bn), wcol),
                  pl.BlockSpec((da, bn), wcol),
                  pl.BlockSpec((1, bn), lambda i, j: (0, j)),
                  pl.BlockSpec((1, bn), lambda i, j: (0, j + nj)),
                  _ALIASED],
        out_specs=pl.BlockSpec((bm, bn), lambda i, j: (i + 1, j)),
        out_shape=jax.ShapeDtypeStruct((m, d), BF16),
        input_output_aliases={9: 0},
        compiler_params=_params(56, ("parallel", "arbitrary")),
        name="gated_mix",
    )(h, f, a, wg0b, wg1b, wfb, wab, b_gate, b_gate, buf)


def _proj_residual_head_kernel(a_ref, w_ref, r_ref, o_ref, wb_ref):
    o_ref[...] = r_ref[...] + _cast_dot(a_ref, w_ref, wb_ref)


def _proj_residual_tail_kernel(a_ref, wb_ref, r_ref, o_ref):
    o_ref[...] = r_ref[...] + jnp.dot(a_ref[...], wb_ref[...], preferred_element_type=F32)


def proj_residual(a, w, l, res, bn_head, bm, bn):
    m, k = a.shape
    n = w.shape[2]
    res, wb = pl.pallas_call(
        _proj_residual_head_kernel,
        grid=(n // bn_head,),
        in_specs=[_resident((HEAD_ROWS, k), lambda j: (0, 0)),
                  pl.BlockSpec((None, k, bn_head), lambda j: (l, 0, j)),
                  pl.BlockSpec((HEAD_ROWS, bn_head), lambda j: (0, j))],
        out_specs=[pl.BlockSpec((HEAD_ROWS, bn_head), lambda j: (0, j)),
                   pl.BlockSpec((k, bn_head), lambda j: (0, j))],
        out_shape=[jax.ShapeDtypeStruct((m, n), F32), jax.ShapeDtypeStruct((k, n), BF16)],
        input_output_aliases={2: 0},
        compiler_params=_params(56, ("arbitrary",)),
        name="proj_residual_head",
    )(a, w, res)
    assert bm == HEAD_ROWS
    return pl.pallas_call(
        _proj_residual_tail_kernel,
        grid=(m // bm - 1, n // bn),
        in_specs=[pl.BlockSpec((bm, k), lambda i, j: (i + 1, 0)),
                  pl.BlockSpec((k, bn), lambda i, j: (0, j)),
                  pl.BlockSpec((bm, bn), lambda i, j: (i + 1, j))],
        out_specs=pl.BlockSpec((bm, bn), lambda i, j: (i + 1, j)),
        out_shape=jax.ShapeDtypeStruct((m, n), F32),
        input_output_aliases={2: 0},
        compiler_params=_params(56, ("parallel", "arbitrary")),
        name="proj_residual",
    )(a, wb, res)


def _gelu_exact(x):
    return 0.5 * x * (1.0 + lax.erf(x * math.sqrt(0.5)))


def _build_halo_tile(ext_ref, h_ref, hp_ref, hn_ref):
    bm = h_ref.shape[0]
    halo = BF16_SUBLANES
    ext_ref[0:halo, :] = hp_ref[...]
    ext_ref[halo:halo + bm, :] = h_ref[...]
    ext_ref[halo + bm:, :] = hn_ref[...]


def _conv_glu_matmul_chunk(ext_ref, wg_ref, wv_ref, c, acc):
    halo = BF16_SUBLANES
    bm = ext_ref.shape[0] - 2 * halo
    kc = ext_ref.shape[1] // PIPELINE_CHUNKS
    ks = slice(c * kc, (c + 1) * kc)
    gd = jnp.dot(ext_ref[:, ks], wg_ref[ks, :], preferred_element_type=F32)
    vd = jnp.dot(ext_ref[halo:halo + bm, ks], wv_ref[ks, :], preferred_element_type=F32)
    return (gd, vd) if c == 0 else (acc[0] + gd, acc[1] + vd)


def _conv_glu_epilogue_chunk(gate_ref, val_ref, cw_ref, cb_ref, o_ref, c, row0, seq):
    halo = BF16_SUBLANES
    pad = F32_SUBLANES
    bm = val_ref.shape[0]
    rc = bm // PIPELINE_CHUNKS
    r0 = halo + c * rc
    g = gate_ref[r0 - pad:r0 + rc + pad, :]
    row = lax.broadcasted_iota(jnp.int32, (rc, 1), 0) + c * rc
    pos = (row0 + row) % seq
    g_prev = jnp.where(pos == 0, 0.0, pltpu.roll(g, 1, axis=0)[pad:pad + rc])
    g_next = jnp.where(pos == seq - 1, 0.0, pltpu.roll(g, rc + 2 * pad - 1, axis=0)[pad:pad + rc])
    cw = cw_ref[...]
    conv = g_prev * cw[0:1] + g[pad:pad + rc] * cw[1:2] + g_next * cw[2:3] + cb_ref[...]
    rs = slice(c * rc, (c + 1) * rc)
    o_ref[rs, :] = (_gelu_exact(conv) * val_ref[rs, :]).astype(o_ref.dtype)


def _conv_glu_head_kernel(h_ref, hp_ref, hn_ref, wg_ref, wv_ref, cw_ref, cb_ref, buf_ref,
                          o_ref, wgo_ref, wvo_ref, ext_ref, wgb0, wgb1, wvb0, wvb1, gate0, gate1, val0, val1, *, seq, nj):
    j = pl.program_id(0)
    kc = h_ref.shape[1] // PIPELINE_CHUNKS
    wgb, wvb, gate_s, val_s = (wgb0, wgb1), (wvb0, wvb1), (gate0, gate1), (val0, val1)

    pl.when(j == 0)(lambda: _build_halo_tile(ext_ref, h_ref, hp_ref, hn_ref))

    def stage(p, do_cast, do_matmul, do_epilogue):
        q = 1 - p
        acc = None
        for c in range(PIPELINE_CHUNKS):
            if do_matmul:
                acc = _conv_glu_matmul_chunk(ext_ref, wgb[q], wvb[q], c, acc)
            if do_cast:
                ks = slice(c * kc, (c + 1) * kc)
                wg = wg_ref[ks, :].astype(BF16)
                wv = wv_ref[ks, :].astype(BF16)
                wgb[p][ks, :] = wg
                wvb[p][ks, :] = wv
                wgo_ref[ks, :] = wg
                wvo_ref[ks, :] = wv
            if do_epilogue:
                _conv_glu_epilogue_chunk(gate_s[p], val_s[p], cw_ref, cb_ref, o_ref, c, 0, seq)
        if do_matmul:
            gate_s[q][...], val_s[q][...] = acc

    pl.when(j == 0)(lambda: stage(0, True, False, False))
    pl.when(j == 1)(lambda: stage(1, True, True, False))
    steady = jnp.logical_and(j >= 2, j < nj)
    pl.when(jnp.logical_and(steady, j % 2 == 0))(lambda: stage(0, True, True, True))
    pl.when(jnp.logical_and(steady, j % 2 == 1))(lambda: stage(1, True, True, True))
    pl.when(j == nj)(lambda: stage(nj % 2, False, True, True))
    pl.when(j == nj + 1)(lambda: stage((nj + 1) % 2, False, False, True))


def _conv_glu_tail_kernel(h_ref, hp_ref, hn_ref, wg_ref, wv_ref, cw_ref, cb_ref, buf_ref, o_ref,
                          ext_ref, gate0, gate1, val0, val1, *, seq, nj):
    row0 = (pl.program_id(0) + 1) * h_ref.shape[0]
    j = pl.program_id(1)
    gate_s, val_s = (gate0, gate1), (val0, val1)

    pl.when(j == 0)(lambda: _build_halo_tile(ext_ref, h_ref, hp_ref, hn_ref))

    def stage(p, do_matmul, do_epilogue):
        q = 1 - p
        acc = None
        for c in range(PIPELINE_CHUNKS):
            if do_matmul:
                acc = _conv_glu_matmul_chunk(ext_ref, wg_ref, wv_ref, c, acc)
            if do_epilogue:
                _conv_glu_epilogue_chunk(gate_s[q], val_s[q], cw_ref, cb_ref, o_ref, c, row0, seq)
        if do_matmul:
            gate_s[p][...], val_s[p][...] = acc

    pl.when(j == 0)(lambda: stage(0, True, False))
    steady = jnp.logical_and(j >= 1, j < nj)
    pl.when(jnp.logical_and(steady, j % 2 == 0))(lambda: stage(0, True, True))
    pl.when(jnp.logical_and(steady, j % 2 == 1))(lambda: stage(1, True, True))
    pl.when(j == nj)(lambda: stage(nj % 2, False, True))


def conv_glu_up(h, w_up, l, conv_w, conv_b, seq, buf, bm=1024, bn_head=256, bn=256):
    m, k = h.shape
    dff = w_up.shape[2] // 2
    assert buf.shape == (m, dff) and buf.dtype == BF16
    halo = BF16_SUBLANES
    rb = bm // halo
    last = m // halo - 1
    conv_b = conv_b.reshape(1, dff)
    assert bm == HEAD_ROWS
    prev_tile = lambda i: jnp.maximum(i * rb - 1, 0)
    next_tile = lambda i: jnp.minimum((i + 1) * rb, last)
    ext = pltpu.VMEM((bm + 2 * halo, k), BF16)

    nj = dff // bn_head
    assert nj >= 2
    w_tile = lambda j: jnp.minimum(j, nj - 1)
    o_tile = lambda j: jnp.clip(j - 2, 0, nj - 1)
    wb = pltpu.VMEM((k, bn_head), BF16)
    gate = pltpu.VMEM((bm + 2 * halo, bn_head), F32)
    val = pltpu.VMEM((bm, bn_head), F32)
    wb_out = jax.ShapeDtypeStruct((k, dff), BF16)
    buf, wgb, wvb = pl.pallas_call(
        functools.partial(_conv_glu_head_kernel, seq=seq, nj=nj),
        grid=(nj + 2,),
        in_specs=[_resident((bm, k), lambda j: (0, 0)),
                  _resident((halo, k), lambda j: (prev_tile(0), 0)),
                  _resident((halo, k), lambda j: (next_tile(0), 0)),
                  pl.BlockSpec((None, k, bn_head), lambda j: (l, 0, w_tile(j))),
                  pl.BlockSpec((None, k, bn_head), lambda j: (l, 0, w_tile(j) + nj)),
                  pl.BlockSpec((CONV_WIDTH, bn_head), lambda j: (0, o_tile(j))),
                  pl.BlockSpec((1, bn_head), lambda j: (0, o_tile(j))),
                  _ALIASED],
        out_specs=[pl.BlockSpec((bm, bn_head), lambda j: (0, o_tile(j))),
                   pl.BlockSpec((k, bn_head), lambda j: (0, w_tile(j))),
                   pl.BlockSpec((k, bn_head), lambda j: (0, w_tile(j)))],
        out_shape=[jax.ShapeDtypeStruct((m, dff), BF16), wb_out, wb_out],
        input_output_aliases={7: 0},
        scratch_shapes=[ext, wb, wb, wb, wb, gate, gate, val, val],
        compiler_params=_params(56, ("arbitrary",)),
        name="conv_glu_up_head",
    )(h, h, h, w_up, w_up, conv_w, conv_b, buf)

    nj = dff // bn
    w_tile = lambda j: jnp.minimum(j, nj - 1)
    o_tile = lambda j: jnp.clip(j - 1, 0, nj - 1)
    gate = pltpu.VMEM((bm + 2 * halo, bn), F32)
    val = pltpu.VMEM((bm, bn), F32)
    return pl.pallas_call(
        functools.partial(_conv_glu_tail_kernel, seq=seq, nj=nj),
        grid=(m // bm - 1, nj + 1),
        in_specs=[_resident((bm, k), lambda i, j: (i + 1, 0)),
                  _resident((halo, k), lambda i, j: (prev_tile(i + 1), 0)),
                  _resident((halo, k), lambda i, j: (next_tile(i + 1), 0)),
                  pl.BlockSpec((k, bn), lambda i, j: (0, w_tile(j))),
                  pl.BlockSpec((k, bn), lambda i, j: (0, w_tile(j))),
                  pl.BlockSpec((CONV_WIDTH, bn), lambda i, j: (0, o_tile(j))),
                  pl.BlockSpec((1, bn), lambda i, j: (0, o_tile(j))),
                  _ALIASED],
        out_specs=pl.BlockSpec((bm, bn), lambda i, j: (i + 1, o_tile(j))),
        out_shape=jax.ShapeDtypeStruct((m, dff), BF16),
        input_output_aliases={7: 0},
        scratch_shapes=[ext, gate, gate, val, val],
        compiler_params=_params(56, ("parallel", "arbitrary")),
        name="conv_glu_up",
    )(h, h, h, wgb, wvb, conv_w, conv_b, buf)


def kernel(x, norm1_gain, w_in, w_fourier_out, lambdas, subln_gain, rel_bias_table, w_attn_out,
           w_gate, b_gate, w_o, norm2_gain, w_up, conv_w, conv_b, w_down, final_norm_gain):
    batch, seq, d_model = x.shape
    depth = w_in.shape[0]
    n_heads = rel_bias_table.shape[1]
    d_attn = n_heads * HEAD_WIDTH
    d_in = w_in.shape[2]
    tq = 512

    cs, sn, chan = _dft_constants(seq, FOURIER_GROUP_DIM)
    col = np.ones((1, d_in), np.float32)
    col[:, D_FOURIER:D_FOURIER + d_attn] = ATTN_HEAD_DIM ** -0.5 * LOG2_E
    col_scale = jnp.asarray(col)
    wb = bias_window(rel_bias_table, seq, tq)

    x = x.reshape(batch * seq, d_model)
    m = batch * seq
    u = jnp.zeros((m, d_in), BF16)
    mixed = jnp.zeros((m, d_model), BF16)
    act = jnp.zeros((m, w_up.shape[2] // 2), BF16)
    for l in range(depth):
        lam_init = 0.8 - 0.6 * math.exp(-0.3 * l)
        h = rmsnorm(x, norm1_gain[l], BF16)
        u = in_proj(h, w_in, l, col_scale, u)
        t1, t2 = fourier_chan(u, chan)
        f = fourier_seq(cs, sn, t1, t2, seq)
        a = diff_attention(u, wb, lambdas[l], subln_gain[l], lam_init, seq, n_heads, tq)
        mixed = gated_mix(h, f, a, w_gate, w_fourier_out, w_attn_out, l, b_gate[l], mixed)
        x = proj_residual(mixed, w_o, l, x, bn_head=512, bm=1024, bn=1024)
        h2 = rmsnorm(x, norm2_gain[l], BF16)
        act = conv_glu_up(h2, w_up, l, conv_w[l], conv_b[l], seq, act)
        x = proj_residual(act, w_down, l, x, bn_head=256, bm=1024, bn=256)
    out = rmsnorm(x, final_norm_gain, F32)
    return out.reshape(batch, seq, d_model)
```

```python
import functools
import math

import numpy as np
import jax
import jax.numpy as jnp
from jax import lax
from jax.experimental import pallas as pl
from jax.experimental.pallas import tpu as pltpu

F32 = jnp.float32
BF16 = jnp.bfloat16

EPS = 1e-6
N_FOURIER_GROUPS = 4
FOURIER_GROUP_DIM = 256
D_FOURIER = N_FOURIER_GROUPS * FOURIER_GROUP_DIM
ATTN_HEAD_DIM = 128
HEAD_WIDTH = 2 * ATTN_HEAD_DIM
N_REL_BUCKETS = 32
REL_MAX_DISTANCE = 128
LOG2_E = math.log2(math.e)
ATTN_KEY_CHUNK = 512
CONV_WIDTH = 3
BF16_SUBLANES = 16
F32_SUBLANES = 8
PIPELINE_CHUNKS = 16
MXU_DEPTH = 256
TAIL_SUBTILES = 2

MIB = 1024 * 1024


def _params(vmem_mib, semantics):
    return pltpu.CompilerParams(dimension_semantics=semantics, vmem_limit_bytes=vmem_mib * MIB)


def _resident(shape, index_map):
    return pl.BlockSpec(shape, index_map, pipeline_mode=pl.Buffered(1))


def _rmsnorm_kernel(x_ref, g_ref, o_ref):
    x = x_ref[...]
    ms = jnp.mean(x * x, axis=-1, keepdims=True)
    o_ref[...] = (x * lax.rsqrt(ms + EPS) * g_ref[...]).astype(o_ref.dtype)


def rmsnorm(x, gain, out_dtype, bm=256):
    m, d = x.shape
    return pl.pallas_call(
        _rmsnorm_kernel,
        grid=(m // bm,),
        in_specs=[pl.BlockSpec((bm, d), lambda i: (i, 0)),
                  pl.BlockSpec((1, d), lambda i: (0, 0))],
        out_specs=pl.BlockSpec((bm, d), lambda i: (i, 0)),
        out_shape=jax.ShapeDtypeStruct((m, d), out_dtype),
        compiler_params=_params(40, ("parallel",)),
        name="rmsnorm",
    )(x, gain.reshape(1, d))


HEAD_ROWS = 1024
_ALIASED = pl.BlockSpec(memory_space=pl.ANY)


def _cast_dot(lhs_ref, w_ref, wb_ref):
    kc = MXU_DEPTH
    acc = None
    for c in range(w_ref.shape[0] // kc):
        ks = slice(c * kc, (c + 1) * kc)
        wb_ref[ks, :] = w_ref[ks, :].astype(BF16)
        part = jnp.dot(lhs_ref[:, ks], wb_ref[ks, :], preferred_element_type=F32)
        acc = part if c == 0 else acc + part
    return acc


def _in_proj_head_kernel(h_ref, w_ref, s_ref, buf_ref, o_ref, wb_ref):
    o_ref[...] = (_cast_dot(h_ref, w_ref, wb_ref) * s_ref[...]).astype(o_ref.dtype)


def _in_proj_tail_kernel(h_ref, wb_ref, s_ref, buf_ref, o_ref):
    acc = jnp.dot(h_ref[...], wb_ref[...], preferred_element_type=F32)
    o_ref[...] = (acc * s_ref[...]).astype(o_ref.dtype)


def in_proj(h, w, l, col_scale, buf, bm=1024, bn_head=512, bn=1024):
    m, k = h.shape
    n = w.shape[2]
    assert buf.shape == (m, n) and buf.dtype == BF16
    buf, wb = pl.pallas_call(
        _in_proj_head_kernel,
        grid=(n // bn_head,),
        in_specs=[_resident((HEAD_ROWS, k), lambda j: (0, 0)),
                  pl.BlockSpec((None, k, bn_head), lambda j: (l, 0, j)),
                  pl.BlockSpec((1, bn_head), lambda j: (0, j)),
                  _ALIASED],
        out_specs=[pl.BlockSpec((HEAD_ROWS, bn_head), lambda j: (0, j)),
                   pl.BlockSpec((k, bn_head), lambda j: (0, j))],
        out_shape=[jax.ShapeDtypeStruct((m, n), BF16), jax.ShapeDtypeStruct((k, n), BF16)],
        input_output_aliases={3: 0},
        compiler_params=_params(48, ("arbitrary",)),
        name="in_proj_head",
    )(h, w, col_scale, buf)
    assert bm == HEAD_ROWS
    return pl.pallas_call(
        _in_proj_tail_kernel,
        grid=(m // bm - 1, n // bn),
        in_specs=[pl.BlockSpec((bm, k), lambda i, j: (i + 1, 0)),
                  pl.BlockSpec((k, bn), lambda i, j: (0, j)),
                  pl.BlockSpec((1, bn), lambda i, j: (0, j)),
                  _ALIASED],
        out_specs=pl.BlockSpec((bm, bn), lambda i, j: (i + 1, j)),
        out_shape=jax.ShapeDtypeStruct((m, n), BF16),
        input_output_aliases={3: 0},
        compiler_params=_params(48, ("parallel", "arbitrary")),
        name="in_proj",
    )(h, wb, col_scale, buf)


def _dft_constants(seq, dim):
    def cos_sin(n):
        idx = np.arange(n, dtype=np.int64)
        ang = 2.0 * np.pi * ((idx[:, None] * idx[None, :]) % n).astype(np.float64) / n
        return np.cos(ang), np.sin(ang)
    cs, ss = cos_sin(seq)
    cc, sc = cos_sin(dim)
    chan = np.concatenate([cc, sc], axis=1)
    return (jnp.asarray(cs, BF16), jnp.asarray(-ss, BF16), jnp.asarray(chan, BF16))


def _fourier_chan_kernel(u_ref, c_ref, t1_ref, t2_ref):
    c = c_ref[...]
    gd = FOURIER_GROUP_DIM
    for g in range(N_FOURIER_GROUPS):
        t = jnp.dot(u_ref[:, g * gd:(g + 1) * gd], c, preferred_element_type=F32)
        t1_ref[:, g * gd:(g + 1) * gd] = t[:, :gd].astype(t1_ref.dtype)
        t2_ref[:, g * gd:(g + 1) * gd] = t[:, gd:].astype(t2_ref.dtype)


def fourier_chan(u, chan, bm=1024):
    m = u.shape[0]
    gd = FOURIER_GROUP_DIM
    out = jax.ShapeDtypeStruct((m, D_FOURIER), BF16)
    return pl.pallas_call(
        _fourier_chan_kernel,
        grid=(m // bm,),
        in_specs=[pl.BlockSpec((bm, D_FOURIER), lambda i: (i, 0)),
                  pl.BlockSpec((gd, 2 * gd), lambda i: (0, 0))],
        out_specs=[pl.BlockSpec((bm, D_FOURIER), lambda i: (i, 0)),
                   pl.BlockSpec((bm, D_FOURIER), lambda i: (i, 0))],
        out_shape=[out, out],
        compiler_params=_params(32, ("parallel",)),
        name="fourier_chan",
    )(u, chan)


def _fourier_seq_kernel(cs_ref, sn_ref, t1_ref, t2_ref, o_ref, *, scale):
    acc = jnp.dot(cs_ref[...], t1_ref[...], preferred_element_type=F32)
    acc += jnp.dot(sn_ref[...], t2_ref[...], preferred_element_type=F32)
    o_ref[...] = (acc * scale).astype(o_ref.dtype)


def fourier_seq(cs, sn, t1, t2, seq, bm=512):
    m, n = t1.shape
    nb = m // seq
    ni = seq // bm
    scale = 1.0 / math.sqrt(seq * FOURIER_GROUP_DIM)
    return pl.pallas_call(
        functools.partial(_fourier_seq_kernel, scale=scale),
        grid=(nb, ni),
        in_specs=[pl.BlockSpec((bm, seq), lambda b, i: (i, 0)),
                  pl.BlockSpec((bm, seq), lambda b, i: (i, 0)),
                  pl.BlockSpec((seq, n), lambda b, i: (b, 0)),
                  pl.BlockSpec((seq, n), lambda b, i: (b, 0))],
        out_specs=pl.BlockSpec((bm, n), lambda b, i: (b * ni + i, 0)),
        out_shape=jax.ShapeDtypeStruct((m, n), BF16),
        compiler_params=_params(40, ("parallel", "arbitrary")),
        name="fourier_seq",
    )(cs, sn, t1, t2)


def _bias_window_kernel(tab_ref, o_ref, *, tq, nq):
    h = pl.program_id(0)
    half = N_REL_BUCKETS // 2
    max_exact = half // 2
    width = o_ref.shape[2]
    lo = (nq - 1) * tq - REL_MAX_DISTANCE
    hi = nq * tq + REL_MAX_DISTANCE
    o_ref[0, :, :lo] = jnp.full((tq, lo), tab_ref[half - 1, h] * LOG2_E, F32)
    o_ref[0, :, hi:] = jnp.full((tq, width - hi), tab_ref[N_REL_BUCKETS - 1, h] * LOG2_E, F32)

    shape = (tq, hi - lo)
    i = lax.broadcasted_iota(jnp.int32, shape, 0)
    c = lax.broadcasted_iota(jnp.int32, shape, 1) + lo
    n = i + (nq - 1) * tq - c
    ret = jnp.where(n < 0, half, 0)
    n = jnp.abs(n)
    n2 = n * n
    large = jnp.full(shape, max_exact, jnp.int32)
    for j in range(1, half - max_exact):
        large += (n2 >= (max_exact * max_exact) * (2 ** j)).astype(jnp.int32)
    bucket = ret + jnp.where(n < max_exact, n, large)
    acc = jnp.zeros(shape, F32)
    for b in range(N_REL_BUCKETS):
        acc = jnp.where(bucket == b, tab_ref[b, h] * LOG2_E, acc)
    o_ref[0, :, lo:hi] = acc


def bias_window(rel_table, seq, tq):
    nq = seq // tq
    nh = rel_table.shape[1]
    width = 2 * seq - tq
    assert tq >= REL_MAX_DISTANCE and nq >= 2
    return pl.pallas_call(
        functools.partial(_bias_window_kernel, tq=tq, nq=nq),
        grid=(nh,),
        in_specs=[pl.BlockSpec(memory_space=pltpu.SMEM)],
        out_specs=pl.BlockSpec((1, tq, width), lambda h: (h, 0, 0)),
        out_shape=jax.ShapeDtypeStruct((nh, tq, width), F32),
        compiler_params=_params(40, ("parallel",)),
        name="bias_window",
    )(rel_table)


def _lane_tile_reduce(x, op):
    lanes = 128
    out = x[:, 0:lanes]
    for t in range(1, x.shape[1] // lanes):
        out = op(out, x[:, t * lanes:(t + 1) * lanes])
    return out


def _attn_kernel(q_ref, k_ref, v_ref, wb_ref, lam_ref, sg_ref, o_ref, s0_ref, s1_ref, mx0_ref, mx1_ref,
                 *, lam_init, tq, nq, seq):
    i = pl.program_id(2)
    d = ATTN_HEAD_DIM
    ck = ATTN_KEY_CHUNK
    nc = seq // ck
    s_scr = (s0_ref, s1_ref)
    mx_scr = (mx0_ref, mx1_ref)
    chunks = [(m, c) for m in range(2) for c in range(nc)]

    def logits_chunk(p, m, c, run_max):
        off = pl.multiple_of((nq - 1 - i) * tq + c * ck, tq)
        s = lax.dot_general(q_ref[:, m * d:(m + 1) * d], k_ref[c * ck:(c + 1) * ck, m * d:(m + 1) * d],
                            (((1,), (1,)), ((), ())), preferred_element_type=F32)
        s = s + wb_ref[0, :, pl.ds(off, ck)]
        s_scr[p][m, :, c * ck:(c + 1) * ck] = s
        part = _lane_tile_reduce(s, jnp.maximum)
        run_max = part if c == 0 else jnp.maximum(run_max, part)
        if c == nc - 1:
            mx_scr[p][m] = jnp.broadcast_to(jnp.max(run_max, axis=-1, keepdims=True), run_max.shape)
        return run_max

    def probs_chunk(p, m, c, state):
        row_max, row_sum, out = state
        if c == 0:
            row_max = pltpu.repeat(mx_scr[p][m], ck // 128, axis=1)
        e = jnp.exp2(s_scr[p][m, :, c * ck:(c + 1) * ck] - row_max)
        part = _lane_tile_reduce(e, jnp.add)
        pv = jnp.dot(e.astype(BF16), v_ref[c * ck:(c + 1) * ck, :], preferred_element_type=F32)
        return (row_max, part if c == 0 else row_sum + part, pv if c == 0 else out + pv)

    def finish(results):
        (_, sum0, out0), (_, sum1, out1) = results
        lam_p = lam_ref[...]
        lam = (jnp.exp(jnp.sum(lam_p[0:1] * lam_p[1:2], axis=-1, keepdims=True))
               - jnp.exp(jnp.sum(lam_p[2:3] * lam_p[3:4], axis=-1, keepdims=True)) + lam_init)
        l0 = jnp.sum(sum0, axis=-1, keepdims=True)
        l1 = jnp.sum(sum1, axis=-1, keepdims=True)
        o = out0 * (1.0 / l0) - out1 * (lam / l1)
        ms = jnp.mean(o * o, axis=-1, keepdims=True)
        o = o * lax.rsqrt(ms + EPS) * sg_ref[...] * (1.0 - lam_init)
        o_ref[...] = o.astype(o_ref.dtype)

    def stage(p, do_logits, do_probs):
        lead = 2 if do_logits and do_probs else 0
        run_max = None
        state = (None, None, None)
        results = []
        for idx in range(len(chunks) + lead):
            if do_probs and idx < len(chunks):
                m, c = chunks[idx]
                state = probs_chunk(1 - p, m, c, state)
                if c == nc - 1:
                    results.append(state)
                if idx == len(chunks) - 1:
                    finish(results)
            if do_logits and idx >= lead:
                m, c = chunks[idx - lead]
                run_max = logits_chunk(p, m, c, run_max)

    pl.when(i == 0)(lambda: stage(0, True, False))
    steady = jnp.logical_and(i >= 1, i < nq)
    pl.when(jnp.logical_and(steady, i % 2 == 0))(lambda: stage(0, True, True))
    pl.when(jnp.logical_and(steady, i % 2 == 1))(lambda: stage(1, True, True))
    pl.when(i == nq)(lambda: stage(nq % 2, False, True))


def diff_attention(u, wb, lambdas, sub_gain, lam_init, seq, n_heads, tq):
    m = u.shape[0]
    nb = m // seq
    nq = seq // tq
    hw = HEAD_WIDTH
    q_blk = D_FOURIER // hw
    k_blk = q_blk + n_heads
    v_blk = k_blk + n_heads
    assert ATTN_KEY_CHUNK % tq == 0 and seq % ATTN_KEY_CHUNK == 0
    logits = pltpu.VMEM((2, tq, seq), F32)
    maxima = pltpu.VMEM((2, tq, 128), F32)
    return pl.pallas_call(
        functools.partial(_attn_kernel, lam_init=lam_init, tq=tq, nq=nq, seq=seq),
        grid=(nb, n_heads, nq + 1),
        in_specs=[pl.BlockSpec((tq, hw), lambda b, h, i: (b * nq + jnp.minimum(i, nq - 1), q_blk + h)),
                  pl.BlockSpec((seq, hw), lambda b, h, i: (b, k_blk + h)),
                  pl.BlockSpec((seq, hw), lambda b, h, i: (b, v_blk + h)),
                  pl.BlockSpec((1, tq, 2 * seq - tq), lambda b, h, i: (h, 0, 0)),
                  pl.BlockSpec((4, ATTN_HEAD_DIM), lambda b, h, i: (0, 0)),
                  pl.BlockSpec((1, hw), lambda b, h, i: (0, 0))],
        out_specs=pl.BlockSpec((tq, hw), lambda b, h, i: (b * nq + jnp.maximum(i - 1, 0), h)),
        out_shape=jax.ShapeDtypeStruct((m, n_heads * hw), BF16),
        scratch_shapes=[logits, logits, maxima, maxima],
        compiler_params=_params(48, ("parallel", "parallel", "arbitrary")),
        name="diff_attention",
    )(u, u, u, wb, lambdas, sub_gain.reshape(1, hw))


def _mix_combine(g0, g1, y_f, y_a, b0_ref, b1_ref, o_ref):
    g0 = jax.nn.sigmoid(g0 + b0_ref[...])
    g1 = jax.nn.sigmoid(g1 + b1_ref[...])
    o_ref[...] = (g0 * y_f + g1 * y_a).astype(o_ref.dtype)


def _mix_head_kernel(h_ref, f_ref, a_ref, wg0_ref, wg1_ref, wf_ref, wa_ref, b0_ref, b1_ref, buf_ref,
                     o_ref, wg0b_ref, wg1b_ref, wfb_ref, wab_ref):
    _mix_combine(_cast_dot(h_ref, wg0_ref, wg0b_ref), _cast_dot(h_ref, wg1_ref, wg1b_ref),
                 _cast_dot(f_ref, wf_ref, wfb_ref), _cast_dot(a_ref, wa_ref, wab_ref), b0_ref, b1_ref, o_ref)


def _mix_tail_kernel(h_ref, f_ref, a_ref, wg0_ref, wg1_ref, wf_ref, wa_ref, b0_ref, b1_ref, buf_ref, o_ref):
    h = h_ref[...]
    dot = functools.partial(jnp.dot, preferred_element_type=F32)
    _mix_combine(dot(h, wg0_ref[...]), dot(h, wg1_ref[...]), dot(f_ref[...], wf_ref[...]),
                 dot(a_ref[...], wa_ref[...]), b0_ref, b1_ref, o_ref)


def gated_mix(h, f, a, w_gate, w_f, w_a, l, b_gate, buf, bm=1024, bn=256):
    m, d = h.shape
    df, da = f.shape[1], a.shape[1]
    nj = d // bn
    b_gate = b_gate.reshape(1, 2 * d)
    assert buf.shape == (m, d) and buf.dtype == BF16
    buf, wg0b, wg1b, wfb, wab = pl.pallas_call(
        _mix_head_kernel,
        grid=(nj,),
        in_specs=[_resident((HEAD_ROWS, d), lambda j: (0, 0)),
                  _resident((HEAD_ROWS, df), lambda j: (0, 0)),
                  _resident((HEAD_ROWS, da), lambda j: (0, 0)),
                  pl.BlockSpec((None, d, bn), lambda j: (l, 0, j)),
                  pl.BlockSpec((None, d, bn), lambda j: (l, 0, j + nj)),
                  pl.BlockSpec((None, df, bn), lambda j: (l, 0, j)),
                  pl.BlockSpec((None, da, bn), lambda j: (l, 0, j)),
                  pl.BlockSpec((1, bn), lambda j: (0, j)),
                  pl.BlockSpec((1, bn), lambda j: (0, j + nj)),
                  _ALIASED],
        out_specs=[pl.BlockSpec((HEAD_ROWS, bn), lambda j: (0, j)),
                   pl.BlockSpec((d, bn), lambda j: (0, j)),
                   pl.BlockSpec((d, bn), lambda j: (0, j)),
                   pl.BlockSpec((df, bn), lambda j: (0, j)),
                   pl.BlockSpec((da, bn), lambda j: (0, j))],
        out_shape=[jax.ShapeDtypeStruct((m, d), BF16), jax.ShapeDtypeStruct((d, d), BF16),
                   jax.ShapeDtypeStruct((d, d), BF16), jax.ShapeDtypeStruct((df, d), BF16),
                   jax.ShapeDtypeStruct((da, d), BF16)],
        input_output_aliases={9: 0},
        compiler_params=_params(56, ("arbitrary",)),
        name="gated_mix_head",
    )(h, f, a, w_gate, w_gate, w_f, w_a, b_gate, b_gate, buf)
    assert bm == HEAD_ROWS
    wcol = lambda i, j: (0, j)
    return pl.pallas_call(
        _mix_tail_kernel,
        grid=(m // bm - 1, nj),
        in_specs=[pl.BlockSpec((bm, d), lambda i, j: (i + 1, 0)),
                  pl.BlockSpec((bm, df), lambda i, j: (i + 1, 0)),
                  pl.BlockSpec((bm, da), lambda i, j: (i + 1, 0)),
                  pl.BlockSpec((d, bn), wcol),
                  pl.BlockSpec((d, bn), wcol),
                  pl.BlockSpec((df, bn), wcol),
                  pl.BlockSpec((da, bn), wcol),
                  pl.BlockSpec((1, bn), lambda i, j: (0, j)),
                  pl.BlockSpec((1, bn), lambda i, j: (0, j + nj)),
                  _ALIASED],
        out_specs=pl.BlockSpec((bm, bn), lambda i, j: (i + 1, j)),
        out_shape=jax.ShapeDtypeStruct((m, d), BF16),
        input_output_aliases={9: 0},
        compiler_params=_params(56, ("parallel", "arbitrary")),
        name="gated_mix",
    )(h, f, a, wg0b, wg1b, wfb, wab, b_gate, b_gate, buf)


def _proj_residual_head_kernel(a_ref, w_ref, r_ref, o_ref, wb_ref):
    o_ref[...] = r_ref[...] + _cast_dot(a_ref, w_ref, wb_ref)


def _proj_residual_tail_kernel(a_ref, wb_ref, r_ref, o_ref):
    o_ref[...] = r_ref[...] + jnp.dot(a_ref[...], wb_ref[...], preferred_element_type=F32)


def proj_residual(a, w, l, res, bn_head, bm, bn):
    m, k = a.shape
    n = w.shape[2]
    res, wb = pl.pallas_call(
        _proj_residual_head_kernel,
        grid=(n // bn_head,),
        in_specs=[_resident((HEAD_ROWS, k), lambda j: (0, 0)),
                  pl.BlockSpec((None, k, bn_head), lambda j: (l, 0, j)),
                  pl.BlockSpec((HEAD_ROWS, bn_head), lambda j: (0, j))],
        out_specs=[pl.BlockSpec((HEAD_ROWS, bn_head), lambda j: (0, j)),
                   pl.BlockSpec((k, bn_head), lambda j: (0, j))],
        out_shape=[jax.ShapeDtypeStruct((m, n), F32), jax.ShapeDtypeStruct((k, n), BF16)],
        input_output_aliases={2: 0},
        compiler_params=_params(56, ("arbitrary",)),
        name="proj_residual_head",
    )(a, w, res)
    assert bm == HEAD_ROWS
    return pl.pallas_call(
        _proj_residual_tail_kernel,
        grid=(m // bm - 1, n // bn),
        in_specs=[pl.BlockSpec((bm, k), lambda i, j: (i + 1, 0)),
                  pl.BlockSpec((k, bn), lambda i, j: (0, j)),
                  pl.BlockSpec((bm, bn), lambda i, j: (i + 1, j))],
        out_specs=pl.BlockSpec((bm, bn), lambda i, j: (i + 1, j)),
        out_shape=jax.ShapeDtypeStruct((m, n), F32),
        input_output_aliases={2: 0},
        compiler_params=_params(56, ("parallel", "arbitrary")),
        name="proj_residual",
    )(a, wb, res)


def _gelu_exact(x):
    return 0.5 * x * (1.0 + lax.erf(x * math.sqrt(0.5)))


def _build_halo_tile(ext_ref, h_ref, hp_ref, hn_ref):
    bm = h_ref.shape[0]
    halo = BF16_SUBLANES
    ext_ref[0:halo, :] = hp_ref[...]
    ext_ref[halo:halo + bm, :] = h_ref[...]
    ext_ref[halo + bm:, :] = hn_ref[...]


def _conv_glu_matmul_chunk(ext_ref, wg_ref, wv_ref, c, acc):
    halo = BF16_SUBLANES
    bm = ext_ref.shape[0] - 2 * halo
    kc = ext_ref.shape[1] // PIPELINE_CHUNKS
    ks = slice(c * kc, (c + 1) * kc)
    gd = jnp.dot(ext_ref[:, ks], wg_ref[ks, :], preferred_element_type=F32)
    vd = jnp.dot(ext_ref[halo:halo + bm, ks], wv_ref[ks, :], preferred_element_type=F32)
    return (gd, vd) if c == 0 else (acc[0] + gd, acc[1] + vd)


def _conv_glu_epilogue_chunk(gate_ref, val_ref, cw_ref, cb_ref, o_ref, c, row0, seq):
    halo = BF16_SUBLANES
    pad = F32_SUBLANES
    bm = val_ref.shape[0]
    rc = bm // PIPELINE_CHUNKS
    r0 = halo + c * rc
    g = gate_ref[r0 - pad:r0 + rc + pad, :]
    row = lax.broadcasted_iota(jnp.int32, (rc, 1), 0) + c * rc
    pos = (row0 + row) % seq
    g_prev = jnp.where(pos == 0, 0.0, pltpu.roll(g, 1, axis=0)[pad:pad + rc])
    g_next = jnp.where(pos == seq - 1, 0.0, pltpu.roll(g, rc + 2 * pad - 1, axis=0)[pad:pad + rc])
    cw = cw_ref[...]
    conv = g_prev * cw[0:1] + g[pad:pad + rc] * cw[1:2] + g_next * cw[2:3] + cb_ref[...]
    rs = slice(c * rc, (c + 1) * rc)
    o_ref[rs, :] = (_gelu_exact(conv) * val_ref[rs, :]).astype(o_ref.dtype)


def _conv_glu_head_kernel(h_ref, hp_ref, hn_ref, wg_ref, wv_ref, cw_ref, cb_ref, buf_ref,
                          o_ref, wgo_ref, wvo_ref, ext_ref, wgb0, wgb1, wvb0, wvb1, gate0, gate1, val0, val1, *, seq, nj):
    j = pl.program_id(0)
    kc = h_ref.shape[1] // PIPELINE_CHUNKS
    wgb, wvb, gate_s, val_s = (wgb0, wgb1), (wvb0, wvb1), (gate0, gate1), (val0, val1)

    pl.when(j == 0)(lambda: _build_halo_tile(ext_ref, h_ref, hp_ref, hn_ref))

    def stage(p, do_cast, do_matmul, do_epilogue):
        q = 1 - p
        acc = None
        for c in range(PIPELINE_CHUNKS):
            if do_matmul:
                acc = _conv_glu_matmul_chunk(ext_ref, wgb[q], wvb[q], c, acc)
            if do_cast:
                ks = slice(c * kc, (c + 1) * kc)
                wg = wg_ref[ks, :].astype(BF16)
                wv = wv_ref[ks, :].astype(BF16)
                wgb[p][ks, :] = wg
                wvb[p][ks, :] = wv
                wgo_ref[ks, :] = wg
                wvo_ref[ks, :] = wv
            if do_epilogue:
                _conv_glu_epilogue_chunk(gate_s[p], val_s[p], cw_ref, cb_ref, o_ref, c, 0, seq)
        if do_matmul:
            gate_s[q][...], val_s[q][...] = acc

    pl.when(j == 0)(lambda: stage(0, True, False, False))
    pl.when(j == 1)(lambda: stage(1, True, True, False))
    steady = jnp.logical_and(j >= 2, j < nj)
    pl.when(jnp.logical_and(steady, j % 2 == 0))(lambda: stage(0, True, True, True))
    pl.when(jnp.logical_and(steady, j % 2 == 1))(lambda: stage(1, True, True, True))
    pl.when(j == nj)(lambda: stage(nj % 2, False, True, True))
    pl.when(j == nj + 1)(lambda: stage((nj + 1) % 2, False, False, True))


def _conv_glu_tail_kernel(h_ref, hp_ref, hn_ref, wg_ref, wv_ref, cw_ref, cb_ref, buf_ref, o_ref,
                          ext_ref, gate0, gate1, val0, val1, *, seq, nj):
    row0 = (pl.program_id(0) + 1) * h_ref.shape[0]
    j = pl.program_id(1)
    gate_s, val_s = (gate0, gate1), (val0, val1)

    pl.when(j == 0)(lambda: _build_halo_tile(ext_ref, h_ref, hp_ref, hn_ref))

    def stage(p, do_matmul, do_epilogue):
        q = 1 - p
        width = o_ref.shape[1] // TAIL_SUBTILES
        for t in range(TAIL_SUBTILES):
            cols = pl.ds(t * width, width)
            acc = None
            for c in range(PIPELINE_CHUNKS):
                if do_matmul:
                    acc = _conv_glu_matmul_chunk(ext_ref, wg_ref.at[:, cols], wv_ref.at[:, cols], c, acc)
                if do_epilogue:
                    _conv_glu_epilogue_chunk(gate_s[q].at[:, cols], val_s[q].at[:, cols], cw_ref.at[:, cols],
                                             cb_ref.at[:, cols], o_ref.at[:, cols], c, row0, seq)
            if do_matmul:
                gate_s[p][:, cols], val_s[p][:, cols] = acc

    pl.when(j == 0)(lambda: stage(0, True, False))
    steady = jnp.logical_and(j >= 1, j < nj)
    pl.when(jnp.logical_and(steady, j % 2 == 0))(lambda: stage(0, True, True))
    pl.when(jnp.logical_and(steady, j % 2 == 1))(lambda: stage(1, True, True))
    pl.when(j == nj)(lambda: stage(nj % 2, False, True))


def conv_glu_up(h, w_up, l, conv_w, conv_b, seq, buf, bm=1024, bn_head=256, bn=256 * TAIL_SUBTILES):
    m, k = h.shape
    dff = w_up.shape[2] // 2
    assert buf.shape == (m, dff) and buf.dtype == BF16
    halo = BF16_SUBLANES
    rb = bm // halo
    last = m // halo - 1
    conv_b = conv_b.reshape(1, dff)
    assert bm == HEAD_ROWS
    prev_tile = lambda i: jnp.maximum(i * rb - 1, 0)
    next_tile = lambda i: jnp.minimum((i + 1) * rb, last)
    ext = pltpu.VMEM((bm + 2 * halo, k), BF16)

    nj = dff // bn_head
    assert nj >= 2
    w_tile = lambda j: jnp.minimum(j, nj - 1)
    o_tile = lambda j: jnp.clip(j - 2, 0, nj - 1)
    wb = pltpu.VMEM((k, bn_head), BF16)
    gate = pltpu.VMEM((bm + 2 * halo, bn_head), F32)
    val = pltpu.VMEM((bm, bn_head), F32)
    wb_out = jax.ShapeDtypeStruct((k, dff), BF16)
    buf, wgb, wvb = pl.pallas_call(
        functools.partial(_conv_glu_head_kernel, seq=seq, nj=nj),
        grid=(nj + 2,),
        in_specs=[_resident((bm, k), lambda j: (0, 0)),
                  _resident((halo, k), lambda j: (prev_tile(0), 0)),
                  _resident((halo, k), lambda j: (next_tile(0), 0)),
                  pl.BlockSpec((None, k, bn_head), lambda j: (l, 0, w_tile(j))),
                  pl.BlockSpec((None, k, bn_head), lambda j: (l, 0, w_tile(j) + nj)),
                  pl.BlockSpec((CONV_WIDTH, bn_head), lambda j: (0, o_tile(j))),
                  pl.BlockSpec((1, bn_head), lambda j: (0, o_tile(j))),
                  _ALIASED],
        out_specs=[pl.BlockSpec((bm, bn_head), lambda j: (0, o_tile(j))),
                   pl.BlockSpec((k, bn_head), lambda j: (0, w_tile(j))),
                   pl.BlockSpec((k, bn_head), lambda j: (0, w_tile(j)))],
        out_shape=[jax.ShapeDtypeStruct((m, dff), BF16), wb_out, wb_out],
        input_output_aliases={7: 0},
        scratch_shapes=[ext, wb, wb, wb, wb, gate, gate, val, val],
        compiler_params=_params(56, ("arbitrary",)),
        name="conv_glu_up_head",
    )(h, h, h, w_up, w_up, conv_w, conv_b, buf)

    nj = dff // bn
    w_tile = lambda j: jnp.minimum(j, nj - 1)
    o_tile = lambda j: jnp.clip(j - 1, 0, nj - 1)
    gate = pltpu.VMEM((bm + 2 * halo, bn), F32)
    val = pltpu.VMEM((bm, bn), F32)
    return pl.pallas_call(
        functools.partial(_conv_glu_tail_kernel, seq=seq, nj=nj),
        grid=(m // bm - 1, nj + 1),
        in_specs=[_resident((bm, k), lambda i, j: (i + 1, 0)),
                  _resident((halo, k), lambda i, j: (prev_tile(i + 1), 0)),
                  _resident((halo, k), lambda i, j: (next_tile(i + 1), 0)),
                  pl.BlockSpec((k, bn), lambda i, j: (0, w_tile(j))),
                  pl.BlockSpec((k, bn), lambda i, j: (0, w_tile(j))),
                  pl.BlockSpec((CONV_WIDTH, bn), lambda i, j: (0, o_tile(j))),
                  pl.BlockSpec((1, bn), lambda i, j: (0, o_tile(j))),
                  _ALIASED],
        out_specs=pl.BlockSpec((bm, bn), lambda i, j: (i + 1, o_tile(j))),
        out_shape=jax.ShapeDtypeStruct((m, dff), BF16),
        input_output_aliases={7: 0},
        scratch_shapes=[ext, gate, gate, val, val],
        compiler_params=_params(56, ("parallel", "arbitrary")),
        name="conv_glu_up",
    )(h, h, h, wgb, wvb, conv_w, conv_b, buf)


def kernel(x, norm1_gain, w_in, w_fourier_out, lambdas, subln_gain, rel_bias_table, w_attn_out,
           w_gate, b_gate, w_o, norm2_gain, w_up, conv_w, conv_b, w_down, final_norm_gain):
    batch, seq, d_model = x.shape
    depth = w_in.shape[0]
    n_heads = rel_bias_table.shape[1]
    d_attn = n_heads * HEAD_WIDTH
    d_in = w_in.shape[2]
    tq = 512

    cs, sn, chan = _dft_constants(seq, FOURIER_GROUP_DIM)
    col = np.ones((1, d_in), np.float32)
    col[:, D_FOURIER:D_FOURIER + d_attn] = ATTN_HEAD_DIM ** -0.5 * LOG2_E
    col_scale = jnp.asarray(col)
    wb = bias_window(rel_bias_table, seq, tq)

    x = x.reshape(batch * seq, d_model)
    m = batch * seq
    u = jnp.zeros((m, d_in), BF16)
    mixed = jnp.zeros((m, d_model), BF16)
    act = jnp.zeros((m, w_up.shape[2] // 2), BF16)
    for l in range(depth):
        lam_init = 0.8 - 0.6 * math.exp(-0.3 * l)
        h = rmsnorm(x, norm1_gain[l], BF16)
        u = in_proj(h, w_in, l, col_scale, u)
        t1, t2 = fourier_chan(u, chan)
        f = fourier_seq(cs, sn, t1, t2, seq)
        a = diff_attention(u, wb, lambdas[l], subln_gain[l], lam_init, seq, n_heads, tq)
        mixed = gated_mix(h, f, a, w_gate, w_fourier_out, w_attn_out, l, b_gate[l], mixed)
        x = proj_residual(mixed, w_o, l, x, bn_head=512, bm=1024, bn=1024)
        h2 = rmsnorm(x, norm2_gain[l], BF16)
        act = conv_glu_up(h2, w_up, l, conv_w[l], conv_b[l], seq, act)
        x = proj_residual(act, w_down, l, x, bn_head=256, bm=1024, bn=256)
    out = rmsnorm(x, final_norm_gain, F32)
    return out.reshape(batch, seq, d_model)
```

```python
import functools
import math

import numpy as np
import jax
import jax.numpy as jnp
from jax import lax
from jax.experimental import pallas as pl
from jax.experimental.pallas import tpu as pltpu

F32 = jnp.float32
BF16 = jnp.bfloat16

EPS = 1e-6
N_FOURIER_GROUPS = 4
FOURIER_GROUP_DIM = 256
D_FOURIER = N_FOURIER_GROUPS * FOURIER_GROUP_DIM
ATTN_HEAD_DIM = 128
HEAD_WIDTH = 2 * ATTN_HEAD_DIM
N_REL_BUCKETS = 32
REL_MAX_DISTANCE = 128
LOG2_E = math.log2(math.e)
ATTN_KEY_CHUNK = 512
CONV_WIDTH = 3
BF16_SUBLANES = 16
F32_SUBLANES = 8
PIPELINE_CHUNKS = 16
MXU_DEPTH = 256
TAIL_SUBTILES = 2

MIB = 1024 * 1024


def _params(vmem_mib, semantics):
    return pltpu.CompilerParams(dimension_semantics=semantics, vmem_limit_bytes=vmem_mib * MIB)


def _resident(shape, index_map):
    return pl.BlockSpec(shape, index_map, pipeline_mode=pl.Buffered(1))


def _rmsnorm_kernel(x_ref, g_ref, o_ref):
    x = x_ref[...]
    ms = jnp.mean(x * x, axis=-1, keepdims=True)
    o_ref[...] = (x * lax.rsqrt(ms + EPS) * g_ref[...]).astype(o_ref.dtype)


def rmsnorm(x, gain, out_dtype, bm=256):
    m, d = x.shape
    return pl.pallas_call(
        _rmsnorm_kernel,
        grid=(m // bm,),
        in_specs=[pl.BlockSpec((bm, d), lambda i: (i, 0)),
                  pl.BlockSpec((1, d), lambda i: (0, 0))],
        out_specs=pl.BlockSpec((bm, d), lambda i: (i, 0)),
        out_shape=jax.ShapeDtypeStruct((m, d), out_dtype),
        compiler_params=_params(40, ("parallel",)),
        name="rmsnorm",
    )(x, gain.reshape(1, d))


HEAD_ROWS = 1024
_ALIASED = pl.BlockSpec(memory_space=pl.ANY)


def _cast_dot(lhs_ref, w_ref, wb_ref):
    kc = MXU_DEPTH
    acc = None
    for c in range(w_ref.shape[0] // kc):
        ks = slice(c * kc, (c + 1) * kc)
        wb_ref[ks, :] = w_ref[ks, :].astype(BF16)
        part = jnp.dot(lhs_ref[:, ks], wb_ref[ks, :], preferred_element_type=F32)
        acc = part if c == 0 else acc + part
    return acc


def _in_proj_head_kernel(h_ref, w_ref, s_ref, buf_ref, o_ref, wb_ref):
    o_ref[...] = (_cast_dot(h_ref, w_ref, wb_ref) * s_ref[...]).astype(o_ref.dtype)


def _in_proj_tail_kernel(h_ref, wb_ref, s_ref, buf_ref, o_ref):
    acc = jnp.dot(h_ref[...], wb_ref[...], preferred_element_type=F32)
    o_ref[...] = (acc * s_ref[...]).astype(o_ref.dtype)


def in_proj(h, w, l, col_scale, buf, head_rows=2 * HEAD_ROWS, bm=1024, bn_head=512, bn=1024):
    m, k = h.shape
    n = w.shape[2]
    assert buf.shape == (m, n) and buf.dtype == BF16
    buf, wb = pl.pallas_call(
        _in_proj_head_kernel,
        grid=(n // bn_head,),
        in_specs=[_resident((head_rows, k), lambda j: (0, 0)),
                  pl.BlockSpec((None, k, bn_head), lambda j: (l, 0, j)),
                  pl.BlockSpec((1, bn_head), lambda j: (0, j)),
                  _ALIASED],
        out_specs=[pl.BlockSpec((head_rows, bn_head), lambda j: (0, j)),
                   pl.BlockSpec((k, bn_head), lambda j: (0, j))],
        out_shape=[jax.ShapeDtypeStruct((m, n), BF16), jax.ShapeDtypeStruct((k, n), BF16)],
        input_output_aliases={3: 0},
        compiler_params=_params(48, ("arbitrary",)),
        name="in_proj_head",
    )(h, w, col_scale, buf)
    skip = head_rows // bm
    return pl.pallas_call(
        _in_proj_tail_kernel,
        grid=(m // bm - skip, n // bn),
        in_specs=[pl.BlockSpec((bm, k), lambda i, j: (i + skip, 0)),
                  pl.BlockSpec((k, bn), lambda i, j: (0, j)),
                  pl.BlockSpec((1, bn), lambda i, j: (0, j)),
                  _ALIASED],
        out_specs=pl.BlockSpec((bm, bn), lambda i, j: (i + skip, j)),
        out_shape=jax.ShapeDtypeStruct((m, n), BF16),
        input_output_aliases={3: 0},
        compiler_params=_params(48, ("parallel", "arbitrary")),
        name="in_proj",
    )(h, wb, col_scale, buf)


def _dft_constants(seq, dim):
    def cos_sin(n):
        idx = np.arange(n, dtype=np.int64)
        ang = 2.0 * np.pi * ((idx[:, None] * idx[None, :]) % n).astype(np.float64) / n
        return np.cos(ang), np.sin(ang)
    cs, ss = cos_sin(seq)
    cc, sc = cos_sin(dim)
    chan = np.concatenate([cc, sc], axis=1)
    return (jnp.asarray(cs, BF16), jnp.asarray(-ss, BF16), jnp.asarray(chan, BF16))


def _fourier_chan_kernel(u_ref, c_ref, t1_ref, t2_ref):
    c = c_ref[...]
    gd = FOURIER_GROUP_DIM
    for g in range(N_FOURIER_GROUPS):
        t = jnp.dot(u_ref[:, g * gd:(g + 1) * gd], c, preferred_element_type=F32)
        t1_ref[:, g * gd:(g + 1) * gd] = t[:, :gd].astype(t1_ref.dtype)
        t2_ref[:, g * gd:(g + 1) * gd] = t[:, gd:].astype(t2_ref.dtype)


def fourier_chan(u, chan, bm=1024):
    m = u.shape[0]
    gd = FOURIER_GROUP_DIM
    out = jax.ShapeDtypeStruct((m, D_FOURIER), BF16)
    return pl.pallas_call(
        _fourier_chan_kernel,
        grid=(m // bm,),
        in_specs=[pl.BlockSpec((bm, D_FOURIER), lambda i: (i, 0)),
                  pl.BlockSpec((gd, 2 * gd), lambda i: (0, 0))],
        out_specs=[pl.BlockSpec((bm, D_FOURIER), lambda i: (i, 0)),
                   pl.BlockSpec((bm, D_FOURIER), lambda i: (i, 0))],
        out_shape=[out, out],
        compiler_params=_params(32, ("parallel",)),
        name="fourier_chan",
    )(u, chan)


def _fourier_seq_kernel(cs_ref, sn_ref, t1_ref, t2_ref, o_ref, *, scale):
    acc = jnp.dot(cs_ref[...], t1_ref[...], preferred_element_type=F32)
    acc += jnp.dot(sn_ref[...], t2_ref[...], preferred_element_type=F32)
    o_ref[...] = (acc * scale).astype(o_ref.dtype)


def fourier_seq(cs, sn, t1, t2, seq, bm=512):
    m, n = t1.shape
    nb = m // seq
    ni = seq // bm
    scale = 1.0 / math.sqrt(seq * FOURIER_GROUP_DIM)
    return pl.pallas_call(
        functools.partial(_fourier_seq_kernel, scale=scale),
        grid=(nb, ni),
        in_specs=[pl.BlockSpec((bm, seq), lambda b, i: (i, 0)),
                  pl.BlockSpec((bm, seq), lambda b, i: (i, 0)),
                  pl.BlockSpec((seq, n), lambda b, i: (b, 0)),
                  pl.BlockSpec((seq, n), lambda b, i: (b, 0))],
        out_specs=pl.BlockSpec((bm, n), lambda b, i: (b * ni + i, 0)),
        out_shape=jax.ShapeDtypeStruct((m, n), BF16),
        compiler_params=_params(40, ("parallel", "arbitrary")),
        name="fourier_seq",
    )(cs, sn, t1, t2)


def _bias_window_kernel(tab_ref, o_ref, *, tq, nq):
    h = pl.program_id(0)
    half = N_REL_BUCKETS // 2
    max_exact = half // 2
    width = o_ref.shape[2]
    lo = (nq - 1) * tq - REL_MAX_DISTANCE
    hi = nq * tq + REL_MAX_DISTANCE
    o_ref[0, :, :lo] = jnp.full((tq, lo), tab_ref[half - 1, h] * LOG2_E, F32)
    o_ref[0, :, hi:] = jnp.full((tq, width - hi), tab_ref[N_REL_BUCKETS - 1, h] * LOG2_E, F32)

    shape = (tq, hi - lo)
    i = lax.broadcasted_iota(jnp.int32, shape, 0)
    c = lax.broadcasted_iota(jnp.int32, shape, 1) + lo
    n = i + (nq - 1) * tq - c
    ret = jnp.where(n < 0, half, 0)
    n = jnp.abs(n)
    n2 = n * n
    large = jnp.full(shape, max_exact, jnp.int32)
    for j in range(1, half - max_exact):
        large += (n2 >= (max_exact * max_exact) * (2 ** j)).astype(jnp.int32)
    bucket = ret + jnp.where(n < max_exact, n, large)
    acc = jnp.zeros(shape, F32)
    for b in range(N_REL_BUCKETS):
        acc = jnp.where(bucket == b, tab_ref[b, h] * LOG2_E, acc)
    o_ref[0, :, lo:hi] = acc


def bias_window(rel_table, seq, tq):
    nq = seq // tq
    nh = rel_table.shape[1]
    width = 2 * seq - tq
    assert tq >= REL_MAX_DISTANCE and nq >= 2
    return pl.pallas_call(
        functools.partial(_bias_window_kernel, tq=tq, nq=nq),
        grid=(nh,),
        in_specs=[pl.BlockSpec(memory_space=pltpu.SMEM)],
        out_specs=pl.BlockSpec((1, tq, width), lambda h: (h, 0, 0)),
        out_shape=jax.ShapeDtypeStruct((nh, tq, width), F32),
        compiler_params=_params(40, ("parallel",)),
        name="bias_window",
    )(rel_table)


def _lane_tile_reduce(x, op):
    lanes = 128
    out = x[:, 0:lanes]
    for t in range(1, x.shape[1] // lanes):
        out = op(out, x[:, t * lanes:(t + 1) * lanes])
    return out


def _attn_kernel(q_ref, k_ref, v_ref, wb_ref, lam_ref, sg_ref, o_ref, s0_ref, s1_ref, mx0_ref, mx1_ref,
                 *, lam_init, tq, nq, seq):
    i = pl.program_id(2)
    d = ATTN_HEAD_DIM
    ck = ATTN_KEY_CHUNK
    nc = seq // ck
    s_scr = (s0_ref, s1_ref)
    mx_scr = (mx0_ref, mx1_ref)
    chunks = [(m, c) for m in range(2) for c in range(nc)]

    def logits_chunk(p, m, c, run_max):
        off = pl.multiple_of((nq - 1 - i) * tq + c * ck, tq)
        s = lax.dot_general(q_ref[:, m * d:(m + 1) * d], k_ref[c * ck:(c + 1) * ck, m * d:(m + 1) * d],
                            (((1,), (1,)), ((), ())), preferred_element_type=F32)
        s = s + wb_ref[0, :, pl.ds(off, ck)]
        s_scr[p][m, :, c * ck:(c + 1) * ck] = s
        part = _lane_tile_reduce(s, jnp.maximum)
        run_max = part if c == 0 else jnp.maximum(run_max, part)
        if c == nc - 1:
            mx_scr[p][m] = jnp.broadcast_to(jnp.max(run_max, axis=-1, keepdims=True), run_max.shape)
        return run_max

    def probs_chunk(p, m, c, state):
        row_max, row_sum, out = state
        if c == 0:
            row_max = jnp.concatenate([mx_scr[p][m]] * (ck // 128), axis=1)
        e = jnp.exp2(s_scr[p][m, :, c * ck:(c + 1) * ck] - row_max)
        part = _lane_tile_reduce(e, jnp.add)
        pv = jnp.dot(e.astype(BF16), v_ref[c * ck:(c + 1) * ck, :], preferred_element_type=F32)
        return (row_max, part if c == 0 else row_sum + part, pv if c == 0 else out + pv)

    def finish(results):
        (_, sum0, out0), (_, sum1, out1) = results
        lam_p = lam_ref[...]
        lam = (jnp.exp(jnp.sum(lam_p[0:1] * lam_p[1:2], axis=-1, keepdims=True))
               - jnp.exp(jnp.sum(lam_p[2:3] * lam_p[3:4], axis=-1, keepdims=True)) + lam_init)
        l0 = jnp.sum(sum0, axis=-1, keepdims=True)
        l1 = jnp.sum(sum1, axis=-1, keepdims=True)
        o = out0 * (1.0 / l0) - out1 * (lam / l1)
        ms = jnp.mean(o * o, axis=-1, keepdims=True)
        o = o * lax.rsqrt(ms + EPS) * sg_ref[...] * (1.0 - lam_init)
        o_ref[...] = o.astype(o_ref.dtype)

    def stage(p, do_logits, do_probs):
        lead = 2 if do_logits and do_probs else 0
        run_max = None
        state = (None, None, None)
        results = []
        for idx in range(len(chunks) + lead):
            if do_probs and idx < len(chunks):
                m, c = chunks[idx]
                state = probs_chunk(1 - p, m, c, state)
                if c == nc - 1:
                    results.append(state)
                if idx == len(chunks) - 1:
                    finish(results)
            if do_logits and idx >= lead:
                m, c = chunks[idx - lead]
                run_max = logits_chunk(p, m, c, run_max)

    pl.when(i == 0)(lambda: stage(0, True, False))
    steady = jnp.logical_and(i >= 1, i < nq)
    pl.when(jnp.logical_and(steady, i % 2 == 0))(lambda: stage(0, True, True))
    pl.when(jnp.logical_and(steady, i % 2 == 1))(lambda: stage(1, True, True))
    pl.when(i == nq)(lambda: stage(nq % 2, False, True))


def diff_attention(u, wb, lambdas, sub_gain, lam_init, seq, n_heads, tq):
    m = u.shape[0]
    nb = m // seq
    nq = seq // tq
    hw = HEAD_WIDTH
    q_blk = D_FOURIER // hw
    k_blk = q_blk + n_heads
    v_blk = k_blk + n_heads
    assert ATTN_KEY_CHUNK % tq == 0 and seq % ATTN_KEY_CHUNK == 0
    logits = pltpu.VMEM((2, tq, seq), F32)
    maxima = pltpu.VMEM((2, tq, 128), F32)
    return pl.pallas_call(
        functools.partial(_attn_kernel, lam_init=lam_init, tq=tq, nq=nq, seq=seq),
        grid=(nb, n_heads, nq + 1),
        in_specs=[pl.BlockSpec((tq, hw), lambda b, h, i: (b * nq + jnp.minimum(i, nq - 1), q_blk + h)),
                  pl.BlockSpec((seq, hw), lambda b, h, i: (b, k_blk + h)),
                  pl.BlockSpec((seq, hw), lambda b, h, i: (b, v_blk + h)),
                  pl.BlockSpec((1, tq, 2 * seq - tq), lambda b, h, i: (h, 0, 0)),
                  pl.BlockSpec((4, ATTN_HEAD_DIM), lambda b, h, i: (0, 0)),
                  pl.BlockSpec((1, hw), lambda b, h, i: (0, 0))],
        out_specs=pl.BlockSpec((tq, hw), lambda b, h, i: (b * nq + jnp.maximum(i - 1, 0), h)),
        out_shape=jax.ShapeDtypeStruct((m, n_heads * hw), BF16),
        scratch_shapes=[logits, logits, maxima, maxima],
        compiler_params=_params(48, ("parallel", "parallel", "arbitrary")),
        name="diff_attention",
    )(u, u, u, wb, lambdas, sub_gain.reshape(1, hw))


def _mix_head_kernel(h_ref, f_ref, a_ref, wg0_ref, wg1_ref, wf_ref, wa_ref, b0_ref, b1_ref, buf_ref,
                     o_ref, wg0b_ref, wg1b_ref, wfb_ref, wab_ref):
    g0 = jax.nn.sigmoid(_cast_dot(h_ref, wg0_ref, wg0b_ref) + b0_ref[...])
    t0 = g0 * _cast_dot(f_ref, wf_ref, wfb_ref)
    g1 = jax.nn.sigmoid(_cast_dot(h_ref, wg1_ref, wg1b_ref) + b1_ref[...])
    o_ref[...] = (t0 + g1 * _cast_dot(a_ref, wa_ref, wab_ref)).astype(o_ref.dtype)


def _mix_tail_kernel(h_ref, f_ref, a_ref, wg0_ref, wg1_ref, wf_ref, wa_ref, b0_ref, b1_ref, buf_ref, o_ref):
    h = h_ref[...]
    dot = functools.partial(jnp.dot, preferred_element_type=F32)
    g0 = jax.nn.sigmoid(dot(h, wg0_ref[...]) + b0_ref[...])
    t0 = g0 * dot(f_ref[...], wf_ref[...])
    g1 = jax.nn.sigmoid(dot(h, wg1_ref[...]) + b1_ref[...])
    o_ref[...] = (t0 + g1 * dot(a_ref[...], wa_ref[...])).astype(o_ref.dtype)


def gated_mix(h, f, a, w_gate, w_f, w_a, l, b_gate, buf, bm=1024, bn=256):
    m, d = h.shape
    df, da = f.shape[1], a.shape[1]
    nj = d // bn
    b_gate = b_gate.reshape(1, 2 * d)
    assert buf.shape == (m, d) and buf.dtype == BF16
    buf, wg0b, wg1b, wfb, wab = pl.pallas_call(
        _mix_head_kernel,
        grid=(nj,),
        in_specs=[_resident((HEAD_ROWS, d), lambda j: (0, 0)),
                  _resident((HEAD_ROWS, df), lambda j: (0, 0)),
                  _resident((HEAD_ROWS, da), lambda j: (0, 0)),
                  pl.BlockSpec((None, d, bn), lambda j: (l, 0, j)),
                  pl.BlockSpec((None, d, bn), lambda j: (l, 0, j + nj)),
                  pl.BlockSpec((None, df, bn), lambda j: (l, 0, j)),
                  pl.BlockSpec((None, da, bn), lambda j: (l, 0, j)),
                  pl.BlockSpec((1, bn), lambda j: (0, j)),
                  pl.BlockSpec((1, bn), lambda j: (0, j + nj)),
                  _ALIASED],
        out_specs=[pl.BlockSpec((HEAD_ROWS, bn), lambda j: (0, j)),
                   pl.BlockSpec((d, bn), lambda j: (0, j)),
                   pl.BlockSpec((d, bn), lambda j: (0, j)),
                   pl.BlockSpec((df, bn), lambda j: (0, j)),
                   pl.BlockSpec((da, bn), lambda j: (0, j))],
        out_shape=[jax.ShapeDtypeStruct((m, d), BF16), jax.ShapeDtypeStruct((d, d), BF16),
                   jax.ShapeDtypeStruct((d, d), BF16), jax.ShapeDtypeStruct((df, d), BF16),
                   jax.ShapeDtypeStruct((da, d), BF16)],
        input_output_aliases={9: 0},
        compiler_params=_params(56, ("arbitrary",)),
        name="gated_mix_head",
    )(h, f, a, w_gate, w_gate, w_f, w_a, b_gate, b_gate, buf)
    assert bm == HEAD_ROWS
    wcol = lambda i, j: (0, j)
    return pl.pallas_call(
        _mix_tail_kernel,
        grid=(m // bm - 1, nj),
        in_specs=[pl.BlockSpec((bm, d), lambda i, j: (i + 1, 0)),
                  pl.BlockSpec((bm, df), lambda i, j: (i + 1, 0)),
                  pl.BlockSpec((bm, da), lambda i, j: (i + 1, 0)),
                  pl.BlockSpec((d, bn), wcol),
                  pl.BlockSpec((d, bn), wcol),
                  pl.BlockSpec((df, bn), wcol),
                  pl.BlockSpec((da, bn), wcol),
                  pl.BlockSpec((1, bn), lambda i, j: (0, j)),
                  pl.BlockSpec((1, bn), lambda i, j: (0, j + nj)),
                  _ALIASED],
        out_specs=pl.BlockSpec((bm, bn), lambda i, j: (i + 1, j)),
        out_shape=jax.ShapeDtypeStruct((m, d), BF16),
        input_output_aliases={9: 0},
        compiler_params=_params(56, ("parallel", "arbitrary")),
        name="gated_mix",
    )(h, f, a, wg0b, wg1b, wfb, wab, b_gate, b_gate, buf)


def _proj_residual_head_kernel(a_ref, w_ref, r_ref, o_ref, wb_ref):
    o_ref[...] = r_ref[...] + _cast_dot(a_ref, w_ref, wb_ref)


def _proj_residual_tail_kernel(a_ref, wb_ref, r_ref, o_ref):
    o_ref[...] = r_ref[...] + jnp.dot(a_ref[...], wb_ref[...], preferred_element_type=F32)


def proj_residual(a, w, l, res, head_rows, bn_head, bm, bn):
    m, k = a.shape
    n = w.shape[2]
    res, wb = pl.pallas_call(
        _proj_residual_head_kernel,
        grid=(n // bn_head,),
        in_specs=[_resident((head_rows, k), lambda j: (0, 0)),
                  pl.BlockSpec((None, k, bn_head), lambda j: (l, 0, j)),
                  pl.BlockSpec((head_rows, bn_head), lambda j: (0, j))],
        out_specs=[pl.BlockSpec((head_rows, bn_head), lambda j: (0, j)),
                   pl.BlockSpec((k, bn_head), lambda j: (0, j))],
        out_shape=[jax.ShapeDtypeStruct((m, n), F32), jax.ShapeDtypeStruct((k, n), BF16)],
        input_output_aliases={2: 0},
        compiler_params=_params(56, ("arbitrary",)),
        name="proj_residual_head",
    )(a, w, res)
    skip = head_rows // bm
    return pl.pallas_call(
        _proj_residual_tail_kernel,
        grid=(m // bm - skip, n // bn),
        in_specs=[pl.BlockSpec((bm, k), lambda i, j: (i + skip, 0)),
                  pl.BlockSpec((k, bn), lambda i, j: (0, j)),
                  pl.BlockSpec((bm, bn), lambda i, j: (i + skip, j))],
        out_specs=pl.BlockSpec((bm, bn), lambda i, j: (i + skip, j)),
        out_shape=jax.ShapeDtypeStruct((m, n), F32),
        input_output_aliases={2: 0},
        compiler_params=_params(56, ("parallel", "arbitrary")),
        name="proj_residual",
    )(a, wb, res)


def _gelu_exact(x):
    return 0.5 * x * (1.0 + lax.erf(x * math.sqrt(0.5)))


def _build_halo_tile(ext_ref, h_ref, hp_ref, hn_ref):
    bm = h_ref.shape[0]
    halo = BF16_SUBLANES
    ext_ref[0:halo, :] = hp_ref[...]
    ext_ref[halo:halo + bm, :] = h_ref[...]
    ext_ref[halo + bm:, :] = hn_ref[...]


def _conv_glu_matmul_chunk(ext_ref, wg_ref, wv_ref, c, acc):
    halo = BF16_SUBLANES
    bm = ext_ref.shape[0] - 2 * halo
    kc = ext_ref.shape[1] // PIPELINE_CHUNKS
    ks = slice(c * kc, (c + 1) * kc)
    gd = jnp.dot(ext_ref[:, ks], wg_ref[ks, :], preferred_element_type=F32)
    vd = jnp.dot(ext_ref[halo:halo + bm, ks], wv_ref[ks, :], preferred_element_type=F32)
    return (gd, vd) if c == 0 else (acc[0] + gd, acc[1] + vd)


def _conv_glu_epilogue_chunk(gate_ref, val_ref, cw_ref, cb_ref, o_ref, c, row0, seq):
    halo = BF16_SUBLANES
    pad = F32_SUBLANES
    bm = val_ref.shape[0]
    rc = bm // PIPELINE_CHUNKS
    r0 = halo + c * rc
    g = gate_ref[r0 - pad:r0 + rc + pad, :]
    row = lax.broadcasted_iota(jnp.int32, (rc, 1), 0) + c * rc
    pos = (row0 + row) % seq
    g_prev = jnp.where(pos == 0, 0.0, pltpu.roll(g, 1, axis=0)[pad:pad + rc])
    g_next = jnp.where(pos == seq - 1, 0.0, pltpu.roll(g, rc + 2 * pad - 1, axis=0)[pad:pad + rc])
    cw = cw_ref[...]
    conv = g_prev * cw[0:1] + g[pad:pad + rc] * cw[1:2] + g_next * cw[2:3] + cb_ref[...]
    rs = slice(c * rc, (c + 1) * rc)
    o_ref[rs, :] = (_gelu_exact(conv) * val_ref[rs, :]).astype(o_ref.dtype)


def _conv_glu_head_kernel(h_ref, hp_ref, hn_ref, wg_ref, wv_ref, cw_ref, cb_ref, buf_ref,
                          o_ref, wgo_ref, wvo_ref, ext_ref, wgb0, wgb1, wvb0, wvb1, gate0, gate1, val0, val1, *, seq, nj):
    j = pl.program_id(0)
    kc = h_ref.shape[1] // PIPELINE_CHUNKS
    wgb, wvb, gate_s, val_s = (wgb0, wgb1), (wvb0, wvb1), (gate0, gate1), (val0, val1)

    pl.when(j == 0)(lambda: _build_halo_tile(ext_ref, h_ref, hp_ref, hn_ref))

    def stage(p, do_cast, do_matmul, do_epilogue):
        q = 1 - p
        acc = None
        for c in range(PIPELINE_CHUNKS):
            if do_matmul:
                acc = _conv_glu_matmul_chunk(ext_ref, wgb[q], wvb[q], c, acc)
            if do_cast:
                ks = slice(c * kc, (c + 1) * kc)
                wg = wg_ref[ks, :].astype(BF16)
                wv = wv_ref[ks, :].astype(BF16)
                wgb[p][ks, :] = wg
                wvb[p][ks, :] = wv
                wgo_ref[ks, :] = wg
                wvo_ref[ks, :] = wv
            if do_epilogue:
                _conv_glu_epilogue_chunk(gate_s[p], val_s[p], cw_ref, cb_ref, o_ref, c, 0, seq)
        if do_matmul:
            gate_s[q][...], val_s[q][...] = acc

    pl.when(j == 0)(lambda: stage(0, True, False, False))
    pl.when(j == 1)(lambda: stage(1, True, True, False))
    steady = jnp.logical_and(j >= 2, j < nj)
    pl.when(jnp.logical_and(steady, j % 2 == 0))(lambda: stage(0, True, True, True))
    pl.when(jnp.logical_and(steady, j % 2 == 1))(lambda: stage(1, True, True, True))
    pl.when(j == nj)(lambda: stage(nj % 2, False, True, True))
    pl.when(j == nj + 1)(lambda: stage((nj + 1) % 2, False, False, True))


def _conv_glu_tail_kernel(h_ref, hp_ref, hn_ref, wg_ref, wv_ref, cw_ref, cb_ref, buf_ref, o_ref,
                          ext_ref, gate0, gate1, val0, val1, *, seq, nj):
    row0 = (pl.program_id(0) + 1) * h_ref.shape[0]
    j = pl.program_id(1)
    gate_s, val_s = (gate0, gate1), (val0, val1)

    pl.when(j == 0)(lambda: _build_halo_tile(ext_ref, h_ref, hp_ref, hn_ref))

    def stage(p, do_matmul, do_epilogue):
        q = 1 - p
        width = o_ref.shape[1] // TAIL_SUBTILES
        for t in range(TAIL_SUBTILES):
            cols = pl.ds(t * width, width)
            acc = None
            for c in range(PIPELINE_CHUNKS):
                if do_matmul:
                    acc = _conv_glu_matmul_chunk(ext_ref, wg_ref.at[:, cols], wv_ref.at[:, cols], c, acc)
                if do_epilogue:
                    _conv_glu_epilogue_chunk(gate_s[q].at[:, cols], val_s[q].at[:, cols], cw_ref.at[:, cols],
                                             cb_ref.at[:, cols], o_ref.at[:, cols], c, row0, seq)
            if do_matmul:
                gate_s[p][:, cols], val_s[p][:, cols] = acc

    pl.when(j == 0)(lambda: stage(0, True, False))
    steady = jnp.logical_and(j >= 1, j < nj)
    pl.when(jnp.logical_and(steady, j % 2 == 0))(lambda: stage(0, True, True))
    pl.when(jnp.logical_and(steady, j % 2 == 1))(lambda: stage(1, True, True))
    pl.when(j == nj)(lambda: stage(nj % 2, False, True))


def conv_glu_up(h, w_up, l, conv_w, conv_b, seq, buf, bm=1024, bn_head=256, bn=256 * TAIL_SUBTILES):
    m, k = h.shape
    dff = w_up.shape[2] // 2
    assert buf.shape == (m, dff) and buf.dtype == BF16
    halo = BF16_SUBLANES
    rb = bm // halo
    last = m // halo - 1
    conv_b = conv_b.reshape(1, dff)
    assert bm == HEAD_ROWS
    prev_tile = lambda i: jnp.maximum(i * rb - 1, 0)
    next_tile = lambda i: jnp.minimum((i + 1) * rb, last)
    ext = pltpu.VMEM((bm + 2 * halo, k), BF16)

    nj = dff // bn_head
    assert nj >= 2
    w_tile = lambda j: jnp.minimum(j, nj - 1)
    o_tile = lambda j: jnp.clip(j - 2, 0, nj - 1)
    wb = pltpu.VMEM((k, bn_head), BF16)
    gate = pltpu.VMEM((bm + 2 * halo, bn_head), F32)
    val = pltpu.VMEM((bm, bn_head), F32)
    wb_out = jax.ShapeDtypeStruct((k, dff), BF16)
    buf, wgb, wvb = pl.pallas_call(
        functools.partial(_conv_glu_head_kernel, seq=seq, nj=nj),
        grid=(nj + 2,),
        in_specs=[_resident((bm, k), lambda j: (0, 0)),
                  _resident((halo, k), lambda j: (prev_tile(0), 0)),
                  _resident((halo, k), lambda j: (next_tile(0), 0)),
                  pl.BlockSpec((None, k, bn_head), lambda j: (l, 0, w_tile(j))),
                  pl.BlockSpec((None, k, bn_head), lambda j: (l, 0, w_tile(j) + nj)),
                  pl.BlockSpec((CONV_WIDTH, bn_head), lambda j: (0, o_tile(j))),
                  pl.BlockSpec((1, bn_head), lambda j: (0, o_tile(j))),
                  _ALIASED],
        out_specs=[pl.BlockSpec((bm, bn_head), lambda j: (0, o_tile(j))),
                   pl.BlockSpec((k, bn_head), lambda j: (0, w_tile(j))),
                   pl.BlockSpec((k, bn_head), lambda j: (0, w_tile(j)))],
        out_shape=[jax.ShapeDtypeStruct((m, dff), BF16), wb_out, wb_out],
        input_output_aliases={7: 0},
        scratch_shapes=[ext, wb, wb, wb, wb, gate, gate, val, val],
        compiler_params=_params(56, ("arbitrary",)),
        name="conv_glu_up_head",
    )(h, h, h, w_up, w_up, conv_w, conv_b, buf)

    nj = dff // bn
    w_tile = lambda j: jnp.minimum(j, nj - 1)
    o_tile = lambda j: jnp.clip(j - 1, 0, nj - 1)
    gate = pltpu.VMEM((bm + 2 * halo, bn), F32)
    val = pltpu.VMEM((bm, bn), F32)
    return pl.pallas_call(
        functools.partial(_conv_glu_tail_kernel, seq=seq, nj=nj),
        grid=(m // bm - 1, nj + 1),
        in_specs=[_resident((bm, k), lambda i, j: (i + 1, 0)),
                  _resident((halo, k), lambda i, j: (prev_tile(i + 1), 0)),
                  _resident((halo, k), lambda i, j: (next_tile(i + 1), 0)),
                  pl.BlockSpec((k, bn), lambda i, j: (0, w_tile(j))),
                  pl.BlockSpec((k, bn), lambda i, j: (0, w_tile(j))),
                  pl.BlockSpec((CONV_WIDTH, bn), lambda i, j: (0, o_tile(j))),
                  pl.BlockSpec((1, bn), lambda i, j: (0, o_tile(j))),
                  _ALIASED],
        out_specs=pl.BlockSpec((bm, bn), lambda i, j: (i + 1, o_tile(j))),
        out_shape=jax.ShapeDtypeStruct((m, dff), BF16),
        input_output_aliases={7: 0},
        scratch_shapes=[ext, gate, gate, val, val],
        compiler_params=_params(56, ("parallel", "arbitrary")),
        name="conv_glu_up",
    )(h, h, h, wgb, wvb, conv_w, conv_b, buf)


def kernel(x, norm1_gain, w_in, w_fourier_out, lambdas, subln_gain, rel_bias_table, w_attn_out,
           w_gate, b_gate, w_o, norm2_gain, w_up, conv_w, conv_b, w_down, final_norm_gain):
    batch, seq, d_model = x.shape
    depth = w_in.shape[0]
    n_heads = rel_bias_table.shape[1]
    d_attn = n_heads * HEAD_WIDTH
    d_in = w_in.shape[2]
    tq = 512

    cs, sn, chan = _dft_constants(seq, FOURIER_GROUP_DIM)
    col = np.ones((1, d_in), np.float32)
    col[:, D_FOURIER:D_FOURIER + d_attn] = ATTN_HEAD_DIM ** -0.5 * LOG2_E
    col_scale = jnp.asarray(col)
    wb = bias_window(rel_bias_table, seq, tq)

    x = x.reshape(batch * seq, d_model)
    m = batch * seq
    u = jnp.zeros((m, d_in), BF16)
    mixed = jnp.zeros((m, d_model), BF16)
    act = jnp.zeros((m, w_up.shape[2] // 2), BF16)
    for l in range(depth):
        lam_init = 0.8 - 0.6 * math.exp(-0.3 * l)
        h = rmsnorm(x, norm1_gain[l], BF16)
        u = in_proj(h, w_in, l, col_scale, u)
        t1, t2 = fourier_chan(u, chan)
        f = fourier_seq(cs, sn, t1, t2, seq)
        a = diff_attention(u, wb, lambdas[l], subln_gain[l], lam_init, seq, n_heads, tq)
        mixed = gated_mix(h, f, a, w_gate, w_fourier_out, w_attn_out, l, b_gate[l], mixed)
        x = proj_residual(mixed, w_o, l, x, head_rows=2 * HEAD_ROWS, bn_head=256, bm=1024, bn=1024)
        h2 = rmsnorm(x, norm2_gain[l], BF16)
        act = conv_glu_up(h2, w_up, l, conv_w[l], conv_b[l], seq, act)
        x = proj_residual(act, w_down, l, x, head_rows=HEAD_ROWS, bn_head=256, bm=1024, bn=256)
    out = rmsnorm(x, final_norm_gain, F32)
    return out.reshape(batch, seq, d_model)
```

```python
import functools
import math

import numpy as np
import jax
import jax.numpy as jnp
from jax import lax
from jax.experimental import pallas as pl
from jax.experimental.pallas import tpu as pltpu

F32 = jnp.float32
BF16 = jnp.bfloat16

EPS = 1e-6
N_FOURIER_GROUPS = 4
FOURIER_GROUP_DIM = 256
D_FOURIER = N_FOURIER_GROUPS * FOURIER_GROUP_DIM
ATTN_HEAD_DIM = 128
HEAD_WIDTH = 2 * ATTN_HEAD_DIM
N_REL_BUCKETS = 32
REL_MAX_DISTANCE = 128
LOG2_E = math.log2(math.e)
ATTN_KEY_CHUNK = 512
CONV_WIDTH = 3
BF16_SUBLANES = 16
F32_SUBLANES = 8
PIPELINE_CHUNKS = 16
MXU_DEPTH = 256
EPILOGUE_SPLIT = 4
TAIL_SUBTILES = 2

MIB = 1024 * 1024


def _params(vmem_mib, semantics):
    return pltpu.CompilerParams(dimension_semantics=semantics, vmem_limit_bytes=vmem_mib * MIB)


def _resident(shape, index_map):
    return pl.BlockSpec(shape, index_map, pipeline_mode=pl.Buffered(1))


def _rmsnorm_kernel(x_ref, g_ref, o_ref):
    x = x_ref[...]
    ms = jnp.mean(x * x, axis=-1, keepdims=True)
    o_ref[...] = (x * lax.rsqrt(ms + EPS) * g_ref[...]).astype(o_ref.dtype)


def rmsnorm(x, gain, out_dtype, bm=256):
    m, d = x.shape
    return pl.pallas_call(
        _rmsnorm_kernel,
        grid=(m // bm,),
        in_specs=[pl.BlockSpec((bm, d), lambda i: (i, 0)),
                  pl.BlockSpec((1, d), lambda i: (0, 0))],
        out_specs=pl.BlockSpec((bm, d), lambda i: (i, 0)),
        out_shape=jax.ShapeDtypeStruct((m, d), out_dtype),
        compiler_params=_params(40, ("parallel",)),
        name="rmsnorm",
    )(x, gain.reshape(1, d))


HEAD_ROWS = 1024
_ALIASED = pl.BlockSpec(memory_space=pl.ANY)


def _cast_dot(lhs_ref, w_ref, wb_ref):
    kc = MXU_DEPTH
    acc = None
    for c in range(w_ref.shape[0] // kc):
        ks = slice(c * kc, (c + 1) * kc)
        wb_ref[ks, :] = w_ref[ks, :].astype(BF16)
        part = jnp.dot(lhs_ref[:, ks], wb_ref[ks, :], preferred_element_type=F32)
        acc = part if c == 0 else acc + part
    return acc


def _in_proj_head_kernel(h_ref, w_ref, s_ref, buf_ref, o_ref, wb_ref):
    o_ref[...] = (_cast_dot(h_ref, w_ref, wb_ref) * s_ref[...]).astype(o_ref.dtype)


def _in_proj_tail_kernel(h_ref, wb_ref, s_ref, buf_ref, o_ref):
    acc = jnp.dot(h_ref[...], wb_ref[...], preferred_element_type=F32)
    o_ref[...] = (acc * s_ref[...]).astype(o_ref.dtype)


def in_proj(h, w, l, col_scale, buf, head_rows=2 * HEAD_ROWS, bm=1024, bn_head=512, bn=1024):
    m, k = h.shape
    n = w.shape[2]
    assert buf.shape == (m, n) and buf.dtype == BF16
    buf, wb = pl.pallas_call(
        _in_proj_head_kernel,
        grid=(n // bn_head,),
        in_specs=[_resident((head_rows, k), lambda j: (0, 0)),
                  pl.BlockSpec((None, k, bn_head), lambda j: (l, 0, j)),
                  pl.BlockSpec((1, bn_head), lambda j: (0, j)),
                  _ALIASED],
        out_specs=[pl.BlockSpec((head_rows, bn_head), lambda j: (0, j)),
                   pl.BlockSpec((k, bn_head), lambda j: (0, j))],
        out_shape=[jax.ShapeDtypeStruct((m, n), BF16), jax.ShapeDtypeStruct((k, n), BF16)],
        input_output_aliases={3: 0},
        compiler_params=_params(48, ("arbitrary",)),
        name="in_proj_head",
    )(h, w, col_scale, buf)
    skip = head_rows // bm
    return pl.pallas_call(
        _in_proj_tail_kernel,
        grid=(m // bm - skip, n // bn),
        in_specs=[pl.BlockSpec((bm, k), lambda i, j: (i + skip, 0)),
                  pl.BlockSpec((k, bn), lambda i, j: (0, j)),
                  pl.BlockSpec((1, bn), lambda i, j: (0, j)),
                  _ALIASED],
        out_specs=pl.BlockSpec((bm, bn), lambda i, j: (i + skip, j)),
        out_shape=jax.ShapeDtypeStruct((m, n), BF16),
        input_output_aliases={3: 0},
        compiler_params=_params(48, ("parallel", "arbitrary")),
        name="in_proj",
    )(h, wb, col_scale, buf)


def _dft_constants(seq, dim):
    def cos_sin(n):
        idx = np.arange(n, dtype=np.int64)
        ang = 2.0 * np.pi * ((idx[:, None] * idx[None, :]) % n).astype(np.float64) / n
        return np.cos(ang), np.sin(ang)
    cs, ss = cos_sin(seq)
    cc, sc = cos_sin(dim)
    chan = np.concatenate([cc, sc], axis=1)
    return (jnp.asarray(cs, BF16), jnp.asarray(-ss, BF16), jnp.asarray(chan, BF16))


def _fourier_chan_kernel(u_ref, c_ref, t1_ref, t2_ref):
    c = c_ref[...]
    gd = FOURIER_GROUP_DIM
    for g in range(N_FOURIER_GROUPS):
        t = jnp.dot(u_ref[:, g * gd:(g + 1) * gd], c, preferred_element_type=F32)
        t1_ref[:, g * gd:(g + 1) * gd] = t[:, :gd].astype(t1_ref.dtype)
        t2_ref[:, g * gd:(g + 1) * gd] = t[:, gd:].astype(t2_ref.dtype)


def fourier_chan(u, chan, bm=1024):
    m = u.shape[0]
    gd = FOURIER_GROUP_DIM
    out = jax.ShapeDtypeStruct((m, D_FOURIER), BF16)
    return pl.pallas_call(
        _fourier_chan_kernel,
        grid=(m // bm,),
        in_specs=[pl.BlockSpec((bm, D_FOURIER), lambda i: (i, 0)),
                  pl.BlockSpec((gd, 2 * gd), lambda i: (0, 0))],
        out_specs=[pl.BlockSpec((bm, D_FOURIER), lambda i: (i, 0)),
                   pl.BlockSpec((bm, D_FOURIER), lambda i: (i, 0))],
        out_shape=[out, out],
        compiler_params=_params(32, ("parallel",)),
        name="fourier_chan",
    )(u, chan)


def _fourier_seq_kernel(cs_ref, sn_ref, t1_ref, t2_ref, o_ref, *, scale):
    acc = jnp.dot(cs_ref[...], t1_ref[...], preferred_element_type=F32)
    acc += jnp.dot(sn_ref[...], t2_ref[...], preferred_element_type=F32)
    o_ref[...] = (acc * scale).astype(o_ref.dtype)


def fourier_seq(cs, sn, t1, t2, seq, bm=512):
    m, n = t1.shape
    nb = m // seq
    ni = seq // bm
    scale = 1.0 / math.sqrt(seq * FOURIER_GROUP_DIM)
    return pl.pallas_call(
        functools.partial(_fourier_seq_kernel, scale=scale),
        grid=(nb, ni),
        in_specs=[pl.BlockSpec((bm, seq), lambda b, i: (i, 0)),
                  pl.BlockSpec((bm, seq), lambda b, i: (i, 0)),
                  pl.BlockSpec((seq, n), lambda b, i: (b, 0)),
                  pl.BlockSpec((seq, n), lambda b, i: (b, 0))],
        out_specs=pl.BlockSpec((bm, n), lambda b, i: (b * ni + i, 0)),
        out_shape=jax.ShapeDtypeStruct((m, n), BF16),
        compiler_params=_params(40, ("parallel", "arbitrary")),
        name="fourier_seq",
    )(cs, sn, t1, t2)


def _bias_window_kernel(tab_ref, o_ref, *, tq, nq):
    h = pl.program_id(0)
    half = N_REL_BUCKETS // 2
    max_exact = half // 2
    width = o_ref.shape[2]
    lo = (nq - 1) * tq - REL_MAX_DISTANCE
    hi = nq * tq + REL_MAX_DISTANCE
    o_ref[0, :, :lo] = jnp.full((tq, lo), tab_ref[half - 1, h] * LOG2_E, F32)
    o_ref[0, :, hi:] = jnp.full((tq, width - hi), tab_ref[N_REL_BUCKETS - 1, h] * LOG2_E, F32)

    shape = (tq, hi - lo)
    i = lax.broadcasted_iota(jnp.int32, shape, 0)
    c = lax.broadcasted_iota(jnp.int32, shape, 1) + lo
    n = i + (nq - 1) * tq - c
    ret = jnp.where(n < 0, half, 0)
    n = jnp.abs(n)
    n2 = n * n
    large = jnp.full(shape, max_exact, jnp.int32)
    for j in range(1, half - max_exact):
        large += (n2 >= (max_exact * max_exact) * (2 ** j)).astype(jnp.int32)
    bucket = ret + jnp.where(n < max_exact, n, large)
    acc = jnp.zeros(shape, F32)
    for b in range(N_REL_BUCKETS):
        acc = jnp.where(bucket == b, tab_ref[b, h] * LOG2_E, acc)
    o_ref[0, :, lo:hi] = acc


def bias_window(rel_table, seq, tq):
    nq = seq // tq
    nh = rel_table.shape[1]
    width = 2 * seq - tq
    assert tq >= REL_MAX_DISTANCE and nq >= 2
    return pl.pallas_call(
        functools.partial(_bias_window_kernel, tq=tq, nq=nq),
        grid=(nh,),
        in_specs=[pl.BlockSpec(memory_space=pltpu.SMEM)],
        out_specs=pl.BlockSpec((1, tq, width), lambda h: (h, 0, 0)),
        out_shape=jax.ShapeDtypeStruct((nh, tq, width), F32),
        compiler_params=_params(40, ("parallel",)),
        name="bias_window",
    )(rel_table)


def _lane_tile_reduce(x, op):
    lanes = 128
    out = x[:, 0:lanes]
    for t in range(1, x.shape[1] // lanes):
        out = op(out, x[:, t * lanes:(t + 1) * lanes])
    return out


def _attn_kernel(q_ref, k_ref, v_ref, wb_ref, lam_ref, sg_ref, o_ref, s0_ref, s1_ref, mx0_ref, mx1_ref,
                 *, lam_init, tq, nq, seq):
    i = pl.program_id(2)
    d = ATTN_HEAD_DIM
    ck = ATTN_KEY_CHUNK
    nc = seq // ck
    s_scr = (s0_ref, s1_ref)
    mx_scr = (mx0_ref, mx1_ref)
    chunks = [(m, c) for m in range(2) for c in range(nc)]

    def logits_chunk(p, m, c, run_max):
        off = pl.multiple_of((nq - 1 - i) * tq + c * ck, tq)
        s = lax.dot_general(q_ref[:, m * d:(m + 1) * d], k_ref[c * ck:(c + 1) * ck, m * d:(m + 1) * d],
                            (((1,), (1,)), ((), ())), preferred_element_type=F32)
        s = s + wb_ref[0, :, pl.ds(off, ck)]
        s_scr[p][m, :, c * ck:(c + 1) * ck] = s
        part = _lane_tile_reduce(s, jnp.maximum)
        run_max = part if c == 0 else jnp.maximum(run_max, part)
        if c == nc - 1:
            mx_scr[p][m] = jnp.broadcast_to(jnp.max(run_max, axis=-1, keepdims=True), run_max.shape)
        return run_max

    def probs_chunk(p, m, c, state):
        row_max, row_sum, out = state
        if c == 0:
            row_max = jnp.concatenate([mx_scr[p][m]] * (ck // 128), axis=1)
        e = jnp.exp2(s_scr[p][m, :, c * ck:(c + 1) * ck] - row_max)
        part = _lane_tile_reduce(e, jnp.add)
        pv = jnp.dot(e.astype(BF16), v_ref[c * ck:(c + 1) * ck, :], preferred_element_type=F32)
        return (row_max, part if c == 0 else row_sum + part, pv if c == 0 else out + pv)

    def finish(results):
        (_, sum0, out0), (_, sum1, out1) = results
        lam_p = lam_ref[...]
        lam = (jnp.exp(jnp.sum(lam_p[0:1] * lam_p[1:2], axis=-1, keepdims=True))
               - jnp.exp(jnp.sum(lam_p[2:3] * lam_p[3:4], axis=-1, keepdims=True)) + lam_init)
        l0 = jnp.sum(sum0, axis=-1, keepdims=True)
        l1 = jnp.sum(sum1, axis=-1, keepdims=True)
        o = out0 * (1.0 / l0) - out1 * (lam / l1)
        ms = jnp.mean(o * o, axis=-1, keepdims=True)
        o = o * lax.rsqrt(ms + EPS) * sg_ref[...] * (1.0 - lam_init)
        o_ref[...] = o.astype(o_ref.dtype)

    def stage(p, do_logits, do_probs):
        lead = 2 if do_logits and do_probs else 0
        run_max = None
        state = (None, None, None)
        results = []
        for idx in range(len(chunks) + lead):
            if do_probs and idx < len(chunks):
                m, c = chunks[idx]
                state = probs_chunk(1 - p, m, c, state)
                if c == nc - 1:
                    results.append(state)
                if idx == len(chunks) - 1:
                    finish(results)
            if do_logits and idx >= lead:
                m, c = chunks[idx - lead]
                run_max = logits_chunk(p, m, c, run_max)

    pl.when(i == 0)(lambda: stage(0, True, False))
    steady = jnp.logical_and(i >= 1, i < nq)
    pl.when(jnp.logical_and(steady, i % 2 == 0))(lambda: stage(0, True, True))
    pl.when(jnp.logical_and(steady, i % 2 == 1))(lambda: stage(1, True, True))
    pl.when(i == nq)(lambda: stage(nq % 2, False, True))


def diff_attention(u, wb, lambdas, sub_gain, lam_init, seq, n_heads, tq):
    m = u.shape[0]
    nb = m // seq
    nq = seq // tq
    hw = HEAD_WIDTH
    q_blk = D_FOURIER // hw
    k_blk = q_blk + n_heads
    v_blk = k_blk + n_heads
    assert ATTN_KEY_CHUNK % tq == 0 and seq % ATTN_KEY_CHUNK == 0
    logits = pltpu.VMEM((2, tq, seq), F32)
    maxima = pltpu.VMEM((2, tq, 128), F32)
    return pl.pallas_call(
        functools.partial(_attn_kernel, lam_init=lam_init, tq=tq, nq=nq, seq=seq),
        grid=(nb, n_heads, nq + 1),
        in_specs=[pl.BlockSpec((tq, hw), lambda b, h, i: (b * nq + jnp.minimum(i, nq - 1), q_blk + h)),
                  pl.BlockSpec((seq, hw), lambda b, h, i: (b, k_blk + h)),
                  pl.BlockSpec((seq, hw), lambda b, h, i: (b, v_blk + h)),
                  pl.BlockSpec((1, tq, 2 * seq - tq), lambda b, h, i: (h, 0, 0)),
                  pl.BlockSpec((4, ATTN_HEAD_DIM), lambda b, h, i: (0, 0)),
                  pl.BlockSpec((1, hw), lambda b, h, i: (0, 0))],
        out_specs=pl.BlockSpec((tq, hw), lambda b, h, i: (b * nq + jnp.maximum(i - 1, 0), h)),
        out_shape=jax.ShapeDtypeStruct((m, n_heads * hw), BF16),
        scratch_shapes=[logits, logits, maxima, maxima],
        compiler_params=_params(48, ("parallel", "parallel", "arbitrary")),
        name="diff_attention",
    )(u, u, u, wb, lambdas, sub_gain.reshape(1, hw))


def _mix_head_kernel(h_ref, f_ref, a_ref, wg0_ref, wg1_ref, wf_ref, wa_ref, b0_ref, b1_ref, buf_ref,
                     o_ref, wg0b_ref, wg1b_ref, wfb_ref, wab_ref):
    g0 = jax.nn.sigmoid(_cast_dot(h_ref, wg0_ref, wg0b_ref) + b0_ref[...])
    t0 = g0 * _cast_dot(f_ref, wf_ref, wfb_ref)
    g1 = jax.nn.sigmoid(_cast_dot(h_ref, wg1_ref, wg1b_ref) + b1_ref[...])
    o_ref[...] = (t0 + g1 * _cast_dot(a_ref, wa_ref, wab_ref)).astype(o_ref.dtype)


def _mix_tail_kernel(h_ref, f_ref, a_ref, wg0_ref, wg1_ref, wf_ref, wa_ref, b0_ref, b1_ref, buf_ref, o_ref):
    h = h_ref[...]
    dot = functools.partial(jnp.dot, preferred_element_type=F32)
    g0 = jax.nn.sigmoid(dot(h, wg0_ref[...]) + b0_ref[...])
    t0 = g0 * dot(f_ref[...], wf_ref[...])
    g1 = jax.nn.sigmoid(dot(h, wg1_ref[...]) + b1_ref[...])
    o_ref[...] = (t0 + g1 * dot(a_ref[...], wa_ref[...])).astype(o_ref.dtype)


def gated_mix(h, f, a, w_gate, w_f, w_a, l, b_gate, buf, bm=1024, bn=256):
    m, d = h.shape
    df, da = f.shape[1], a.shape[1]
    nj = d // bn
    b_gate = b_gate.reshape(1, 2 * d)
    assert buf.shape == (m, d) and buf.dtype == BF16
    buf, wg0b, wg1b, wfb, wab = pl.pallas_call(
        _mix_head_kernel,
        grid=(nj,),
        in_specs=[_resident((HEAD_ROWS, d), lambda j: (0, 0)),
                  _resident((HEAD_ROWS, df), lambda j: (0, 0)),
                  _resident((HEAD_ROWS, da), lambda j: (0, 0)),
                  pl.BlockSpec((None, d, bn), lambda j: (l, 0, j)),
                  pl.BlockSpec((None, d, bn), lambda j: (l, 0, j + nj)),
                  pl.BlockSpec((None, df, bn), lambda j: (l, 0, j)),
                  pl.BlockSpec((None, da, bn), lambda j: (l, 0, j)),
                  pl.BlockSpec((1, bn), lambda j: (0, j)),
                  pl.BlockSpec((1, bn), lambda j: (0, j + nj)),
                  _ALIASED],
        out_specs=[pl.BlockSpec((HEAD_ROWS, bn), lambda j: (0, j)),
                   pl.BlockSpec((d, bn), lambda j: (0, j)),
                   pl.BlockSpec((d, bn), lambda j: (0, j)),
                   pl.BlockSpec((df, bn), lambda j: (0, j)),
                   pl.BlockSpec((da, bn), lambda j: (0, j))],
        out_shape=[jax.ShapeDtypeStruct((m, d), BF16), jax.ShapeDtypeStruct((d, d), BF16),
                   jax.ShapeDtypeStruct((d, d), BF16), jax.ShapeDtypeStruct((df, d), BF16),
                   jax.ShapeDtypeStruct((da, d), BF16)],
        input_output_aliases={9: 0},
        compiler_params=_params(56, ("arbitrary",)),
        name="gated_mix_head",
    )(h, f, a, w_gate, w_gate, w_f, w_a, b_gate, b_gate, buf)
    assert bm == HEAD_ROWS
    wcol = lambda i, j: (0, j)
    return pl.pallas_call(
        _mix_tail_kernel,
        grid=(m // bm - 1, nj),
        in_specs=[pl.BlockSpec((bm, d), lambda i, j: (i + 1, 0)),
                  pl.BlockSpec((bm, df), lambda i, j: (i + 1, 0)),
                  pl.BlockSpec((bm, da), lambda i, j: (i + 1, 0)),
                  pl.BlockSpec((d, bn), wcol),
                  pl.BlockSpec((d, bn), wcol),
                  pl.BlockSpec((df, bn), wcol),
                  pl.BlockSpec((da, bn), wcol),
                  pl.BlockSpec((1, bn), lambda i, j: (0, j)),
                  pl.BlockSpec((1, bn), lambda i, j: (0, j + nj)),
                  _ALIASED],
        out_specs=pl.BlockSpec((bm, bn), lambda i, j: (i + 1, j)),
        out_shape=jax.ShapeDtypeStruct((m, d), BF16),
        input_output_aliases={9: 0},
        compiler_params=_params(56, ("parallel", "arbitrary")),
        name="gated_mix",
    )(h, f, a, wg0b, wg1b, wfb, wab, b_gate, b_gate, buf)


def _proj_residual_head_kernel(a_ref, w_ref, r_ref, o_ref, wb_ref):
    o_ref[...] = r_ref[...] + _cast_dot(a_ref, w_ref, wb_ref)


def _proj_residual_tail_kernel(a_ref, wb_ref, r_ref, o_ref):
    o_ref[...] = r_ref[...] + jnp.dot(a_ref[...], wb_ref[...], preferred_element_type=F32)


def proj_residual(a, w, l, res, head_rows, bn_head, bm, bn):
    m, k = a.shape
    n = w.shape[2]
    res, wb = pl.pallas_call(
        _proj_residual_head_kernel,
        grid=(n // bn_head,),
        in_specs=[_resident((head_rows, k), lambda j: (0, 0)),
                  pl.BlockSpec((None, k, bn_head), lambda j: (l, 0, j)),
                  pl.BlockSpec((head_rows, bn_head), lambda j: (0, j))],
        out_specs=[pl.BlockSpec((head_rows, bn_head), lambda j: (0, j)),
                   pl.BlockSpec((k, bn_head), lambda j: (0, j))],
        out_shape=[jax.ShapeDtypeStruct((m, n), F32), jax.ShapeDtypeStruct((k, n), BF16)],
        input_output_aliases={2: 0},
        compiler_params=_params(56, ("arbitrary",)),
        name="proj_residual_head",
    )(a, w, res)
    skip = head_rows // bm
    return pl.pallas_call(
        _proj_residual_tail_kernel,
        grid=(m // bm - skip, n // bn),
        in_specs=[pl.BlockSpec((bm, k), lambda i, j: (i + skip, 0)),
                  pl.BlockSpec((k, bn), lambda i, j: (0, j)),
                  pl.BlockSpec((bm, bn), lambda i, j: (i + skip, j))],
        out_specs=pl.BlockSpec((bm, bn), lambda i, j: (i + skip, j)),
        out_shape=jax.ShapeDtypeStruct((m, n), F32),
        input_output_aliases={2: 0},
        compiler_params=_params(56, ("parallel", "arbitrary")),
        name="proj_residual",
    )(a, wb, res)


def _gelu_exact(x):
    return 0.5 * x * (1.0 + lax.erf(x * math.sqrt(0.5)))


def _build_halo_tile(ext_ref, h_ref, hp_ref, hn_ref):
    bm = h_ref.shape[0]
    halo = BF16_SUBLANES
    ext_ref[0:halo, :] = hp_ref[...]
    ext_ref[halo:halo + bm, :] = h_ref[...]
    ext_ref[halo + bm:, :] = hn_ref[...]


def _conv_glu_matmul_chunk(ext_ref, wg_ref, wv_ref, c, acc):
    halo = BF16_SUBLANES
    bm = ext_ref.shape[0] - 2 * halo
    kc = ext_ref.shape[1] // PIPELINE_CHUNKS
    ks = slice(c * kc, (c + 1) * kc)
    gd = jnp.dot(ext_ref[:, ks], wg_ref[ks, :], preferred_element_type=F32)
    vd = jnp.dot(ext_ref[halo:halo + bm, ks], wv_ref[ks, :], preferred_element_type=F32)
    return (gd, vd) if c == 0 else (acc[0] + gd, acc[1] + vd)


def _conv_glu_epilogue_chunk(gate_ref, val_ref, cw_ref, cb_ref, o_ref, c, row0, seq):
    halo = BF16_SUBLANES
    pad = F32_SUBLANES
    bm = val_ref.shape[0]
    rc = bm // PIPELINE_CHUNKS // EPILOGUE_SPLIT
    cw = cw_ref[...]
    for part in range(EPILOGUE_SPLIT):
        first = (c * EPILOGUE_SPLIT + part) * rc
        r0 = halo + first
        g = gate_ref[r0 - pad:r0 + rc + pad, :]
        row = lax.broadcasted_iota(jnp.int32, (rc, 1), 0) + first
        pos = (row0 + row) % seq
        g_prev = jnp.where(pos == 0, 0.0, pltpu.roll(g, 1, axis=0)[pad:pad + rc])
        g_next = jnp.where(pos == seq - 1, 0.0, pltpu.roll(g, rc + 2 * pad - 1, axis=0)[pad:pad + rc])
        conv = g_prev * cw[0:1] + g[pad:pad + rc] * cw[1:2] + g_next * cw[2:3] + cb_ref[...]
        rs = slice(first, first + rc)
        o_ref[rs, :] = (_gelu_exact(conv) * val_ref[rs, :]).astype(o_ref.dtype)


def _conv_glu_head_kernel(h_ref, hp_ref, hn_ref, wg_ref, wv_ref, cw_ref, cb_ref, buf_ref,
                          o_ref, wgo_ref, wvo_ref, ext_ref, wgb0, wgb1, wvb0, wvb1, gate0, gate1, val0, val1, *, seq, nj):
    j = pl.program_id(0)
    kc = h_ref.shape[1] // PIPELINE_CHUNKS
    wgb, wvb, gate_s, val_s = (wgb0, wgb1), (wvb0, wvb1), (gate0, gate1), (val0, val1)

    pl.when(j == 0)(lambda: _build_halo_tile(ext_ref, h_ref, hp_ref, hn_ref))

    def stage(p, do_cast, do_matmul, do_epilogue):
        q = 1 - p
        acc = None
        for c in range(PIPELINE_CHUNKS):
            if do_matmul:
                acc = _conv_glu_matmul_chunk(ext_ref, wgb[q], wvb[q], c, acc)
            if do_cast:
                ks = slice(c * kc, (c + 1) * kc)
                wg = wg_ref[ks, :].astype(BF16)
                wv = wv_ref[ks, :].astype(BF16)
                wgb[p][ks, :] = wg
                wvb[p][ks, :] = wv
                wgo_ref[ks, :] = wg
                wvo_ref[ks, :] = wv
            if do_epilogue:
                _conv_glu_epilogue_chunk(gate_s[p], val_s[p], cw_ref, cb_ref, o_ref, c, 0, seq)
        if do_matmul:
            gate_s[q][...], val_s[q][...] = acc

    pl.when(j == 0)(lambda: stage(0, True, False, False))
    pl.when(j == 1)(lambda: stage(1, True, True, False))
    steady = jnp.logical_and(j >= 2, j < nj)
    pl.when(jnp.logical_and(steady, j % 2 == 0))(lambda: stage(0, True, True, True))
    pl.when(jnp.logical_and(steady, j % 2 == 1))(lambda: stage(1, True, True, True))
    pl.when(j == nj)(lambda: stage(nj % 2, False, True, True))
    pl.when(j == nj + 1)(lambda: stage((nj + 1) % 2, False, False, True))


def _conv_glu_tail_kernel(h_ref, hp_ref, hn_ref, wg_ref, wv_ref, cw_ref, cb_ref, buf_ref, o_ref,
                          ext_ref, gate0, gate1, val0, val1, *, seq, nj):
    row0 = (pl.program_id(0) + 1) * h_ref.shape[0]
    j = pl.program_id(1)
    gate_s, val_s = (gate0, gate1), (val0, val1)

    pl.when(j == 0)(lambda: _build_halo_tile(ext_ref, h_ref, hp_ref, hn_ref))

    def stage(p, do_matmul, do_epilogue):
        q = 1 - p
        width = o_ref.shape[1] // TAIL_SUBTILES
        for t in range(TAIL_SUBTILES):
            cols = pl.ds(t * width, width)
            acc = None
            for c in range(PIPELINE_CHUNKS):
                if do_matmul:
                    acc = _conv_glu_matmul_chunk(ext_ref, wg_ref.at[:, cols], wv_ref.at[:, cols], c, acc)
                if do_epilogue:
                    _conv_glu_epilogue_chunk(gate_s[q].at[:, cols], val_s[q].at[:, cols], cw_ref.at[:, cols],
                                             cb_ref.at[:, cols], o_ref.at[:, cols], c, row0, seq)
            if do_matmul:
                gate_s[p][:, cols], val_s[p][:, cols] = acc

    pl.when(j == 0)(lambda: stage(0, True, False))
    steady = jnp.logical_and(j >= 1, j < nj)
    pl.when(jnp.logical_and(steady, j % 2 == 0))(lambda: stage(0, True, True))
    pl.when(jnp.logical_and(steady, j % 2 == 1))(lambda: stage(1, True, True))
    pl.when(j == nj)(lambda: stage(nj % 2, False, True))


def conv_glu_up(h, w_up, l, conv_w, conv_b, seq, buf, bm=1024, bn_head=256, bn=256 * TAIL_SUBTILES):
    m, k = h.shape
    dff = w_up.shape[2] // 2
    assert buf.shape == (m, dff) and buf.dtype == BF16
    halo = BF16_SUBLANES
    rb = bm // halo
    last = m // halo - 1
    conv_b = conv_b.reshape(1, dff)
    assert bm == HEAD_ROWS
    prev_tile = lambda i: jnp.maximum(i * rb - 1, 0)
    next_tile = lambda i: jnp.minimum((i + 1) * rb, last)
    ext = pltpu.VMEM((bm + 2 * halo, k), BF16)

    nj = dff // bn_head
    assert nj >= 2
    w_tile = lambda j: jnp.minimum(j, nj - 1)
    o_tile = lambda j: jnp.clip(j - 2, 0, nj - 1)
    wb = pltpu.VMEM((k, bn_head), BF16)
    gate = pltpu.VMEM((bm + 2 * halo, bn_head), F32)
    val = pltpu.VMEM((bm, bn_head), F32)
    wb_out = jax.ShapeDtypeStruct((k, dff), BF16)
    buf, wgb, wvb = pl.pallas_call(
        functools.partial(_conv_glu_head_kernel, seq=seq, nj=nj),
        grid=(nj + 2,),
        in_specs=[_resident((bm, k), lambda j: (0, 0)),
                  _resident((halo, k), lambda j: (prev_tile(0), 0)),
                  _resident((halo, k), lambda j: (next_tile(0), 0)),
                  pl.BlockSpec((None, k, bn_head), lambda j: (l, 0, w_tile(j))),
                  pl.BlockSpec((None, k, bn_head), lambda j: (l, 0, w_tile(j) + nj)),
                  pl.BlockSpec((CONV_WIDTH, bn_head), lambda j: (0, o_tile(j))),
                  pl.BlockSpec((1, bn_head), lambda j: (0, o_tile(j))),
                  _ALIASED],
        out_specs=[pl.BlockSpec((bm, bn_head), lambda j: (0, o_tile(j))),
                   pl.BlockSpec((k, bn_head), lambda j: (0, w_tile(j))),
                   pl.BlockSpec((k, bn_head), lambda j: (0, w_tile(j)))],
        out_shape=[jax.ShapeDtypeStruct((m, dff), BF16), wb_out, wb_out],
        input_output_aliases={7: 0},
        scratch_shapes=[ext, wb, wb, wb, wb, gate, gate, val, val],
        compiler_params=_params(56, ("arbitrary",)),
        name="conv_glu_up_head",
    )(h, h, h, w_up, w_up, conv_w, conv_b, buf)

    nj = dff // bn
    w_tile = lambda j: jnp.minimum(j, nj - 1)
    o_tile = lambda j: jnp.clip(j - 1, 0, nj - 1)
    gate = pltpu.VMEM((bm + 2 * halo, bn), F32)
    val = pltpu.VMEM((bm, bn), F32)
    return pl.pallas_call(
        functools.partial(_conv_glu_tail_kernel, seq=seq, nj=nj),
        grid=(m // bm - 1, nj + 1),
        in_specs=[_resident((bm, k), lambda i, j: (i + 1, 0)),
                  _resident((halo, k), lambda i, j: (prev_tile(i + 1), 0)),
                  _resident((halo, k), lambda i, j: (next_tile(i + 1), 0)),
                  pl.BlockSpec((k, bn), lambda i, j: (0, w_tile(j))),
                  pl.BlockSpec((k, bn), lambda i, j: (0, w_tile(j))),
                  pl.BlockSpec((CONV_WIDTH, bn), lambda i, j: (0, o_tile(j))),
                  pl.BlockSpec((1, bn), lambda i, j: (0, o_tile(j))),
                  _ALIASED],
        out_specs=pl.BlockSpec((bm, bn), lambda i, j: (i + 1, o_tile(j))),
        out_shape=jax.ShapeDtypeStruct((m, dff), BF16),
        input_output_aliases={7: 0},
        scratch_shapes=[ext, gate, gate, val, val],
        compiler_params=_params(56, ("parallel", "arbitrary")),
        name="conv_glu_up",
    )(h, h, h, wgb, wvb, conv_w, conv_b, buf)


def kernel(x, norm1_gain, w_in, w_fourier_out, lambdas, subln_gain, rel_bias_table, w_attn_out,
           w_gate, b_gate, w_o, norm2_gain, w_up, conv_w, conv_b, w_down, final_norm_gain):
    batch, seq, d_model = x.shape
    depth = w_in.shape[0]
    n_heads = rel_bias_table.shape[1]
    d_attn = n_heads * HEAD_WIDTH
    d_in = w_in.shape[2]
    tq = 512

    cs, sn, chan = _dft_constants(seq, FOURIER_GROUP_DIM)
    col = np.ones((1, d_in), np.float32)
    col[:, D_FOURIER:D_FOURIER + d_attn] = ATTN_HEAD_DIM ** -0.5 * LOG2_E
    col_scale = jnp.asarray(col)
    wb = bias_window(rel_bias_table, seq, tq)

    x = x.reshape(batch * seq, d_model)
    m = batch * seq
    u = jnp.zeros((m, d_in), BF16)
    mixed = jnp.zeros((m, d_model), BF16)
    act = jnp.zeros((m, w_up.shape[2] // 2), BF16)
    for l in range(depth):
        lam_init = 0.8 - 0.6 * math.exp(-0.3 * l)
        h = rmsnorm(x, norm1_gain[l], BF16)
        u = in_proj(h, w_in, l, col_scale, u)
        t1, t2 = fourier_chan(u, chan)
        f = fourier_seq(cs, sn, t1, t2, seq)
        a = diff_attention(u, wb, lambdas[l], subln_gain[l], lam_init, seq, n_heads, tq)
        mixed = gated_mix(h, f, a, w_gate, w_fourier_out, w_attn_out, l, b_gate[l], mixed)
        x = proj_residual(mixed, w_o, l, x, head_rows=2 * HEAD_ROWS, bn_head=256, bm=1024, bn=1024)
        h2 = rmsnorm(x, norm2_gain[l], BF16)
        act = conv_glu_up(h2, w_up, l, conv_w[l], conv_b[l], seq, act)
        x = proj_residual(act, w_down, l, x, head_rows=HEAD_ROWS, bn_head=256, bm=1024, bn=256)
    out = rmsnorm(x, final_norm_gain, F32)
    return out.reshape(batch, seq, d_model)
```

```python
import functools
import math

import numpy as np
import jax
import jax.numpy as jnp
from jax import lax
from jax.experimental import pallas as pl
from jax.experimental.pallas import tpu as pltpu

F32 = jnp.float32
BF16 = jnp.bfloat16

EPS = 1e-6
N_FOURIER_GROUPS = 4
FOURIER_GROUP_DIM = 256
D_FOURIER = N_FOURIER_GROUPS * FOURIER_GROUP_DIM
ATTN_HEAD_DIM = 128
HEAD_WIDTH = 2 * ATTN_HEAD_DIM
N_REL_BUCKETS = 32
REL_MAX_DISTANCE = 128
LOG2_E = math.log2(math.e)
ATTN_KEY_CHUNK = 512
CONV_WIDTH = 3
BF16_SUBLANES = 16
F32_SUBLANES = 8
PIPELINE_CHUNKS = 16
MXU_DEPTH = 256
EPILOGUE_SPLIT = 2
TAIL_SUBTILES = 2

MIB = 1024 * 1024


def _params(vmem_mib, semantics):
    return pltpu.CompilerParams(dimension_semantics=semantics, vmem_limit_bytes=vmem_mib * MIB)


def _resident(shape, index_map):
    return pl.BlockSpec(shape, index_map, pipeline_mode=pl.Buffered(1))


def _rmsnorm_kernel(x_ref, g_ref, o_ref):
    x = x_ref[...]
    ms = jnp.mean(x * x, axis=-1, keepdims=True)
    o_ref[...] = (x * lax.rsqrt(ms + EPS) * g_ref[...]).astype(o_ref.dtype)


def rmsnorm(x, gain, out_dtype, bm=512):
    m, d = x.shape
    return pl.pallas_call(
        _rmsnorm_kernel,
        grid=(m // bm,),
        in_specs=[pl.BlockSpec((bm, d), lambda i: (i, 0)),
                  pl.BlockSpec((1, d), lambda i: (0, 0))],
        out_specs=pl.BlockSpec((bm, d), lambda i: (i, 0)),
        out_shape=jax.ShapeDtypeStruct((m, d), out_dtype),
        compiler_params=_params(40, ("parallel",)),
        name="rmsnorm",
    )(x, gain.reshape(1, d))


HEAD_ROWS = 1024
_ALIASED = pl.BlockSpec(memory_space=pl.ANY)


def _cast_dot(lhs_ref, w_ref, wb_ref):
    kc = MXU_DEPTH
    acc = None
    for c in range(w_ref.shape[0] // kc):
        ks = slice(c * kc, (c + 1) * kc)
        wb_ref[ks, :] = w_ref[ks, :].astype(BF16)
        part = jnp.dot(lhs_ref[:, ks], wb_ref[ks, :], preferred_element_type=F32)
        acc = part if c == 0 else acc + part
    return acc


def _in_proj_head_kernel(h_ref, w_ref, s_ref, buf_ref, o_ref, wb_ref):
    o_ref[...] = (_cast_dot(h_ref, w_ref, wb_ref) * s_ref[...]).astype(o_ref.dtype)


def _in_proj_tail_kernel(h_ref, wb_ref, s_ref, buf_ref, o_ref):
    acc = jnp.dot(h_ref[...], wb_ref[...], preferred_element_type=F32)
    o_ref[...] = (acc * s_ref[...]).astype(o_ref.dtype)


def in_proj(h, w, l, col_scale, buf, head_rows=2 * HEAD_ROWS, bm=1024, bn_head=512, bn=1024):
    m, k = h.shape
    n = w.shape[2]
    assert buf.shape == (m, n) and buf.dtype == BF16
    buf, wb = pl.pallas_call(
        _in_proj_head_kernel,
        grid=(n // bn_head,),
        in_specs=[_resident((head_rows, k), lambda j: (0, 0)),
                  pl.BlockSpec((None, k, bn_head), lambda j: (l, 0, j)),
                  pl.BlockSpec((1, bn_head), lambda j: (0, j)),
                  _ALIASED],
        out_specs=[pl.BlockSpec((head_rows, bn_head), lambda j: (0, j)),
                   pl.BlockSpec((k, bn_head), lambda j: (0, j))],
        out_shape=[jax.ShapeDtypeStruct((m, n), BF16), jax.ShapeDtypeStruct((k, n), BF16)],
        input_output_aliases={3: 0},
        compiler_params=_params(48, ("arbitrary",)),
        name="in_proj_head",
    )(h, w, col_scale, buf)
    skip = head_rows // bm
    return pl.pallas_call(
        _in_proj_tail_kernel,
        grid=(m // bm - skip, n // bn),
        in_specs=[pl.BlockSpec((bm, k), lambda i, j: (i + skip, 0)),
                  pl.BlockSpec((k, bn), lambda i, j: (0, j)),
                  pl.BlockSpec((1, bn), lambda i, j: (0, j)),
                  _ALIASED],
        out_specs=pl.BlockSpec((bm, bn), lambda i, j: (i + skip, j)),
        out_shape=jax.ShapeDtypeStruct((m, n), BF16),
        input_output_aliases={3: 0},
        compiler_params=_params(48, ("parallel", "arbitrary")),
        name="in_proj",
    )(h, wb, col_scale, buf)


def _dft_constants(seq, dim):
    def cos_sin(n):
        idx = np.arange(n, dtype=np.int64)
        ang = 2.0 * np.pi * ((idx[:, None] * idx[None, :]) % n).astype(np.float64) / n
        return np.cos(ang), np.sin(ang)
    cs, ss = cos_sin(seq)
    cc, sc = cos_sin(dim)
    chan = np.concatenate([cc, sc], axis=1)
    return (jnp.asarray(cs, BF16), jnp.asarray(-ss, BF16), jnp.asarray(chan, BF16))


def _fourier_chan_kernel(u_ref, c_ref, t1_ref, t2_ref):
    c = c_ref[...]
    gd = FOURIER_GROUP_DIM
    for g in range(N_FOURIER_GROUPS):
        t = jnp.dot(u_ref[:, g * gd:(g + 1) * gd], c, preferred_element_type=F32)
        t1_ref[:, g * gd:(g + 1) * gd] = t[:, :gd].astype(t1_ref.dtype)
        t2_ref[:, g * gd:(g + 1) * gd] = t[:, gd:].astype(t2_ref.dtype)


def fourier_chan(u, chan, bm=1024):
    m = u.shape[0]
    gd = FOURIER_GROUP_DIM
    out = jax.ShapeDtypeStruct((m, D_FOURIER), BF16)
    return pl.pallas_call(
        _fourier_chan_kernel,
        grid=(m // bm,),
        in_specs=[pl.BlockSpec((bm, D_FOURIER), lambda i: (i, 0)),
                  pl.BlockSpec((gd, 2 * gd), lambda i: (0, 0))],
        out_specs=[pl.BlockSpec((bm, D_FOURIER), lambda i: (i, 0)),
                   pl.BlockSpec((bm, D_FOURIER), lambda i: (i, 0))],
        out_shape=[out, out],
        compiler_params=_params(32, ("parallel",)),
        name="fourier_chan",
    )(u, chan)


def _fourier_seq_kernel(cs_ref, sn_ref, t1_ref, t2_ref, o_ref, *, scale):
    acc = jnp.dot(cs_ref[...], t1_ref[...], preferred_element_type=F32)
    acc += jnp.dot(sn_ref[...], t2_ref[...], preferred_element_type=F32)
    o_ref[...] = (acc * scale).astype(o_ref.dtype)


def fourier_seq(cs, sn, t1, t2, seq, bm=512):
    m, n = t1.shape
    nb = m // seq
    ni = seq // bm
    scale = 1.0 / math.sqrt(seq * FOURIER_GROUP_DIM)
    return pl.pallas_call(
        functools.partial(_fourier_seq_kernel, scale=scale),
        grid=(nb, ni),
        in_specs=[pl.BlockSpec((bm, seq), lambda b, i: (i, 0)),
                  pl.BlockSpec((bm, seq), lambda b, i: (i, 0)),
                  pl.BlockSpec((seq, n), lambda b, i: (b, 0)),
                  pl.BlockSpec((seq, n), lambda b, i: (b, 0))],
        out_specs=pl.BlockSpec((bm, n), lambda b, i: (b * ni + i, 0)),
        out_shape=jax.ShapeDtypeStruct((m, n), BF16),
        compiler_params=_params(40, ("parallel", "arbitrary")),
        name="fourier_seq",
    )(cs, sn, t1, t2)


def _bias_window_kernel(tab_ref, o_ref, *, tq, nq):
    h = pl.program_id(0)
    half = N_REL_BUCKETS // 2
    max_exact = half // 2
    width = o_ref.shape[2]
    lo = (nq - 1) * tq - REL_MAX_DISTANCE
    hi = nq * tq + REL_MAX_DISTANCE
    o_ref[0, :, :lo] = jnp.full((tq, lo), tab_ref[half - 1, h] * LOG2_E, F32)
    o_ref[0, :, hi:] = jnp.full((tq, width - hi), tab_ref[N_REL_BUCKETS - 1, h] * LOG2_E, F32)

    shape = (tq, hi - lo)
    i = lax.broadcasted_iota(jnp.int32, shape, 0)
    c = lax.broadcasted_iota(jnp.int32, shape, 1) + lo
    n = i + (nq - 1) * tq - c
    ret = jnp.where(n < 0, half, 0)
    n = jnp.abs(n)
    n2 = n * n
    large = jnp.full(shape, max_exact, jnp.int32)
    for j in range(1, half - max_exact):
        large += (n2 >= (max_exact * max_exact) * (2 ** j)).astype(jnp.int32)
    bucket = ret + jnp.where(n < max_exact, n, large)
    acc = jnp.zeros(shape, F32)
    for b in range(N_REL_BUCKETS):
        acc = jnp.where(bucket == b, tab_ref[b, h] * LOG2_E, acc)
    o_ref[0, :, lo:hi] = acc


def bias_window(rel_table, seq, tq):
    nq = seq // tq
    nh = rel_table.shape[1]
    width = 2 * seq - tq
    assert tq >= REL_MAX_DISTANCE and nq >= 2
    return pl.pallas_call(
        functools.partial(_bias_window_kernel, tq=tq, nq=nq),
        grid=(nh,),
        in_specs=[pl.BlockSpec(memory_space=pltpu.SMEM)],
        out_specs=pl.BlockSpec((1, tq, width), lambda h: (h, 0, 0)),
        out_shape=jax.ShapeDtypeStruct((nh, tq, width), F32),
        compiler_params=_params(40, ("parallel",)),
        name="bias_window",
    )(rel_table)


def _lane_tile_reduce(x, op):
    lanes = 128
    out = x[:, 0:lanes]
    for t in range(1, x.shape[1] // lanes):
        out = op(out, x[:, t * lanes:(t + 1) * lanes])
    return out


def _attn_kernel(q_ref, k_ref, v_ref, wb_ref, lam_ref, sg_ref, o_ref, s0_ref, s1_ref, mx0_ref, mx1_ref,
                 *, lam_init, tq, nq, seq):
    i = pl.program_id(2)
    d = ATTN_HEAD_DIM
    ck = ATTN_KEY_CHUNK
    nc = seq // ck
    s_scr = (s0_ref, s1_ref)
    mx_scr = (mx0_ref, mx1_ref)
    chunks = [(m, c) for m in range(2) for c in range(nc)]

    def logits_chunk(p, m, c, run_max):
        off = pl.multiple_of((nq - 1 - i) * tq + c * ck, tq)
        s = lax.dot_general(q_ref[:, m * d:(m + 1) * d], k_ref[c * ck:(c + 1) * ck, m * d:(m + 1) * d],
                            (((1,), (1,)), ((), ())), preferred_element_type=F32)
        s = s + wb_ref[0, :, pl.ds(off, ck)]
        s_scr[p][m, :, c * ck:(c + 1) * ck] = s
        part = _lane_tile_reduce(s, jnp.maximum)
        run_max = part if c == 0 else jnp.maximum(run_max, part)
        if c == nc - 1:
            mx_scr[p][m] = jnp.broadcast_to(jnp.max(run_max, axis=-1, keepdims=True), run_max.shape)
        return run_max

    def probs_chunk(p, m, c, state):
        row_max, row_sum, out = state
        if c == 0:
            row_max = jnp.concatenate([mx_scr[p][m]] * (ck // 128), axis=1)
        e = jnp.exp2(s_scr[p][m, :, c * ck:(c + 1) * ck] - row_max)
        part = _lane_tile_reduce(e, jnp.add)
        pv = jnp.dot(e.astype(BF16), v_ref[c * ck:(c + 1) * ck, :], preferred_element_type=F32)
        return (row_max, part if c == 0 else row_sum + part, pv if c == 0 else out + pv)

    def finish(results):
        (_, sum0, out0), (_, sum1, out1) = results
        lam_p = lam_ref[...]
        lam = (jnp.exp(jnp.sum(lam_p[0:1] * lam_p[1:2], axis=-1, keepdims=True))
               - jnp.exp(jnp.sum(lam_p[2:3] * lam_p[3:4], axis=-1, keepdims=True)) + lam_init)
        l0 = jnp.sum(sum0, axis=-1, keepdims=True)
        l1 = jnp.sum(sum1, axis=-1, keepdims=True)
        o = out0 * (1.0 / l0) - out1 * (lam / l1)
        ms = jnp.mean(o * o, axis=-1, keepdims=True)
        o = o * lax.rsqrt(ms + EPS) * sg_ref[...] * (1.0 - lam_init)
        o_ref[...] = o.astype(o_ref.dtype)

    def stage(p, do_logits, do_probs):
        lead = 2 if do_logits and do_probs else 0
        run_max = None
        state = (None, None, None)
        results = []
        for idx in range(len(chunks) + lead):
            if do_probs and idx < len(chunks):
                m, c = chunks[idx]
                state = probs_chunk(1 - p, m, c, state)
                if c == nc - 1:
                    results.append(state)
                if idx == len(chunks) - 1:
                    finish(results)
            if do_logits and idx >= lead:
                m, c = chunks[idx - lead]
                run_max = logits_chunk(p, m, c, run_max)

    pl.when(i == 0)(lambda: stage(0, True, False))
    steady = jnp.logical_and(i >= 1, i < nq)
    pl.when(jnp.logical_and(steady, i % 2 == 0))(lambda: stage(0, True, True))
    pl.when(jnp.logical_and(steady, i % 2 == 1))(lambda: stage(1, True, True))
    pl.when(i == nq)(lambda: stage(nq % 2, False, True))


def diff_attention(u, wb, lambdas, sub_gain, lam_init, seq, n_heads, tq):
    m = u.shape[0]
    nb = m // seq
    nq = seq // tq
    hw = HEAD_WIDTH
    q_blk = D_FOURIER // hw
    k_blk = q_blk + n_heads
    v_blk = k_blk + n_heads
    assert ATTN_KEY_CHUNK % tq == 0 and seq % ATTN_KEY_CHUNK == 0
    logits = pltpu.VMEM((2, tq, seq), F32)
    maxima = pltpu.VMEM((2, tq, 128), F32)
    return pl.pallas_call(
        functools.partial(_attn_kernel, lam_init=lam_init, tq=tq, nq=nq, seq=seq),
        grid=(nb, n_heads, nq + 1),
        in_specs=[pl.BlockSpec((tq, hw), lambda b, h, i: (b * nq + jnp.minimum(i, nq - 1), q_blk + h)),
                  pl.BlockSpec((seq, hw), lambda b, h, i: (b, k_blk + h)),
                  pl.BlockSpec((seq, hw), lambda b, h, i: (b, v_blk + h)),
                  pl.BlockSpec((1, tq, 2 * seq - tq), lambda b, h, i: (h, 0, 0)),
                  pl.BlockSpec((4, ATTN_HEAD_DIM), lambda b, h, i: (0, 0)),
                  pl.BlockSpec((1, hw), lambda b, h, i: (0, 0))],
        out_specs=pl.BlockSpec((tq, hw), lambda b, h, i: (b * nq + jnp.maximum(i - 1, 0), h)),
        out_shape=jax.ShapeDtypeStruct((m, n_heads * hw), BF16),
        scratch_shapes=[logits, logits, maxima, maxima],
        compiler_params=_params(48, ("parallel", "parallel", "arbitrary")),
        name="diff_attention",
    )(u, u, u, wb, lambdas, sub_gain.reshape(1, hw))


def _mix_head_kernel(h_ref, f_ref, a_ref, wg0_ref, wg1_ref, wf_ref, wa_ref, b0_ref, b1_ref, buf_ref,
                     o_ref, wg0b_ref, wg1b_ref, wfb_ref, wab_ref):
    g0 = jax.nn.sigmoid(_cast_dot(h_ref, wg0_ref, wg0b_ref) + b0_ref[...])
    t0 = g0 * _cast_dot(f_ref, wf_ref, wfb_ref)
    g1 = jax.nn.sigmoid(_cast_dot(h_ref, wg1_ref, wg1b_ref) + b1_ref[...])
    o_ref[...] = (t0 + g1 * _cast_dot(a_ref, wa_ref, wab_ref)).astype(o_ref.dtype)


def _mix_tail_kernel(h_ref, f_ref, a_ref, wg0_ref, wg1_ref, wf_ref, wa_ref, b0_ref, b1_ref, buf_ref, o_ref):
    h = h_ref[...]
    dot = functools.partial(jnp.dot, preferred_element_type=F32)
    g0 = jax.nn.sigmoid(dot(h, wg0_ref[...]) + b0_ref[...])
    t0 = g0 * dot(f_ref[...], wf_ref[...])
    g1 = jax.nn.sigmoid(dot(h, wg1_ref[...]) + b1_ref[...])
    o_ref[...] = (t0 + g1 * dot(a_ref[...], wa_ref[...])).astype(o_ref.dtype)


def gated_mix(h, f, a, w_gate, w_f, w_a, l, b_gate, buf, bm=1024, bn=256):
    m, d = h.shape
    df, da = f.shape[1], a.shape[1]
    nj = d // bn
    b_gate = b_gate.reshape(1, 2 * d)
    assert buf.shape == (m, d) and buf.dtype == BF16
    buf, wg0b, wg1b, wfb, wab = pl.pallas_call(
        _mix_head_kernel,
        grid=(nj,),
        in_specs=[_resident((HEAD_ROWS, d), lambda j: (0, 0)),
                  _resident((HEAD_ROWS, df), lambda j: (0, 0)),
                  _resident((HEAD_ROWS, da), lambda j: (0, 0)),
                  pl.BlockSpec((None, d, bn), lambda j: (l, 0, j)),
                  pl.BlockSpec((None, d, bn), lambda j: (l, 0, j + nj)),
                  pl.BlockSpec((None, df, bn), lambda j: (l, 0, j)),
                  pl.BlockSpec((None, da, bn), lambda j: (l, 0, j)),
                  pl.BlockSpec((1, bn), lambda j: (0, j)),
                  pl.BlockSpec((1, bn), lambda j: (0, j + nj)),
                  _ALIASED],
        out_specs=[pl.BlockSpec((HEAD_ROWS, bn), lambda j: (0, j)),
                   pl.BlockSpec((d, bn), lambda j: (0, j)),
                   pl.BlockSpec((d, bn), lambda j: (0, j)),
                   pl.BlockSpec((df, bn), lambda j: (0, j)),
                   pl.BlockSpec((da, bn), lambda j: (0, j))],
        out_shape=[jax.ShapeDtypeStruct((m, d), BF16), jax.ShapeDtypeStruct((d, d), BF16),
                   jax.ShapeDtypeStruct((d, d), BF16), jax.ShapeDtypeStruct((df, d), BF16),
                   jax.ShapeDtypeStruct((da, d), BF16)],
        input_output_aliases={9: 0},
        compiler_params=_params(56, ("arbitrary",)),
        name="gated_mix_head",
    )(h, f, a, w_gate, w_gate, w_f, w_a, b_gate, b_gate, buf)
    assert bm == HEAD_ROWS
    wcol = lambda i, j: (0, j)
    return pl.pallas_call(
        _mix_tail_kernel,
        grid=(m // bm - 1, nj),
        in_specs=[pl.BlockSpec((bm, d), lambda i, j: (i + 1, 0)),
                  pl.BlockSpec((bm, df), lambda i, j: (i + 1, 0)),
                  pl.BlockSpec((bm, da), lambda i, j: (i + 1, 0)),
                  pl.BlockSpec((d, bn), wcol),
                  pl.BlockSpec((d, bn), wcol),
                  pl.BlockSpec((df, bn), wcol),
                  pl.BlockSpec((da, bn), wcol),
                  pl.BlockSpec((1, bn), lambda i, j: (0, j)),
                  pl.BlockSpec((1, bn), lambda i, j: (0, j + nj)),
                  _ALIASED],
        out_specs=pl.BlockSpec((bm, bn), lambda i, j: (i + 1, j)),
        out_shape=jax.ShapeDtypeStruct((m, d), BF16),
        input_output_aliases={9: 0},
        compiler_params=_params(56, ("parallel", "arbitrary")),
        name="gated_mix",
    )(h, f, a, wg0b, wg1b, wfb, wab, b_gate, b_gate, buf)


def _proj_residual_head_kernel(a_ref, w_ref, r_ref, o_ref, wb_ref):
    o_ref[...] = r_ref[...] + _cast_dot(a_ref, w_ref, wb_ref)


def _proj_residual_tail_kernel(a_ref, wb_ref, r_ref, o_ref):
    o_ref[...] = r_ref[...] + jnp.dot(a_ref[...], wb_ref[...], preferred_element_type=F32)


def proj_residual(a, w, l, res, head_rows, bn_head, bm, bn):
    m, k = a.shape
    n = w.shape[2]
    res, wb = pl.pallas_call(
        _proj_residual_head_kernel,
        grid=(n // bn_head,),
        in_specs=[_resident((head_rows, k), lambda j: (0, 0)),
                  pl.BlockSpec((None, k, bn_head), lambda j: (l, 0, j)),
                  pl.BlockSpec((head_rows, bn_head), lambda j: (0, j))],
        out_specs=[pl.BlockSpec((head_rows, bn_head), lambda j: (0, j)),
                   pl.BlockSpec((k, bn_head), lambda j: (0, j))],
        out_shape=[jax.ShapeDtypeStruct((m, n), F32), jax.ShapeDtypeStruct((k, n), BF16)],
        input_output_aliases={2: 0},
        compiler_params=_params(56, ("arbitrary",)),
        name="proj_residual_head",
    )(a, w, res)
    skip = head_rows // bm
    return pl.pallas_call(
        _proj_residual_tail_kernel,
        grid=(m // bm - skip, n // bn),
        in_specs=[pl.BlockSpec((bm, k), lambda i, j: (i + skip, 0)),
                  pl.BlockSpec((k, bn), lambda i, j: (0, j)),
                  pl.BlockSpec((bm, bn), lambda i, j: (i + skip, j))],
        out_specs=pl.BlockSpec((bm, bn), lambda i, j: (i + skip, j)),
        out_shape=jax.ShapeDtypeStruct((m, n), F32),
        input_output_aliases={2: 0},
        compiler_params=_params(56, ("parallel", "arbitrary")),
        name="proj_residual",
    )(a, wb, res)


def _gelu_exact(x):
    return 0.5 * x * (1.0 + lax.erf(x * math.sqrt(0.5)))


def _build_halo_tile(ext_ref, h_ref, hp_ref, hn_ref):
    bm = h_ref.shape[0]
    halo = BF16_SUBLANES
    ext_ref[0:halo, :] = hp_ref[...]
    ext_ref[halo:halo + bm, :] = h_ref[...]
    ext_ref[halo + bm:, :] = hn_ref[...]


def _conv_glu_matmul_chunk(ext_ref, wg_ref, wv_ref, c, acc):
    halo = BF16_SUBLANES
    bm = ext_ref.shape[0] - 2 * halo
    kc = ext_ref.shape[1] // PIPELINE_CHUNKS
    ks = slice(c * kc, (c + 1) * kc)
    gd = jnp.dot(ext_ref[:, ks], wg_ref[ks, :], preferred_element_type=F32)
    vd = jnp.dot(ext_ref[halo:halo + bm, ks], wv_ref[ks, :], preferred_element_type=F32)
    return (gd, vd) if c == 0 else (acc[0] + gd, acc[1] + vd)


def _conv_glu_epilogue_chunk(gate_ref, val_ref, cw_ref, cb_ref, o_ref, c, row0, seq):
    halo = BF16_SUBLANES
    pad = F32_SUBLANES
    bm = val_ref.shape[0]
    rc = bm // PIPELINE_CHUNKS // EPILOGUE_SPLIT
    cw = cw_ref[...]
    tiles_align = seq % bm == 0
    for part in range(EPILOGUE_SPLIT):
        first = (c * EPILOGUE_SPLIT + part) * rc
        r0 = halo + first
        g = gate_ref[r0 - pad:r0 + rc + pad, :]
        g_prev = pltpu.roll(g, 1, axis=0)[pad:pad + rc]
        g_next = pltpu.roll(g, rc + 2 * pad - 1, axis=0)[pad:pad + rc]
        pos = (row0 + first + lax.broadcasted_iota(jnp.int32, (rc, 1), 0)) % seq
        if not tiles_align or first == 0:
            g_prev = jnp.where(pos == 0, 0.0, g_prev)
        if not tiles_align or first + rc == bm:
            g_next = jnp.where(pos == seq - 1, 0.0, g_next)
        conv = g_prev * cw[0:1] + g[pad:pad + rc] * cw[1:2] + g_next * cw[2:3] + cb_ref[...]
        rs = slice(first, first + rc)
        o_ref[rs, :] = (_gelu_exact(conv) * val_ref[rs, :]).astype(o_ref.dtype)


def _conv_glu_head_kernel(h_ref, hp_ref, hn_ref, wg_ref, wv_ref, cw_ref, cb_ref, buf_ref,
                          o_ref, wgo_ref, wvo_ref, ext_ref, wgb0, wgb1, wvb0, wvb1, gate0, gate1, val0, val1, *, seq, nj):
    j = pl.program_id(0)
    kc = h_ref.shape[1] // PIPELINE_CHUNKS
    wgb, wvb, gate_s, val_s = (wgb0, wgb1), (wvb0, wvb1), (gate0, gate1), (val0, val1)

    pl.when(j == 0)(lambda: _build_halo_tile(ext_ref, h_ref, hp_ref, hn_ref))

    def stage(p, do_cast, do_matmul, do_epilogue):
        q = 1 - p
        acc = None
        for c in range(PIPELINE_CHUNKS):
            if do_matmul:
                acc = _conv_glu_matmul_chunk(ext_ref, wgb[q], wvb[q], c, acc)
            if do_cast:
                ks = slice(c * kc, (c + 1) * kc)
                wg = wg_ref[ks, :].astype(BF16)
                wv = wv_ref[ks, :].astype(BF16)
                wgb[p][ks, :] = wg
                wvb[p][ks, :] = wv
                wgo_ref[ks, :] = wg
                wvo_ref[ks, :] = wv
            if do_epilogue:
                _conv_glu_epilogue_chunk(gate_s[p], val_s[p], cw_ref, cb_ref, o_ref, c, 0, seq)
        if do_matmul:
            gate_s[q][...], val_s[q][...] = acc

    pl.when(j == 0)(lambda: stage(0, True, False, False))
    pl.when(j == 1)(lambda: stage(1, True, True, False))
    steady = jnp.logical_and(j >= 2, j < nj)
    pl.when(jnp.logical_and(steady, j % 2 == 0))(lambda: stage(0, True, True, True))
    pl.when(jnp.logical_and(steady, j % 2 == 1))(lambda: stage(1, True, True, True))
    pl.when(j == nj)(lambda: stage(nj % 2, False, True, True))
    pl.when(j == nj + 1)(lambda: stage((nj + 1) % 2, False, False, True))


def _conv_glu_tail_kernel(h_ref, hp_ref, hn_ref, wg_ref, wv_ref, cw_ref, cb_ref, buf_ref, o_ref,
                          ext_ref, gate0, gate1, val0, val1, *, seq, nj):
    row0 = (pl.program_id(0) + 1) * h_ref.shape[0]
    j = pl.program_id(1)
    gate_s, val_s = (gate0, gate1), (val0, val1)

    pl.when(j == 0)(lambda: _build_halo_tile(ext_ref, h_ref, hp_ref, hn_ref))

    def stage(p, do_matmul, do_epilogue):
        q = 1 - p
        width = o_ref.shape[1] // TAIL_SUBTILES
        for t in range(TAIL_SUBTILES):
            cols = pl.ds(t * width, width)
            acc = None
            for c in range(PIPELINE_CHUNKS):
                if do_matmul:
                    acc = _conv_glu_matmul_chunk(ext_ref, wg_ref.at[:, cols], wv_ref.at[:, cols], c, acc)
                if do_epilogue:
                    _conv_glu_epilogue_chunk(gate_s[q].at[:, cols], val_s[q].at[:, cols], cw_ref.at[:, cols],
                                             cb_ref.at[:, cols], o_ref.at[:, cols], c, row0, seq)
            if do_matmul:
                gate_s[p][:, cols], val_s[p][:, cols] = acc

    pl.when(j == 0)(lambda: stage(0, True, False))
    steady = jnp.logical_and(j >= 1, j < nj)
    pl.when(jnp.logical_and(steady, j % 2 == 0))(lambda: stage(0, True, True))
    pl.when(jnp.logical_and(steady, j % 2 == 1))(lambda: stage(1, True, True))
    pl.when(j == nj)(lambda: stage(nj % 2, False, True))


def conv_glu_up(h, w_up, l, conv_w, conv_b, seq, buf, bm=1024, bn_head=256, bn=256 * TAIL_SUBTILES):
    m, k = h.shape
    dff = w_up.shape[2] // 2
    assert buf.shape == (m, dff) and buf.dtype == BF16
    halo = BF16_SUBLANES
    rb = bm // halo
    last = m // halo - 1
    conv_b = conv_b.reshape(1, dff)
    assert bm == HEAD_ROWS
    prev_tile = lambda i: jnp.maximum(i * rb - 1, 0)
    next_tile = lambda i: jnp.minimum((i + 1) * rb, last)
    ext = pltpu.VMEM((bm + 2 * halo, k), BF16)

    nj = dff // bn_head
    assert nj >= 2
    w_tile = lambda j: jnp.minimum(j, nj - 1)
    o_tile = lambda j: jnp.clip(j - 2, 0, nj - 1)
    wb = pltpu.VMEM((k, bn_head), BF16)
    gate = pltpu.VMEM((bm + 2 * halo, bn_head), F32)
    val = pltpu.VMEM((bm, bn_head), F32)
    wb_out = jax.ShapeDtypeStruct((k, dff), BF16)
    buf, wgb, wvb = pl.pallas_call(
        functools.partial(_conv_glu_head_kernel, seq=seq, nj=nj),
        grid=(nj + 2,),
        in_specs=[_resident((bm, k), lambda j: (0, 0)),
                  _resident((halo, k), lambda j: (prev_tile(0), 0)),
                  _resident((halo, k), lambda j: (next_tile(0), 0)),
                  pl.BlockSpec((None, k, bn_head), lambda j: (l, 0, w_tile(j))),
                  pl.BlockSpec((None, k, bn_head), lambda j: (l, 0, w_tile(j) + nj)),
                  pl.BlockSpec((CONV_WIDTH, bn_head), lambda j: (0, o_tile(j))),
                  pl.BlockSpec((1, bn_head), lambda j: (0, o_tile(j))),
                  _ALIASED],
        out_specs=[pl.BlockSpec((bm, bn_head), lambda j: (0, o_tile(j))),
                   pl.BlockSpec((k, bn_head), lambda j: (0, w_tile(j))),
                   pl.BlockSpec((k, bn_head), lambda j: (0, w_tile(j)))],
        out_shape=[jax.ShapeDtypeStruct((m, dff), BF16), wb_out, wb_out],
        input_output_aliases={7: 0},
        scratch_shapes=[ext, wb, wb, wb, wb, gate, gate, val, val],
        compiler_params=_params(56, ("arbitrary",)),
        name="conv_glu_up_head",
    )(h, h, h, w_up, w_up, conv_w, conv_b, buf)

    nj = dff // bn
    w_tile = lambda j: jnp.minimum(j, nj - 1)
    o_tile = lambda j: jnp.clip(j - 1, 0, nj - 1)
    gate = pltpu.VMEM((bm + 2 * halo, bn), F32)
    val = pltpu.VMEM((bm, bn), F32)
    return pl.pallas_call(
        functools.partial(_conv_glu_tail_kernel, seq=seq, nj=nj),
        grid=(m // bm - 1, nj + 1),
        in_specs=[_resident((bm, k), lambda i, j: (i + 1, 0)),
                  _resident((halo, k), lambda i, j: (prev_tile(i + 1), 0)),
                  _resident((halo, k), lambda i, j: (next_tile(i + 1), 0)),
                  pl.BlockSpec((k, bn), lambda i, j: (0, w_tile(j))),
                  pl.BlockSpec((k, bn), lambda i, j: (0, w_tile(j))),
                  pl.BlockSpec((CONV_WIDTH, bn), lambda i, j: (0, o_tile(j))),
                  pl.BlockSpec((1, bn), lambda i, j: (0, o_tile(j))),
                  _ALIASED],
        out_specs=pl.BlockSpec((bm, bn), lambda i, j: (i + 1, o_tile(j))),
        out_shape=jax.ShapeDtypeStruct((m, dff), BF16),
        input_output_aliases={7: 0},
        scratch_shapes=[ext, gate, gate, val, val],
        compiler_params=_params(56, ("parallel", "arbitrary")),
        name="conv_glu_up",
    )(h, h, h, wgb, wvb, conv_w, conv_b, buf)


def kernel(x, norm1_gain, w_in, w_fourier_out, lambdas, subln_gain, rel_bias_table, w_attn_out,
           w_gate, b_gate, w_o, norm2_gain, w_up, conv_w, conv_b, w_down, final_norm_gain):
    batch, seq, d_model = x.shape
    depth = w_in.shape[0]
    n_heads = rel_bias_table.shape[1]
    d_attn = n_heads * HEAD_WIDTH
    d_in = w_in.shape[2]
    tq = 512

    cs, sn, chan = _dft_constants(seq, FOURIER_GROUP_DIM)
    col = np.ones((1, d_in), np.float32)
    col[:, D_FOURIER:D_FOURIER + d_attn] = ATTN_HEAD_DIM ** -0.5 * LOG2_E
    col_scale = jnp.asarray(col)
    wb = bias_window(rel_bias_table, seq, tq)

    x = x.reshape(batch * seq, d_model)
    m = batch * seq
    u = jnp.zeros((m, d_in), BF16)
    mixed = jnp.zeros((m, d_model), BF16)
    act = jnp.zeros((m, w_up.shape[2] // 2), BF16)
    for l in range(depth):
        lam_init = 0.8 - 0.6 * math.exp(-0.3 * l)
        h = rmsnorm(x, norm1_gain[l], BF16)
        u = in_proj(h, w_in, l, col_scale, u)
        t1, t2 = fourier_chan(u, chan)
        f = fourier_seq(cs, sn, t1, t2, seq)
        a = diff_attention(u, wb, lambdas[l], subln_gain[l], lam_init, seq, n_heads, tq)
        mixed = gated_mix(h, f, a, w_gate, w_fourier_out, w_attn_out, l, b_gate[l], mixed)
        x = proj_residual(mixed, w_o, l, x, head_rows=2 * HEAD_ROWS, bn_head=256, bm=1024, bn=1024)
        h2 = rmsnorm(x, norm2_gain[l], BF16)
        act = conv_glu_up(h2, w_up, l, conv_w[l], conv_b[l], seq, act)
        x = proj_residual(act, w_down, l, x, head_rows=HEAD_ROWS, bn_head=256, bm=1024, bn=256)
    out = rmsnorm(x, final_norm_gain, F32)
    return out.reshape(batch, seq, d_model)
```

```python
import functools
import math

import numpy as np
import jax
import jax.numpy as jnp
from jax import lax
from jax.experimental import pallas as pl
from jax.experimental.pallas import tpu as pltpu

F32 = jnp.float32
BF16 = jnp.bfloat16

EPS = 1e-6
N_FOURIER_GROUPS = 4
FOURIER_GROUP_DIM = 256
D_FOURIER = N_FOURIER_GROUPS * FOURIER_GROUP_DIM
ATTN_HEAD_DIM = 128
HEAD_WIDTH = 2 * ATTN_HEAD_DIM
N_REL_BUCKETS = 32
REL_MAX_DISTANCE = 128
LOG2_E = math.log2(math.e)
ATTN_KEY_CHUNK = 512
ATTN_FINISH_LEAD = 6
CONV_WIDTH = 3
BF16_SUBLANES = 16
F32_SUBLANES = 8
PIPELINE_CHUNKS = 16
MXU_DEPTH = 256
EPILOGUE_SPLIT = 2
TAIL_SUBTILES = 2

MIB = 1024 * 1024


def _params(vmem_mib, semantics):
    return pltpu.CompilerParams(dimension_semantics=semantics, vmem_limit_bytes=vmem_mib * MIB)


def _resident(shape, index_map):
    return pl.BlockSpec(shape, index_map, pipeline_mode=pl.Buffered(1))


def _rmsnorm_kernel(x_ref, g_ref, o_ref):
    x = x_ref[...]
    ms = jnp.mean(x * x, axis=-1, keepdims=True)
    o_ref[...] = (x * lax.rsqrt(ms + EPS) * g_ref[...]).astype(o_ref.dtype)


def rmsnorm(x, gain, out_dtype, bm=512):
    m, d = x.shape
    return pl.pallas_call(
        _rmsnorm_kernel,
        grid=(m // bm,),
        in_specs=[pl.BlockSpec((bm, d), lambda i: (i, 0)),
                  pl.BlockSpec((1, d), lambda i: (0, 0))],
        out_specs=pl.BlockSpec((bm, d), lambda i: (i, 0)),
        out_shape=jax.ShapeDtypeStruct((m, d), out_dtype),
        compiler_params=_params(40, ("parallel",)),
        name="rmsnorm",
    )(x, gain.reshape(1, d))


HEAD_ROWS = 1024
_ALIASED = pl.BlockSpec(memory_space=pl.ANY)


def _cast_dot(lhs_ref, w_ref, wb_ref):
    kc = MXU_DEPTH
    acc = None
    for c in range(w_ref.shape[0] // kc):
        ks = slice(c * kc, (c + 1) * kc)
        wb_ref[ks, :] = w_ref[ks, :].astype(BF16)
        part = jnp.dot(lhs_ref[:, ks], wb_ref[ks, :], preferred_element_type=F32)
        acc = part if c == 0 else acc + part
    return acc


def _in_proj_head_kernel(h_ref, w_ref, s_ref, buf_ref, o_ref, wb_ref):
    o_ref[...] = (_cast_dot(h_ref, w_ref, wb_ref) * s_ref[...]).astype(o_ref.dtype)


def _in_proj_tail_kernel(h_ref, wb_ref, s_ref, buf_ref, o_ref):
    acc = jnp.dot(h_ref[...], wb_ref[...], preferred_element_type=F32)
    o_ref[...] = (acc * s_ref[...]).astype(o_ref.dtype)


def in_proj(h, w, l, col_scale, buf, head_rows=2 * HEAD_ROWS, bm=1024, bn_head=512, bn=1024):
    m, k = h.shape
    n = w.shape[2]
    assert buf.shape == (m, n) and buf.dtype == BF16
    buf, wb = pl.pallas_call(
        _in_proj_head_kernel,
        grid=(n // bn_head,),
        in_specs=[_resident((head_rows, k), lambda j: (0, 0)),
                  pl.BlockSpec((None, k, bn_head), lambda j: (l, 0, j)),
                  pl.BlockSpec((1, bn_head), lambda j: (0, j)),
                  _ALIASED],
        out_specs=[pl.BlockSpec((head_rows, bn_head), lambda j: (0, j)),
                   pl.BlockSpec((k, bn_head), lambda j: (0, j))],
        out_shape=[jax.ShapeDtypeStruct((m, n), BF16), jax.ShapeDtypeStruct((k, n), BF16)],
        input_output_aliases={3: 0},
        compiler_params=_params(48, ("arbitrary",)),
        name="in_proj_head",
    )(h, w, col_scale, buf)
    skip = head_rows // bm
    return pl.pallas_call(
        _in_proj_tail_kernel,
        grid=(m // bm - skip, n // bn),
        in_specs=[pl.BlockSpec((bm, k), lambda i, j: (i + skip, 0)),
                  pl.BlockSpec((k, bn), lambda i, j: (0, j)),
                  pl.BlockSpec((1, bn), lambda i, j: (0, j)),
                  _ALIASED],
        out_specs=pl.BlockSpec((bm, bn), lambda i, j: (i + skip, j)),
        out_shape=jax.ShapeDtypeStruct((m, n), BF16),
        input_output_aliases={3: 0},
        compiler_params=_params(48, ("parallel", "arbitrary")),
        name="in_proj",
    )(h, wb, col_scale, buf)


def _dft_constants(seq, dim):
    def cos_sin(n):
        idx = np.arange(n, dtype=np.int64)
        ang = 2.0 * np.pi * ((idx[:, None] * idx[None, :]) % n).astype(np.float64) / n
        return np.cos(ang), np.sin(ang)
    cs, ss = cos_sin(seq)
    cc, sc = cos_sin(dim)
    chan = np.concatenate([cc, sc], axis=1)
    return (jnp.asarray(cs, BF16), jnp.asarray(-ss, BF16), jnp.asarray(chan, BF16))


def _fourier_chan_kernel(u_ref, c_ref, t1_ref, t2_ref):
    c = c_ref[...]
    gd = FOURIER_GROUP_DIM
    for g in range(N_FOURIER_GROUPS):
        t = jnp.dot(u_ref[:, g * gd:(g + 1) * gd], c, preferred_element_type=F32)
        t1_ref[:, g * gd:(g + 1) * gd] = t[:, :gd].astype(t1_ref.dtype)
        t2_ref[:, g * gd:(g + 1) * gd] = t[:, gd:].astype(t2_ref.dtype)


def fourier_chan(u, chan, bm=1024):
    m = u.shape[0]
    gd = FOURIER_GROUP_DIM
    out = jax.ShapeDtypeStruct((m, D_FOURIER), BF16)
    return pl.pallas_call(
        _fourier_chan_kernel,
        grid=(m // bm,),
        in_specs=[pl.BlockSpec((bm, D_FOURIER), lambda i: (i, 0)),
                  pl.BlockSpec((gd, 2 * gd), lambda i: (0, 0))],
        out_specs=[pl.BlockSpec((bm, D_FOURIER), lambda i: (i, 0)),
                   pl.BlockSpec((bm, D_FOURIER), lambda i: (i, 0))],
        out_shape=[out, out],
        compiler_params=_params(32, ("parallel",)),
        name="fourier_chan",
    )(u, chan)


def _fourier_seq_kernel(cs_ref, sn_ref, t1_ref, t2_ref, o_ref, *, scale):
    acc = jnp.dot(cs_ref[...], t1_ref[...], preferred_element_type=F32)
    acc += jnp.dot(sn_ref[...], t2_ref[...], preferred_element_type=F32)
    o_ref[...] = (acc * scale).astype(o_ref.dtype)


def fourier_seq(cs, sn, t1, t2, seq, bm=512):
    m, n = t1.shape
    nb = m // seq
    ni = seq // bm
    scale = 1.0 / math.sqrt(seq * FOURIER_GROUP_DIM)
    return pl.pallas_call(
        functools.partial(_fourier_seq_kernel, scale=scale),
        grid=(nb, ni),
        in_specs=[pl.BlockSpec((bm, seq), lambda b, i: (i, 0)),
                  pl.BlockSpec((bm, seq), lambda b, i: (i, 0)),
                  pl.BlockSpec((seq, n), lambda b, i: (b, 0)),
                  pl.BlockSpec((seq, n), lambda b, i: (b, 0))],
        out_specs=pl.BlockSpec((bm, n), lambda b, i: (b * ni + i, 0)),
        out_shape=jax.ShapeDtypeStruct((m, n), BF16),
        compiler_params=_params(40, ("parallel", "arbitrary")),
        name="fourier_seq",
    )(cs, sn, t1, t2)


def _bias_window_kernel(tab_ref, o_ref, *, tq, nq):
    h = pl.program_id(0)
    half = N_REL_BUCKETS // 2
    max_exact = half // 2
    width = o_ref.shape[2]
    lo = (nq - 1) * tq - REL_MAX_DISTANCE
    hi = nq * tq + REL_MAX_DISTANCE
    o_ref[0, :, :lo] = jnp.full((tq, lo), tab_ref[half - 1, h] * LOG2_E, F32)
    o_ref[0, :, hi:] = jnp.full((tq, width - hi), tab_ref[N_REL_BUCKETS - 1, h] * LOG2_E, F32)

    shape = (tq, hi - lo)
    i = lax.broadcasted_iota(jnp.int32, shape, 0)
    c = lax.broadcasted_iota(jnp.int32, shape, 1) + lo
    n = i + (nq - 1) * tq - c
    ret = jnp.where(n < 0, half, 0)
    n = jnp.abs(n)
    n2 = n * n
    large = jnp.full(shape, max_exact, jnp.int32)
    for j in range(1, half - max_exact):
        large += (n2 >= (max_exact * max_exact) * (2 ** j)).astype(jnp.int32)
    bucket = ret + jnp.where(n < max_exact, n, large)
    acc = jnp.zeros(shape, F32)
    for b in range(N_REL_BUCKETS):
        acc = jnp.where(bucket == b, tab_ref[b, h] * LOG2_E, acc)
    o_ref[0, :, lo:hi] = acc


def bias_window(rel_table, seq, tq):
    nq = seq // tq
    nh = rel_table.shape[1]
    width = 2 * seq - tq
    assert tq >= REL_MAX_DISTANCE and nq >= 2
    return pl.pallas_call(
        functools.partial(_bias_window_kernel, tq=tq, nq=nq),
        grid=(nh,),
        in_specs=[pl.BlockSpec(memory_space=pltpu.SMEM)],
        out_specs=pl.BlockSpec((1, tq, width), lambda h: (h, 0, 0)),
        out_shape=jax.ShapeDtypeStruct((nh, tq, width), F32),
        compiler_params=_params(40, ("parallel",)),
        name="bias_window",
    )(rel_table)


def _lane_tile_reduce(x, op):
    lanes = 128
    out = x[:, 0:lanes]
    for t in range(1, x.shape[1] // lanes):
        out = op(out, x[:, t * lanes:(t + 1) * lanes])
    return out


def _attn_kernel(q_ref, k_ref, v_ref, wb_ref, lam_ref, sg_ref, o_ref, s0_ref, s1_ref, mx0_ref, mx1_ref,
                 *, lam_init, tq, nq, seq):
    i = pl.program_id(2)
    d = ATTN_HEAD_DIM
    ck = ATTN_KEY_CHUNK
    nc = seq // ck
    s_scr = (s0_ref, s1_ref)
    mx_scr = (mx0_ref, mx1_ref)
    chunks = [(m, c) for m in range(2) for c in range(nc)]

    def logits_chunk(p, m, c, run_max):
        off = pl.multiple_of((nq - 1 - i) * tq + c * ck, tq)
        s = lax.dot_general(q_ref[:, m * d:(m + 1) * d], k_ref[c * ck:(c + 1) * ck, m * d:(m + 1) * d],
                            (((1,), (1,)), ((), ())), preferred_element_type=F32)
        s = s + wb_ref[0, :, pl.ds(off, ck)]
        s_scr[p][m, :, c * ck:(c + 1) * ck] = s
        part = _lane_tile_reduce(s, jnp.maximum)
        run_max = part if c == 0 else jnp.maximum(run_max, part)
        if c == nc - 1:
            mx_scr[p][m] = jnp.broadcast_to(jnp.max(run_max, axis=-1, keepdims=True), run_max.shape)
        return run_max

    def probs_chunk(p, m, c, state):
        row_max, row_sum, out = state
        if c == 0:
            row_max = jnp.concatenate([mx_scr[p][m]] * (ck // 128), axis=1)
        e = jnp.exp2(s_scr[p][m, :, c * ck:(c + 1) * ck] - row_max)
        part = _lane_tile_reduce(e, jnp.add)
        pv = jnp.dot(e.astype(BF16), v_ref[c * ck:(c + 1) * ck, :], preferred_element_type=F32)
        return (row_max, part if c == 0 else row_sum + part, pv if c == 0 else out + pv)

    def finish(results):
        (_, sum0, out0), (_, sum1, out1) = results
        lam_p = lam_ref[...]
        lam = (jnp.exp(jnp.sum(lam_p[0:1] * lam_p[1:2], axis=-1, keepdims=True))
               - jnp.exp(jnp.sum(lam_p[2:3] * lam_p[3:4], axis=-1, keepdims=True)) + lam_init)
        l0 = jnp.sum(sum0, axis=-1, keepdims=True)
        l1 = jnp.sum(sum1, axis=-1, keepdims=True)
        o = out0 * (1.0 / l0) - out1 * (lam / l1)
        ms = jnp.mean(o * o, axis=-1, keepdims=True)
        o = o * lax.rsqrt(ms + EPS) * sg_ref[...] * (1.0 - lam_init)
        o_ref[...] = o.astype(o_ref.dtype)

    def stage(p, do_logits, do_probs):
        lead = ATTN_FINISH_LEAD if do_logits and do_probs else 0
        run_max = None
        state = (None, None, None)
        results = []
        for idx in range(len(chunks) + lead):
            if do_probs and idx < len(chunks):
                m, c = chunks[idx]
                state = probs_chunk(1 - p, m, c, state)
                if c == nc - 1:
                    results.append(state)
                if idx == len(chunks) - 1:
                    finish(results)
            if do_logits and idx >= lead:
                m, c = chunks[idx - lead]
                run_max = logits_chunk(p, m, c, run_max)

    pl.when(i == 0)(lambda: stage(0, True, False))
    steady = jnp.logical_and(i >= 1, i < nq)
    pl.when(jnp.logical_and(steady, i % 2 == 0))(lambda: stage(0, True, True))
    pl.when(jnp.logical_and(steady, i % 2 == 1))(lambda: stage(1, True, True))
    pl.when(i == nq)(lambda: stage(nq % 2, False, True))


def diff_attention(u, wb, lambdas, sub_gain, lam_init, seq, n_heads, tq):
    m = u.shape[0]
    nb = m // seq
    nq = seq // tq
    hw = HEAD_WIDTH
    q_blk = D_FOURIER // hw
    k_blk = q_blk + n_heads
    v_blk = k_blk + n_heads
    assert ATTN_KEY_CHUNK % tq == 0 and seq % ATTN_KEY_CHUNK == 0
    logits = pltpu.VMEM((2, tq, seq), F32)
    maxima = pltpu.VMEM((2, tq, 128), F32)
    return pl.pallas_call(
        functools.partial(_attn_kernel, lam_init=lam_init, tq=tq, nq=nq, seq=seq),
        grid=(nb, n_heads, nq + 1),
        in_specs=[pl.BlockSpec((tq, hw), lambda b, h, i: (b * nq + jnp.minimum(i, nq - 1), q_blk + h)),
                  pl.BlockSpec((seq, hw), lambda b, h, i: (b, k_blk + h)),
                  pl.BlockSpec((seq, hw), lambda b, h, i: (b, v_blk + h)),
                  pl.BlockSpec((1, tq, 2 * seq - tq), lambda b, h, i: (h, 0, 0)),
                  pl.BlockSpec((4, ATTN_HEAD_DIM), lambda b, h, i: (0, 0)),
                  pl.BlockSpec((1, hw), lambda b, h, i: (0, 0))],
        out_specs=pl.BlockSpec((tq, hw), lambda b, h, i: (b * nq + jnp.maximum(i - 1, 0), h)),
        out_shape=jax.ShapeDtypeStruct((m, n_heads * hw), BF16),
        scratch_shapes=[logits, logits, maxima, maxima],
        compiler_params=_params(48, ("parallel", "parallel", "arbitrary")),
        name="diff_attention",
    )(u, u, u, wb, lambdas, sub_gain.reshape(1, hw))


def _mix_head_kernel(h_ref, f_ref, a_ref, wg0_ref, wg1_ref, wf_ref, wa_ref, b0_ref, b1_ref, buf_ref,
                     o_ref, wg0b_ref, wg1b_ref, wfb_ref, wab_ref):
    g0 = jax.nn.sigmoid(_cast_dot(h_ref, wg0_ref, wg0b_ref) + b0_ref[...])
    t0 = g0 * _cast_dot(f_ref, wf_ref, wfb_ref)
    g1 = jax.nn.sigmoid(_cast_dot(h_ref, wg1_ref, wg1b_ref) + b1_ref[...])
    o_ref[...] = (t0 + g1 * _cast_dot(a_ref, wa_ref, wab_ref)).astype(o_ref.dtype)


def _mix_tail_kernel(h_ref, f_ref, a_ref, wg0_ref, wg1_ref, wf_ref, wa_ref, b0_ref, b1_ref, buf_ref, o_ref):
    h = h_ref[...]
    dot = functools.partial(jnp.dot, preferred_element_type=F32)
    g0 = jax.nn.sigmoid(dot(h, wg0_ref[...]) + b0_ref[...])
    t0 = g0 * dot(f_ref[...], wf_ref[...])
    g1 = jax.nn.sigmoid(dot(h, wg1_ref[...]) + b1_ref[...])
    o_ref[...] = (t0 + g1 * dot(a_ref[...], wa_ref[...])).astype(o_ref.dtype)


def gated_mix(h, f, a, w_gate, w_f, w_a, l, b_gate, buf, bm=1024, bn=256):
    m, d = h.shape
    df, da = f.shape[1], a.shape[1]
    nj = d // bn
    b_gate = b_gate.reshape(1, 2 * d)
    assert buf.shape == (m, d) and buf.dtype == BF16
    buf, wg0b, wg1b, wfb, wab = pl.pallas_call(
        _mix_head_kernel,
        grid=(nj,),
        in_specs=[_resident((HEAD_ROWS, d), lambda j: (0, 0)),
                  _resident((HEAD_ROWS, df), lambda j: (0, 0)),
                  _resident((HEAD_ROWS, da), lambda j: (0, 0)),
                  pl.BlockSpec((None, d, bn), lambda j: (l, 0, j)),
                  pl.BlockSpec((None, d, bn), lambda j: (l, 0, j + nj)),
                  pl.BlockSpec((None, df, bn), lambda j: (l, 0, j)),
                  pl.BlockSpec((None, da, bn), lambda j: (l, 0, j)),
                  pl.BlockSpec((1, bn), lambda j: (0, j)),
                  pl.BlockSpec((1, bn), lambda j: (0, j + nj)),
                  _ALIASED],
        out_specs=[pl.BlockSpec((HEAD_ROWS, bn), lambda j: (0, j)),
                   pl.BlockSpec((d, bn), lambda j: (0, j)),
                   pl.BlockSpec((d, bn), lambda j: (0, j)),
                   pl.BlockSpec((df, bn), lambda j: (0, j)),
                   pl.BlockSpec((da, bn), lambda j: (0, j))],
        out_shape=[jax.ShapeDtypeStruct((m, d), BF16), jax.ShapeDtypeStruct((d, d), BF16),
                   jax.ShapeDtypeStruct((d, d), BF16), jax.ShapeDtypeStruct((df, d), BF16),
                   jax.ShapeDtypeStruct((da, d), BF16)],
        input_output_aliases={9: 0},
        compiler_params=_params(56, ("arbitrary",)),
        name="gated_mix_head",
    )(h, f, a, w_gate, w_gate, w_f, w_a, b_gate, b_gate, buf)
    assert bm == HEAD_ROWS
    wcol = lambda i, j: (0, j)
    return pl.pallas_call(
        _mix_tail_kernel,
        grid=(m // bm - 1, nj),
        in_specs=[pl.BlockSpec((bm, d), lambda i, j: (i + 1, 0)),
                  pl.BlockSpec((bm, df), lambda i, j: (i + 1, 0)),
                  pl.BlockSpec((bm, da), lambda i, j: (i + 1, 0)),
                  pl.BlockSpec((d, bn), wcol),
                  pl.BlockSpec((d, bn), wcol),
                  pl.BlockSpec((df, bn), wcol),
                  pl.BlockSpec((da, bn), wcol),
                  pl.BlockSpec((1, bn), lambda i, j: (0, j)),
                  pl.BlockSpec((1, bn), lambda i, j: (0, j + nj)),
                  _ALIASED],
        out_specs=pl.BlockSpec((bm, bn), lambda i, j: (i + 1, j)),
        out_shape=jax.ShapeDtypeStruct((m, d), BF16),
        input_output_aliases={9: 0},
        compiler_params=_params(56, ("parallel", "arbitrary")),
        name="gated_mix",
    )(h, f, a, wg0b, wg1b, wfb, wab, b_gate, b_gate, buf)


def _proj_residual_head_kernel(a_ref, w_ref, r_ref, o_ref, wb_ref):
    o_ref[...] = r_ref[...] + _cast_dot(a_ref, w_ref, wb_ref)


def _proj_residual_tail_kernel(a_ref, wb_ref, r_ref, o_ref):
    o_ref[...] = r_ref[...] + jnp.dot(a_ref[...], wb_ref[...], preferred_element_type=F32)


def proj_residual(a, w, l, res, head_rows, bn_head, bm, bn):
    m, k = a.shape
    n = w.shape[2]
    res, wb = pl.pallas_call(
        _proj_residual_head_kernel,
        grid=(n // bn_head,),
        in_specs=[_resident((head_rows, k), lambda j: (0, 0)),
                  pl.BlockSpec((None, k, bn_head), lambda j: (l, 0, j)),
                  pl.BlockSpec((head_rows, bn_head), lambda j: (0, j))],
        out_specs=[pl.BlockSpec((head_rows, bn_head), lambda j: (0, j)),
                   pl.BlockSpec((k, bn_head), lambda j: (0, j))],
        out_shape=[jax.ShapeDtypeStruct((m, n), F32), jax.ShapeDtypeStruct((k, n), BF16)],
        input_output_aliases={2: 0},
        compiler_params=_params(56, ("arbitrary",)),
        name="proj_residual_head",
    )(a, w, res)
    skip = head_rows // bm
    return pl.pallas_call(
        _proj_residual_tail_kernel,
        grid=(m // bm - skip, n // bn),
        in_specs=[pl.BlockSpec((bm, k), lambda i, j: (i + skip, 0)),
                  pl.BlockSpec((k, bn), lambda i, j: (0, j)),
                  pl.BlockSpec((bm, bn), lambda i, j: (i + skip, j))],
        out_specs=pl.BlockSpec((bm, bn), lambda i, j: (i + skip, j)),
        out_shape=jax.ShapeDtypeStruct((m, n), F32),
        input_output_aliases={2: 0},
        compiler_params=_params(56, ("parallel", "arbitrary")),
        name="proj_residual",
    )(a, wb, res)


def _gelu_exact(x):
    return 0.5 * x * (1.0 + lax.erf(x * math.sqrt(0.5)))


def _build_halo_tile(ext_ref, h_ref, hp_ref, hn_ref):
    bm = h_ref.shape[0]
    halo = BF16_SUBLANES
    ext_ref[0:halo, :] = hp_ref[...]
    ext_ref[halo:halo + bm, :] = h_ref[...]
    ext_ref[halo + bm:, :] = hn_ref[...]


def _conv_glu_matmul_chunk(ext_ref, wg_ref, wv_ref, c, acc):
    halo = BF16_SUBLANES
    bm = ext_ref.shape[0] - 2 * halo
    kc = ext_ref.shape[1] // PIPELINE_CHUNKS
    ks = slice(c * kc, (c + 1) * kc)
    gd = jnp.dot(ext_ref[:, ks], wg_ref[ks, :], preferred_element_type=F32)
    vd = jnp.dot(ext_ref[halo:halo + bm, ks], wv_ref[ks, :], preferred_element_type=F32)
    return (gd, vd) if c == 0 else (acc[0] + gd, acc[1] + vd)


def _conv_glu_epilogue_chunk(gate_ref, val_ref, cw_ref, cb_ref, o_ref, c, row0, seq):
    halo = BF16_SUBLANES
    pad = F32_SUBLANES
    bm = val_ref.shape[0]
    rc = bm // PIPELINE_CHUNKS // EPILOGUE_SPLIT
    cw = cw_ref[...]
    tiles_align = seq % bm == 0
    for part in range(EPILOGUE_SPLIT):
        first = (c * EPILOGUE_SPLIT + part) * rc
        r0 = halo + first
        g = gate_ref[r0 - pad:r0 + rc + pad, :]
        g_prev = pltpu.roll(g, 1, axis=0)[pad:pad + rc]
        g_next = pltpu.roll(g, rc + 2 * pad - 1, axis=0)[pad:pad + rc]
        pos = (row0 + first + lax.broadcasted_iota(jnp.int32, (rc, 1), 0)) % seq
        if not tiles_align or first == 0:
            g_prev = jnp.where(pos == 0, 0.0, g_prev)
        if not tiles_align or first + rc == bm:
            g_next = jnp.where(pos == seq - 1, 0.0, g_next)
        conv = g_prev * cw[0:1] + g[pad:pad + rc] * cw[1:2] + g_next * cw[2:3] + cb_ref[...]
        rs = slice(first, first + rc)
        o_ref[rs, :] = (_gelu_exact(conv) * val_ref[rs, :]).astype(o_ref.dtype)


def _conv_glu_head_kernel(h_ref, hp_ref, hn_ref, wg_ref, wv_ref, cw_ref, cb_ref, buf_ref,
                          o_ref, wgo_ref, wvo_ref, ext_ref, wgb0, wgb1, wvb0, wvb1, gate0, gate1, val0, val1, *, seq, nj):
    j = pl.program_id(0)
    kc = h_ref.shape[1] // PIPELINE_CHUNKS
    wgb, wvb, gate_s, val_s = (wgb0, wgb1), (wvb0, wvb1), (gate0, gate1), (val0, val1)

    pl.when(j == 0)(lambda: _build_halo_tile(ext_ref, h_ref, hp_ref, hn_ref))

    def stage(p, do_cast, do_matmul, do_epilogue):
        q = 1 - p
        acc = None
        for c in range(PIPELINE_CHUNKS):
            if do_matmul:
                acc = _conv_glu_matmul_chunk(ext_ref, wgb[q], wvb[q], c, acc)
            if do_cast:
                ks = slice(c * kc, (c + 1) * kc)
                wg = wg_ref[ks, :].astype(BF16)
                wv = wv_ref[ks, :].astype(BF16)
                wgb[p][ks, :] = wg
                wvb[p][ks, :] = wv
                wgo_ref[ks, :] = wg
                wvo_ref[ks, :] = wv
            if do_epilogue:
                _conv_glu_epilogue_chunk(gate_s[p], val_s[p], cw_ref, cb_ref, o_ref, c, 0, seq)
        if do_matmul:
            gate_s[q][...], val_s[q][...] = acc

    pl.when(j == 0)(lambda: stage(0, True, False, False))
    pl.when(j == 1)(lambda: stage(1, True, True, False))
    steady = jnp.logical_and(j >= 2, j < nj)
    pl.when(jnp.logical_and(steady, j % 2 == 0))(lambda: stage(0, True, True, True))
    pl.when(jnp.logical_and(steady, j % 2 == 1))(lambda: stage(1, True, True, True))
    pl.when(j == nj)(lambda: stage(nj % 2, False, True, True))
    pl.when(j == nj + 1)(lambda: stage((nj + 1) % 2, False, False, True))


def _conv_glu_tail_kernel(h_ref, hp_ref, hn_ref, wg_ref, wv_ref, cw_ref, cb_ref, buf_ref, o_ref,
                          ext_ref, gate0, gate1, val0, val1, *, seq, nj):
    row0 = (pl.program_id(0) + 1) * h_ref.shape[0]
    j = pl.program_id(1)
    gate_s, val_s = (gate0, gate1), (val0, val1)

    pl.when(j == 0)(lambda: _build_halo_tile(ext_ref, h_ref, hp_ref, hn_ref))

    def stage(p, do_matmul, do_epilogue):
        q = 1 - p
        width = o_ref.shape[1] // TAIL_SUBTILES
        for t in range(TAIL_SUBTILES):
            cols = pl.ds(t * width, width)
            acc = None
            for c in range(PIPELINE_CHUNKS):
                if do_matmul:
                    acc = _conv_glu_matmul_chunk(ext_ref, wg_ref.at[:, cols], wv_ref.at[:, cols], c, acc)
                if do_epilogue:
                    _conv_glu_epilogue_chunk(gate_s[q].at[:, cols], val_s[q].at[:, cols], cw_ref.at[:, cols],
                                             cb_ref.at[:, cols], o_ref.at[:, cols], c, row0, seq)
            if do_matmul:
                gate_s[p][:, cols], val_s[p][:, cols] = acc

    pl.when(j == 0)(lambda: stage(0, True, False))
    steady = jnp.logical_and(j >= 1, j < nj)
    pl.when(jnp.logical_and(steady, j % 2 == 0))(lambda: stage(0, True, True))
    pl.when(jnp.logical_and(steady, j % 2 == 1))(lambda: stage(1, True, True))
    pl.when(j == nj)(lambda: stage(nj % 2, False, True))


def conv_glu_up(h, w_up, l, conv_w, conv_b, seq, buf, bm=1024, bn_head=256, bn=256 * TAIL_SUBTILES):
    m, k = h.shape
    dff = w_up.shape[2] // 2
    assert buf.shape == (m, dff) and buf.dtype == BF16
    halo = BF16_SUBLANES
    rb = bm // halo
    last = m // halo - 1
    conv_b = conv_b.reshape(1, dff)
    assert bm == HEAD_ROWS
    prev_tile = lambda i: jnp.maximum(i * rb - 1, 0)
    next_tile = lambda i: jnp.minimum((i + 1) * rb, last)
    ext = pltpu.VMEM((bm + 2 * halo, k), BF16)

    nj = dff // bn_head
    assert nj >= 2
    w_tile = lambda j: jnp.minimum(j, nj - 1)
    o_tile = lambda j: jnp.clip(j - 2, 0, nj - 1)
    wb = pltpu.VMEM((k, bn_head), BF16)
    gate = pltpu.VMEM((bm + 2 * halo, bn_head), F32)
    val = pltpu.VMEM((bm, bn_head), F32)
    wb_out = jax.ShapeDtypeStruct((k, dff), BF16)
    buf, wgb, wvb = pl.pallas_call(
        functools.partial(_conv_glu_head_kernel, seq=seq, nj=nj),
        grid=(nj + 2,),
        in_specs=[_resident((bm, k), lambda j: (0, 0)),
                  _resident((halo, k), lambda j: (prev_tile(0), 0)),
                  _resident((halo, k), lambda j: (next_tile(0), 0)),
                  pl.BlockSpec((None, k, bn_head), lambda j: (l, 0, w_tile(j))),
                  pl.BlockSpec((None, k, bn_head), lambda j: (l, 0, w_tile(j) + nj)),
                  pl.BlockSpec((CONV_WIDTH, bn_head), lambda j: (0, o_tile(j))),
                  pl.BlockSpec((1, bn_head), lambda j: (0, o_tile(j))),
                  _ALIASED],
        out_specs=[pl.BlockSpec((bm, bn_head), lambda j: (0, o_tile(j))),
                   pl.BlockSpec((k, bn_head), lambda j: (0, w_tile(j))),
                   pl.BlockSpec((k, bn_head), lambda j: (0, w_tile(j)))],
        out_shape=[jax.ShapeDtypeStruct((m, dff), BF16), wb_out, wb_out],
        input_output_aliases={7: 0},
        scratch_shapes=[ext, wb, wb, wb, wb, gate, gate, val, val],
        compiler_params=_params(56, ("arbitrary",)),
        name="conv_glu_up_head",
    )(h, h, h, w_up, w_up, conv_w, conv_b, buf)

    nj = dff // bn
    w_tile = lambda j: jnp.minimum(j, nj - 1)
    o_tile = lambda j: jnp.clip(j - 1, 0, nj - 1)
    gate = pltpu.VMEM((bm + 2 * halo, bn), F32)
    val = pltpu.VMEM((bm, bn), F32)
    return pl.pallas_call(
        functools.partial(_conv_glu_tail_kernel, seq=seq, nj=nj),
        grid=(m // bm - 1, nj + 1),
        in_specs=[_resident((bm, k), lambda i, j: (i + 1, 0)),
                  _resident((halo, k), lambda i, j: (prev_tile(i + 1), 0)),
                  _resident((halo, k), lambda i, j: (next_tile(i + 1), 0)),
                  pl.BlockSpec((k, bn), lambda i, j: (0, w_tile(j))),
                  pl.BlockSpec((k, bn), lambda i, j: (0, w_tile(j))),
                  pl.BlockSpec((CONV_WIDTH, bn), lambda i, j: (0, o_tile(j))),
                  pl.BlockSpec((1, bn), lambda i, j: (0, o_tile(j))),
                  _ALIASED],
        out_specs=pl.BlockSpec((bm, bn), lambda i, j: (i + 1, o_tile(j))),
        out_shape=jax.ShapeDtypeStruct((m, dff), BF16),
        input_output_aliases={7: 0},
        scratch_shapes=[ext, gate, gate, val, val],
        compiler_params=_params(56, ("parallel", "arbitrary")),
        name="conv_glu_up",
    )(h, h, h, wgb, wvb, conv_w, conv_b, buf)


def kernel(x, norm1_gain, w_in, w_fourier_out, lambdas, subln_gain, rel_bias_table, w_attn_out,
           w_gate, b_gate, w_o, norm2_gain, w_up, conv_w, conv_b, w_down, final_norm_gain):
    batch, seq, d_model = x.shape
    depth = w_in.shape[0]
    n_heads = rel_bias_table.shape[1]
    d_attn = n_heads * HEAD_WIDTH
    d_in = w_in.shape[2]
    tq = 512

    cs, sn, chan = _dft_constants(seq, FOURIER_GROUP_DIM)
    col = np.ones((1, d_in), np.float32)
    col[:, D_FOURIER:D_FOURIER + d_attn] = ATTN_HEAD_DIM ** -0.5 * LOG2_E
    col_scale = jnp.asarray(col)
    wb = bias_window(rel_bias_table, seq, tq)

    x = x.reshape(batch * seq, d_model)
    m = batch * seq
    u = jnp.zeros((m, d_in), BF16)
    mixed = jnp.zeros((m, d_model), BF16)
    act = jnp.zeros((m, w_up.shape[2] // 2), BF16)
    for l in range(depth):
        lam_init = 0.8 - 0.6 * math.exp(-0.3 * l)
        h = rmsnorm(x, norm1_gain[l], BF16)
        u = in_proj(h, w_in, l, col_scale, u)
        t1, t2 = fourier_chan(u, chan)
        f = fourier_seq(cs, sn, t1, t2, seq)
        a = diff_attention(u, wb, lambdas[l], subln_gain[l], lam_init, seq, n_heads, tq)
        mixed = gated_mix(h, f, a, w_gate, w_fourier_out, w_attn_out, l, b_gate[l], mixed)
        x = proj_residual(mixed, w_o, l, x, head_rows=2 * HEAD_ROWS, bn_head=256, bm=1024, bn=1024)
        h2 = rmsnorm(x, norm2_gain[l], BF16)
        act = conv_glu_up(h2, w_up, l, conv_w[l], conv_b[l], seq, act)
        x = proj_residual(act, w_down, l, x, head_rows=HEAD_ROWS, bn_head=256, bm=1024, bn=256)
    out = rmsnorm(x, final_norm_gain, F32)
    return out.reshape(batch, seq, d_model)
```

```python
import functools
import math

import numpy as np
import jax
import jax.numpy as jnp
from jax import lax
from jax.experimental import pallas as pl
from jax.experimental.pallas import tpu as pltpu

F32 = jnp.float32
BF16 = jnp.bfloat16

EPS = 1e-6
N_FOURIER_GROUPS = 4
FOURIER_GROUP_DIM = 256
D_FOURIER = N_FOURIER_GROUPS * FOURIER_GROUP_DIM
ATTN_HEAD_DIM = 128
HEAD_WIDTH = 2 * ATTN_HEAD_DIM
N_REL_BUCKETS = 32
REL_MAX_DISTANCE = 128
LOG2_E = math.log2(math.e)
ATTN_KEY_CHUNK = 512
ATTN_FINISH_LEAD = 6
CONV_WIDTH = 3
LANES = 128
BF16_SUBLANES = 16
F32_SUBLANES = 8
PIPELINE_CHUNKS = 16
MXU_DEPTH = 256
EPILOGUE_SPLIT = 2
TAIL_SUBTILES = 2

MIB = 1024 * 1024


def _params(vmem_mib, semantics):
    return pltpu.CompilerParams(dimension_semantics=semantics, vmem_limit_bytes=vmem_mib * MIB)


def _resident(shape, index_map):
    return pl.BlockSpec(shape, index_map, pipeline_mode=pl.Buffered(1))


def _rmsnorm_kernel(x_ref, g_ref, o_ref):
    x = x_ref[...]
    ms = jnp.mean(x * x, axis=-1, keepdims=True)
    o_ref[...] = (x * lax.rsqrt(ms + EPS) * g_ref[...]).astype(o_ref.dtype)


def rmsnorm(x, gain, out_dtype, bm=512):
    m, d = x.shape
    return pl.pallas_call(
        _rmsnorm_kernel,
        grid=(m // bm,),
        in_specs=[pl.BlockSpec((bm, d), lambda i: (i, 0)),
                  pl.BlockSpec((1, d), lambda i: (0, 0))],
        out_specs=pl.BlockSpec((bm, d), lambda i: (i, 0)),
        out_shape=jax.ShapeDtypeStruct((m, d), out_dtype),
        compiler_params=_params(40, ("parallel",)),
        name="rmsnorm",
    )(x, gain.reshape(1, d))


HEAD_ROWS = 1024
_ALIASED = pl.BlockSpec(memory_space=pl.ANY)


def _cast_dot(lhs_ref, w_ref, wb_ref):
    kc = MXU_DEPTH
    acc = None
    for c in range(w_ref.shape[0] // kc):
        ks = slice(c * kc, (c + 1) * kc)
        wb_ref[ks, :] = w_ref[ks, :].astype(BF16)
        part = jnp.dot(lhs_ref[:, ks], wb_ref[ks, :], preferred_element_type=F32)
        acc = part if c == 0 else acc + part
    return acc


def _in_proj_head_kernel(h_ref, w_ref, s_ref, buf_ref, o_ref, wb_ref):
    o_ref[...] = (_cast_dot(h_ref, w_ref, wb_ref) * s_ref[...]).astype(o_ref.dtype)


def _in_proj_tail_kernel(h_ref, wb_ref, s_ref, buf_ref, o_ref):
    acc = jnp.dot(h_ref[...], wb_ref[...], preferred_element_type=F32)
    o_ref[...] = (acc * s_ref[...]).astype(o_ref.dtype)


def in_proj(h, w, l, col_scale, buf, head_rows=2 * HEAD_ROWS, bm=1024, bn_head=512, bn=1024):
    m, k = h.shape
    n = w.shape[2]
    assert buf.shape == (m, n) and buf.dtype == BF16
    buf, wb = pl.pallas_call(
        _in_proj_head_kernel,
        grid=(n // bn_head,),
        in_specs=[_resident((head_rows, k), lambda j: (0, 0)),
                  pl.BlockSpec((None, k, bn_head), lambda j: (l, 0, j)),
                  pl.BlockSpec((1, bn_head), lambda j: (0, j)),
                  _ALIASED],
        out_specs=[pl.BlockSpec((head_rows, bn_head), lambda j: (0, j)),
                   pl.BlockSpec((k, bn_head), lambda j: (0, j))],
        out_shape=[jax.ShapeDtypeStruct((m, n), BF16), jax.ShapeDtypeStruct((k, n), BF16)],
        input_output_aliases={3: 0},
        compiler_params=_params(48, ("arbitrary",)),
        name="in_proj_head",
    )(h, w, col_scale, buf)
    skip = head_rows // bm
    return pl.pallas_call(
        _in_proj_tail_kernel,
        grid=(m // bm - skip, n // bn),
        in_specs=[pl.BlockSpec((bm, k), lambda i, j: (i + skip, 0)),
                  pl.BlockSpec((k, bn), lambda i, j: (0, j)),
                  pl.BlockSpec((1, bn), lambda i, j: (0, j)),
                  _ALIASED],
        out_specs=pl.BlockSpec((bm, bn), lambda i, j: (i + skip, j)),
        out_shape=jax.ShapeDtypeStruct((m, n), BF16),
        input_output_aliases={3: 0},
        compiler_params=_params(48, ("parallel", "arbitrary")),
        name="in_proj",
    )(h, wb, col_scale, buf)


def _dft_constants(seq, dim):
    def cos_sin(n):
        idx = np.arange(n, dtype=np.int64)
        ang = 2.0 * np.pi * ((idx[:, None] * idx[None, :]) % n).astype(np.float64) / n
        return np.cos(ang), np.sin(ang)
    cs, ss = cos_sin(seq)
    cc, sc = cos_sin(dim)
    chan = np.concatenate([cc, sc], axis=1)
    return (jnp.asarray(cs, BF16), jnp.asarray(-ss, BF16), jnp.asarray(chan, BF16))


def _fourier_chan_kernel(u_ref, c_ref, t1_ref, t2_ref):
    c = c_ref[...]
    gd = FOURIER_GROUP_DIM
    for g in range(N_FOURIER_GROUPS):
        t = jnp.dot(u_ref[:, g * gd:(g + 1) * gd], c, preferred_element_type=F32)
        t1_ref[:, g * gd:(g + 1) * gd] = t[:, :gd].astype(t1_ref.dtype)
        t2_ref[:, g * gd:(g + 1) * gd] = t[:, gd:].astype(t2_ref.dtype)


def fourier_chan(u, chan, bm=1024):
    m = u.shape[0]
    gd = FOURIER_GROUP_DIM
    out = jax.ShapeDtypeStruct((m, D_FOURIER), BF16)
    return pl.pallas_call(
        _fourier_chan_kernel,
        grid=(m // bm,),
        in_specs=[pl.BlockSpec((bm, D_FOURIER), lambda i: (i, 0)),
                  pl.BlockSpec((gd, 2 * gd), lambda i: (0, 0))],
        out_specs=[pl.BlockSpec((bm, D_FOURIER), lambda i: (i, 0)),
                   pl.BlockSpec((bm, D_FOURIER), lambda i: (i, 0))],
        out_shape=[out, out],
        compiler_params=_params(32, ("parallel",)),
        name="fourier_chan",
    )(u, chan)


def _fourier_seq_kernel(cs_ref, sn_ref, t1_ref, t2_ref, o_ref, *, scale):
    acc = jnp.dot(cs_ref[...], t1_ref[...], preferred_element_type=F32)
    acc += jnp.dot(sn_ref[...], t2_ref[...], preferred_element_type=F32)
    o_ref[...] = (acc * scale).astype(o_ref.dtype)


def fourier_seq(cs, sn, t1, t2, seq, bm=512):
    m, n = t1.shape
    nb = m // seq
    ni = seq // bm
    scale = 1.0 / math.sqrt(seq * FOURIER_GROUP_DIM)
    return pl.pallas_call(
        functools.partial(_fourier_seq_kernel, scale=scale),
        grid=(nb, ni),
        in_specs=[pl.BlockSpec((bm, seq), lambda b, i: (i, 0)),
                  pl.BlockSpec((bm, seq), lambda b, i: (i, 0)),
                  pl.BlockSpec((seq, n), lambda b, i: (b, 0)),
                  pl.BlockSpec((seq, n), lambda b, i: (b, 0))],
        out_specs=pl.BlockSpec((bm, n), lambda b, i: (b * ni + i, 0)),
        out_shape=jax.ShapeDtypeStruct((m, n), BF16),
        compiler_params=_params(40, ("parallel", "arbitrary")),
        name="fourier_seq",
    )(cs, sn, t1, t2)


def _bias_window_kernel(tab_ref, o_ref, *, tq, nq):
    h = pl.program_id(0)
    half = N_REL_BUCKETS // 2
    max_exact = half // 2
    width = o_ref.shape[2]
    lo = (nq - 1) * tq - REL_MAX_DISTANCE
    hi = nq * tq + REL_MAX_DISTANCE
    o_ref[0, :, :lo] = jnp.full((tq, lo), tab_ref[half - 1, h] * LOG2_E, F32)
    o_ref[0, :, hi:] = jnp.full((tq, width - hi), tab_ref[N_REL_BUCKETS - 1, h] * LOG2_E, F32)

    shape = (tq, hi - lo)
    i = lax.broadcasted_iota(jnp.int32, shape, 0)
    c = lax.broadcasted_iota(jnp.int32, shape, 1) + lo
    n = i + (nq - 1) * tq - c
    ret = jnp.where(n < 0, half, 0)
    n = jnp.abs(n)
    n2 = n * n
    large = jnp.full(shape, max_exact, jnp.int32)
    for j in range(1, half - max_exact):
        large += (n2 >= (max_exact * max_exact) * (2 ** j)).astype(jnp.int32)
    bucket = ret + jnp.where(n < max_exact, n, large)
    acc = jnp.zeros(shape, F32)
    for b in range(N_REL_BUCKETS):
        acc = jnp.where(bucket == b, tab_ref[b, h] * LOG2_E, acc)
    o_ref[0, :, lo:hi] = acc


def bias_window(rel_table, seq, tq):
    nq = seq // tq
    nh = rel_table.shape[1]
    width = 2 * seq - tq
    assert tq >= REL_MAX_DISTANCE and nq >= 2
    return pl.pallas_call(
        functools.partial(_bias_window_kernel, tq=tq, nq=nq),
        grid=(nh,),
        in_specs=[pl.BlockSpec(memory_space=pltpu.SMEM)],
        out_specs=pl.BlockSpec((1, tq, width), lambda h: (h, 0, 0)),
        out_shape=jax.ShapeDtypeStruct((nh, tq, width), F32),
        compiler_params=_params(40, ("parallel",)),
        name="bias_window",
    )(rel_table)


def _lane_tile_reduce(x, op):
    out = x[:, 0:LANES]
    for t in range(1, x.shape[1] // LANES):
        out = op(out, x[:, t * LANES:(t + 1) * LANES])
    return out


def _attn_kernel(q_ref, k_ref, v_ref, wb_ref, lam_ref, sg_ref, o_ref, s0_ref, s1_ref, mx0_ref, mx1_ref,
                 *, lam_init, tq, nq, seq):
    i = pl.program_id(2)
    d = ATTN_HEAD_DIM
    ck = ATTN_KEY_CHUNK
    nc = seq // ck
    s_scr = (s0_ref, s1_ref)
    mx_scr = (mx0_ref, mx1_ref)
    chunks = [(m, c) for m in range(2) for c in range(nc)]

    def logits_chunk(p, m, c, run_max):
        off = pl.multiple_of((nq - 1 - i) * tq + c * ck, tq)
        s = lax.dot_general(q_ref[:, m * d:(m + 1) * d], k_ref[c * ck:(c + 1) * ck, m * d:(m + 1) * d],
                            (((1,), (1,)), ((), ())), preferred_element_type=F32)
        s = s + wb_ref[0, :, pl.ds(off, ck)]
        s_scr[p][m, :, c * ck:(c + 1) * ck] = s
        part = _lane_tile_reduce(s, jnp.maximum)
        run_max = part if c == 0 else jnp.maximum(run_max, part)
        if c == nc - 1:
            mx_scr[p][m] = jnp.broadcast_to(jnp.max(run_max, axis=-1, keepdims=True), run_max.shape)
        return run_max

    def probs_chunk(p, m, c, state):
        row_max, row_sum, out = state
        if c == 0:
            row_max = jnp.concatenate([mx_scr[p][m]] * (ck // LANES), axis=1)
        e = jnp.exp2(s_scr[p][m, :, c * ck:(c + 1) * ck] - row_max)
        part = _lane_tile_reduce(e, jnp.add)
        pv = jnp.dot(e.astype(BF16), v_ref[c * ck:(c + 1) * ck, :], preferred_element_type=F32)
        return (row_max, part if c == 0 else row_sum + part, pv if c == 0 else out + pv)

    def finish(results):
        (_, sum0, out0), (_, sum1, out1) = results
        lam_p = lam_ref[...]
        lam = (jnp.exp(jnp.sum(lam_p[0:1] * lam_p[1:2], axis=-1, keepdims=True))
               - jnp.exp(jnp.sum(lam_p[2:3] * lam_p[3:4], axis=-1, keepdims=True)) + lam_init)
        l0 = jnp.sum(sum0, axis=-1, keepdims=True)
        l1 = jnp.sum(sum1, axis=-1, keepdims=True)
        o = out0 * (1.0 / l0) - out1 * (lam / l1)
        ms = jnp.mean(o * o, axis=-1, keepdims=True)
        o = o * lax.rsqrt(ms + EPS) * sg_ref[...] * (1.0 - lam_init)
        o_ref[...] = o.astype(o_ref.dtype)

    def stage(p, do_logits, do_probs):
        lead = ATTN_FINISH_LEAD if do_logits and do_probs else 0
        run_max = None
        state = (None, None, None)
        results = []
        for idx in range(len(chunks) + lead):
            if do_probs and idx < len(chunks):
                m, c = chunks[idx]
                state = probs_chunk(1 - p, m, c, state)
                if c == nc - 1:
                    results.append(state)
                if idx == len(chunks) - 1:
                    finish(results)
            if do_logits and idx >= lead:
                m, c = chunks[idx - lead]
                run_max = logits_chunk(p, m, c, run_max)

    pl.when(i == 0)(lambda: stage(0, True, False))
    steady = jnp.logical_and(i >= 1, i < nq)
    pl.when(jnp.logical_and(steady, i % 2 == 0))(lambda: stage(0, True, True))
    pl.when(jnp.logical_and(steady, i % 2 == 1))(lambda: stage(1, True, True))
    pl.when(i == nq)(lambda: stage(nq % 2, False, True))


def diff_attention(u, wb, lambdas, sub_gain, lam_init, seq, n_heads, tq):
    m = u.shape[0]
    nb = m // seq
    nq = seq // tq
    hw = HEAD_WIDTH
    q_blk = D_FOURIER // hw
    k_blk = q_blk + n_heads
    v_blk = k_blk + n_heads
    assert ATTN_KEY_CHUNK % tq == 0 and seq % ATTN_KEY_CHUNK == 0
    logits = pltpu.VMEM((2, tq, seq), F32)
    maxima = pltpu.VMEM((2, tq, LANES), F32)
    return pl.pallas_call(
        functools.partial(_attn_kernel, lam_init=lam_init, tq=tq, nq=nq, seq=seq),
        grid=(nb, n_heads, nq + 1),
        in_specs=[pl.BlockSpec((tq, hw), lambda b, h, i: (b * nq + jnp.minimum(i, nq - 1), q_blk + h)),
                  pl.BlockSpec((seq, hw), lambda b, h, i: (b, k_blk + h)),
                  pl.BlockSpec((seq, hw), lambda b, h, i: (b, v_blk + h)),
                  pl.BlockSpec((1, tq, 2 * seq - tq), lambda b, h, i: (h, 0, 0)),
                  pl.BlockSpec((4, ATTN_HEAD_DIM), lambda b, h, i: (0, 0)),
                  pl.BlockSpec((1, hw), lambda b, h, i: (0, 0))],
        out_specs=pl.BlockSpec((tq, hw), lambda b, h, i: (b * nq + jnp.maximum(i - 1, 0), h)),
        out_shape=jax.ShapeDtypeStruct((m, n_heads * hw), BF16),
        scratch_shapes=[logits, logits, maxima, maxima],
        compiler_params=_params(48, ("parallel", "parallel", "arbitrary")),
        name="diff_attention",
    )(u, u, u, wb, lambdas, sub_gain.reshape(1, hw))


def _mix_head_kernel(h_ref, f_ref, a_ref, wg0_ref, wg1_ref, wf_ref, wa_ref, b0_ref, b1_ref, buf_ref,
                     o_ref, wg0b_ref, wg1b_ref, wfb_ref, wab_ref):
    g0 = jax.nn.sigmoid(_cast_dot(h_ref, wg0_ref, wg0b_ref) + b0_ref[...])
    t0 = g0 * _cast_dot(f_ref, wf_ref, wfb_ref)
    g1 = jax.nn.sigmoid(_cast_dot(h_ref, wg1_ref, wg1b_ref) + b1_ref[...])
    o_ref[...] = (t0 + g1 * _cast_dot(a_ref, wa_ref, wab_ref)).astype(o_ref.dtype)


def _mix_tail_kernel(h_ref, f_ref, a_ref, wg0_ref, wg1_ref, wf_ref, wa_ref, b0_ref, b1_ref, buf_ref, o_ref):
    h = h_ref[...]
    dot = functools.partial(jnp.dot, preferred_element_type=F32)
    g0 = jax.nn.sigmoid(dot(h, wg0_ref[...]) + b0_ref[...])
    t0 = g0 * dot(f_ref[...], wf_ref[...])
    g1 = jax.nn.sigmoid(dot(h, wg1_ref[...]) + b1_ref[...])
    o_ref[...] = (t0 + g1 * dot(a_ref[...], wa_ref[...])).astype(o_ref.dtype)


def gated_mix(h, f, a, w_gate, w_f, w_a, l, b_gate, buf, bm=1024, bn=256):
    m, d = h.shape
    df, da = f.shape[1], a.shape[1]
    nj = d // bn
    b_gate = b_gate.reshape(1, 2 * d)
    assert buf.shape == (m, d) and buf.dtype == BF16
    buf, wg0b, wg1b, wfb, wab = pl.pallas_call(
        _mix_head_kernel,
        grid=(nj,),
        in_specs=[_resident((HEAD_ROWS, d), lambda j: (0, 0)),
                  _resident((HEAD_ROWS, df), lambda j: (0, 0)),
                  _resident((HEAD_ROWS, da), lambda j: (0, 0)),
                  pl.BlockSpec((None, d, bn), lambda j: (l, 0, j)),
                  pl.BlockSpec((None, d, bn), lambda j: (l, 0, j + nj)),
                  pl.BlockSpec((None, df, bn), lambda j: (l, 0, j)),
                  pl.BlockSpec((None, da, bn), lambda j: (l, 0, j)),
                  pl.BlockSpec((1, bn), lambda j: (0, j)),
                  pl.BlockSpec((1, bn), lambda j: (0, j + nj)),
                  _ALIASED],
        out_specs=[pl.BlockSpec((HEAD_ROWS, bn), lambda j: (0, j)),
                   pl.BlockSpec((d, bn), lambda j: (0, j)),
                   pl.BlockSpec((d, bn), lambda j: (0, j)),
                   pl.BlockSpec((df, bn), lambda j: (0, j)),
                   pl.BlockSpec((da, bn), lambda j: (0, j))],
        out_shape=[jax.ShapeDtypeStruct((m, d), BF16), jax.ShapeDtypeStruct((d, d), BF16),
                   jax.ShapeDtypeStruct((d, d), BF16), jax.ShapeDtypeStruct((df, d), BF16),
                   jax.ShapeDtypeStruct((da, d), BF16)],
        input_output_aliases={9: 0},
        compiler_params=_params(56, ("arbitrary",)),
        name="gated_mix_head",
    )(h, f, a, w_gate, w_gate, w_f, w_a, b_gate, b_gate, buf)
    assert bm == HEAD_ROWS
    wcol = lambda i, j: (0, j)
    return pl.pallas_call(
        _mix_tail_kernel,
        grid=(m // bm - 1, nj),
        in_specs=[pl.BlockSpec((bm, d), lambda i, j: (i + 1, 0)),
                  pl.BlockSpec((bm, df), lambda i, j: (i + 1, 0)),
                  pl.BlockSpec((bm, da), lambda i, j: (i + 1, 0)),
                  pl.BlockSpec((d, bn), wcol),
                  pl.BlockSpec((d, bn), wcol),
                  pl.BlockSpec((df, bn), wcol),
                  pl.BlockSpec((da, bn), wcol),
                  pl.BlockSpec((1, bn), lambda i, j: (0, j)),
                  pl.BlockSpec((1, bn), lambda i, j: (0, j + nj)),
                  _ALIASED],
        out_specs=pl.BlockSpec((bm, bn), lambda i, j: (i + 1, j)),
        out_shape=jax.ShapeDtypeStruct((m, d), BF16),
        input_output_aliases={9: 0},
        compiler_params=_params(56, ("parallel", "arbitrary")),
        name="gated_mix",
    )(h, f, a, wg0b, wg1b, wfb, wab, b_gate, b_gate, buf)


def _proj_residual_head_kernel(a_ref, w_ref, r_ref, o_ref, wb_ref):
    o_ref[...] = r_ref[...] + _cast_dot(a_ref, w_ref, wb_ref)


def _proj_residual_tail_kernel(a_ref, wb_ref, r_ref, o_ref):
    o_ref[...] = r_ref[...] + jnp.dot(a_ref[...], wb_ref[...], preferred_element_type=F32)


def proj_residual(a, w, l, res, head_rows, bn_head, bm, bn):
    m, k = a.shape
    n = w.shape[2]
    res, wb = pl.pallas_call(
        _proj_residual_head_kernel,
        grid=(n // bn_head,),
        in_specs=[_resident((head_rows, k), lambda j: (0, 0)),
                  pl.BlockSpec((None, k, bn_head), lambda j: (l, 0, j)),
                  pl.BlockSpec((head_rows, bn_head), lambda j: (0, j))],
        out_specs=[pl.BlockSpec((head_rows, bn_head), lambda j: (0, j)),
                   pl.BlockSpec((k, bn_head), lambda j: (0, j))],
        out_shape=[jax.ShapeDtypeStruct((m, n), F32), jax.ShapeDtypeStruct((k, n), BF16)],
        input_output_aliases={2: 0},
        compiler_params=_params(56, ("arbitrary",)),
        name="proj_residual_head",
    )(a, w, res)
    skip = head_rows // bm
    return pl.pallas_call(
        _proj_residual_tail_kernel,
        grid=(m // bm - skip, n // bn),
        in_specs=[pl.BlockSpec((bm, k), lambda i, j: (i + skip, 0)),
                  pl.BlockSpec((k, bn), lambda i, j: (0, j)),
                  pl.BlockSpec((bm, bn), lambda i, j: (i + skip, j))],
        out_specs=pl.BlockSpec((bm, bn), lambda i, j: (i + skip, j)),
        out_shape=jax.ShapeDtypeStruct((m, n), F32),
        input_output_aliases={2: 0},
        compiler_params=_params(56, ("parallel", "arbitrary")),
        name="proj_residual",
    )(a, wb, res)


def _gelu_exact(x):
    return 0.5 * x * (1.0 + lax.erf(x * math.sqrt(0.5)))


def _build_halo_tile(ext_ref, h_ref, hp_ref, hn_ref):
    bm = h_ref.shape[0]
    halo = BF16_SUBLANES
    ext_ref[0:halo, :] = hp_ref[...]
    ext_ref[halo:halo + bm, :] = h_ref[...]
    ext_ref[halo + bm:, :] = hn_ref[...]


def _conv_glu_matmul_chunk(ext_ref, wg_ref, wv_ref, c, acc):
    halo = BF16_SUBLANES
    bm = ext_ref.shape[0] - 2 * halo
    kc = ext_ref.shape[1] // PIPELINE_CHUNKS
    ks = slice(c * kc, (c + 1) * kc)
    gd = jnp.dot(ext_ref[:, ks], wg_ref[ks, :], preferred_element_type=F32)
    vd = jnp.dot(ext_ref[halo:halo + bm, ks], wv_ref[ks, :], preferred_element_type=F32)
    return (gd, vd) if c == 0 else (acc[0] + gd, acc[1] + vd)


def _conv_glu_epilogue_chunk(gate_ref, val_ref, cw_ref, cb_ref, o_ref, c, row0, seq):
    halo = BF16_SUBLANES
    pad = F32_SUBLANES
    bm = val_ref.shape[0]
    rc = bm // PIPELINE_CHUNKS // EPILOGUE_SPLIT
    cw = cw_ref[...]
    tiles_align = seq % bm == 0
    for part in range(EPILOGUE_SPLIT):
        first = (c * EPILOGUE_SPLIT + part) * rc
        r0 = halo + first
        g = gate_ref[r0 - pad:r0 + rc + pad, :]
        g_prev = pltpu.roll(g, 1, axis=0)[pad:pad + rc]
        g_next = pltpu.roll(g, rc + 2 * pad - 1, axis=0)[pad:pad + rc]
        pos = (row0 + first + lax.broadcasted_iota(jnp.int32, (rc, 1), 0)) % seq
        if not tiles_align or first == 0:
            g_prev = jnp.where(pos == 0, 0.0, g_prev)
        if not tiles_align or first + rc == bm:
            g_next = jnp.where(pos == seq - 1, 0.0, g_next)
        conv = g_prev * cw[0:1] + g[pad:pad + rc] * cw[1:2] + g_next * cw[2:3] + cb_ref[...]
        rs = slice(first, first + rc)
        o_ref[rs, :] = (_gelu_exact(conv) * val_ref[rs, :]).astype(o_ref.dtype)


def _conv_glu_head_kernel(h_ref, hp_ref, hn_ref, wg_ref, wv_ref, cw_ref, cb_ref, buf_ref,
                          o_ref, wgo_ref, wvo_ref, ext_ref, wgb0, wgb1, wvb0, wvb1, gate0, gate1, val0, val1, *, seq, nj):
    j = pl.program_id(0)
    kc = h_ref.shape[1] // PIPELINE_CHUNKS
    wgb, wvb, gate_s, val_s = (wgb0, wgb1), (wvb0, wvb1), (gate0, gate1), (val0, val1)

    pl.when(j == 0)(lambda: _build_halo_tile(ext_ref, h_ref, hp_ref, hn_ref))

    def stage(p, do_cast, do_matmul, do_epilogue):
        q = 1 - p
        acc = None
        for c in range(PIPELINE_CHUNKS):
            if do_matmul:
                acc = _conv_glu_matmul_chunk(ext_ref, wgb[q], wvb[q], c, acc)
            if do_cast:
                ks = slice(c * kc, (c + 1) * kc)
                wg = wg_ref[ks, :].astype(BF16)
                wv = wv_ref[ks, :].astype(BF16)
                wgb[p][ks, :] = wg
                wvb[p][ks, :] = wv
                wgo_ref[ks, :] = wg
                wvo_ref[ks, :] = wv
            if do_epilogue:
                _conv_glu_epilogue_chunk(gate_s[p], val_s[p], cw_ref, cb_ref, o_ref, c, 0, seq)
        if do_matmul:
            gate_s[q][...], val_s[q][...] = acc

    pl.when(j == 0)(lambda: stage(0, True, False, False))
    pl.when(j == 1)(lambda: stage(1, True, True, False))
    steady = jnp.logical_and(j >= 2, j < nj)
    pl.when(jnp.logical_and(steady, j % 2 == 0))(lambda: stage(0, True, True, True))
    pl.when(jnp.logical_and(steady, j % 2 == 1))(lambda: stage(1, True, True, True))
    pl.when(j == nj)(lambda: stage(nj % 2, False, True, True))
    pl.when(j == nj + 1)(lambda: stage((nj + 1) % 2, False, False, True))


def _conv_glu_tail_kernel(h_ref, hp_ref, hn_ref, wg_ref, wv_ref, cw_ref, cb_ref, buf_ref, o_ref,
                          ext_ref, gate0, gate1, val0, val1, *, seq, nj):
    row0 = (pl.program_id(0) + 1) * h_ref.shape[0]
    j = pl.program_id(1)
    gate_s, val_s = (gate0, gate1), (val0, val1)

    pl.when(j == 0)(lambda: _build_halo_tile(ext_ref, h_ref, hp_ref, hn_ref))

    def stage(p, do_matmul, do_epilogue):
        q = 1 - p
        width = o_ref.shape[1] // TAIL_SUBTILES
        for t in range(TAIL_SUBTILES):
            cols = pl.ds(t * width, width)
            acc = None
            for c in range(PIPELINE_CHUNKS):
                if do_matmul:
                    acc = _conv_glu_matmul_chunk(ext_ref, wg_ref.at[:, cols], wv_ref.at[:, cols], c, acc)
                if do_epilogue:
                    _conv_glu_epilogue_chunk(gate_s[q].at[:, cols], val_s[q].at[:, cols], cw_ref.at[:, cols],
                                             cb_ref.at[:, cols], o_ref.at[:, cols], c, row0, seq)
            if do_matmul:
                gate_s[p][:, cols], val_s[p][:, cols] = acc

    pl.when(j == 0)(lambda: stage(0, True, False))
    steady = jnp.logical_and(j >= 1, j < nj)
    pl.when(jnp.logical_and(steady, j % 2 == 0))(lambda: stage(0, True, True))
    pl.when(jnp.logical_and(steady, j % 2 == 1))(lambda: stage(1, True, True))
    pl.when(j == nj)(lambda: stage(nj % 2, False, True))


def conv_glu_up(h, w_up, l, conv_w, conv_b, seq, buf, bm=1024, bn_head=256, bn=256 * TAIL_SUBTILES):
    m, k = h.shape
    dff = w_up.shape[2] // 2
    assert buf.shape == (m, dff) and buf.dtype == BF16
    halo = BF16_SUBLANES
    rb = bm // halo
    last = m // halo - 1
    conv_b = conv_b.reshape(1, dff)
    assert bm == HEAD_ROWS
    prev_tile = lambda i: jnp.maximum(i * rb - 1, 0)
    next_tile = lambda i: jnp.minimum((i + 1) * rb, last)
    ext = pltpu.VMEM((bm + 2 * halo, k), BF16)

    nj = dff // bn_head
    assert nj >= 2
    w_tile = lambda j: jnp.minimum(j, nj - 1)
    o_tile = lambda j: jnp.clip(j - 2, 0, nj - 1)
    wb = pltpu.VMEM((k, bn_head), BF16)
    gate = pltpu.VMEM((bm + 2 * halo, bn_head), F32)
    val = pltpu.VMEM((bm, bn_head), F32)
    wb_out = jax.ShapeDtypeStruct((k, dff), BF16)
    buf, wgb, wvb = pl.pallas_call(
        functools.partial(_conv_glu_head_kernel, seq=seq, nj=nj),
        grid=(nj + 2,),
        in_specs=[_resident((bm, k), lambda j: (0, 0)),
                  _resident((halo, k), lambda j: (prev_tile(0), 0)),
                  _resident((halo, k), lambda j: (next_tile(0), 0)),
                  pl.BlockSpec((None, k, bn_head), lambda j: (l, 0, w_tile(j))),
                  pl.BlockSpec((None, k, bn_head), lambda j: (l, 0, w_tile(j) + nj)),
                  pl.BlockSpec((CONV_WIDTH, bn_head), lambda j: (0, o_tile(j))),
                  pl.BlockSpec((1, bn_head), lambda j: (0, o_tile(j))),
                  _ALIASED],
        out_specs=[pl.BlockSpec((bm, bn_head), lambda j: (0, o_tile(j))),
                   pl.BlockSpec((k, bn_head), lambda j: (0, w_tile(j))),
                   pl.BlockSpec((k, bn_head), lambda j: (0, w_tile(j)))],
        out_shape=[jax.ShapeDtypeStruct((m, dff), BF16), wb_out, wb_out],
        input_output_aliases={7: 0},
        scratch_shapes=[ext, wb, wb, wb, wb, gate, gate, val, val],
        compiler_params=_params(56, ("arbitrary",)),
        name="conv_glu_up_head",
    )(h, h, h, w_up, w_up, conv_w, conv_b, buf)

    nj = dff // bn
    w_tile = lambda j: jnp.minimum(j, nj - 1)
    o_tile = lambda j: jnp.clip(j - 1, 0, nj - 1)
    gate = pltpu.VMEM((bm + 2 * halo, bn), F32)
    val = pltpu.VMEM((bm, bn), F32)
    return pl.pallas_call(
        functools.partial(_conv_glu_tail_kernel, seq=seq, nj=nj),
        grid=(m // bm - 1, nj + 1),
        in_specs=[_resident((bm, k), lambda i, j: (i + 1, 0)),
                  _resident((halo, k), lambda i, j: (prev_tile(i + 1), 0)),
                  _resident((halo, k), lambda i, j: (next_tile(i + 1), 0)),
                  pl.BlockSpec((k, bn), lambda i, j: (0, w_tile(j))),
                  pl.BlockSpec((k, bn), lambda i, j: (0, w_tile(j))),
                  pl.BlockSpec((CONV_WIDTH, bn), lambda i, j: (0, o_tile(j))),
                  pl.BlockSpec((1, bn), lambda i, j: (0, o_tile(j))),
                  _ALIASED],
        out_specs=pl.BlockSpec((bm, bn), lambda i, j: (i + 1, o_tile(j))),
        out_shape=jax.ShapeDtypeStruct((m, dff), BF16),
        input_output_aliases={7: 0},
        scratch_shapes=[ext, gate, gate, val, val],
        compiler_params=_params(56, ("parallel", "arbitrary")),
        name="conv_glu_up",
    )(h, h, h, wgb, wvb, conv_w, conv_b, buf)


def kernel(x, norm1_gain, w_in, w_fourier_out, lambdas, subln_gain, rel_bias_table, w_attn_out,
           w_gate, b_gate, w_o, norm2_gain, w_up, conv_w, conv_b, w_down, final_norm_gain):
    batch, seq, d_model = x.shape
    depth = w_in.shape[0]
    n_heads = rel_bias_table.shape[1]
    d_attn = n_heads * HEAD_WIDTH
    d_in = w_in.shape[2]
    tq = 512

    cs, sn, chan = _dft_constants(seq, FOURIER_GROUP_DIM)
    col = np.ones((1, d_in), np.float32)
    col[:, D_FOURIER:D_FOURIER + d_attn] = ATTN_HEAD_DIM ** -0.5 * LOG2_E
    col_scale = jnp.asarray(col)
    wb = bias_window(rel_bias_table, seq, tq)

    x = x.reshape(batch * seq, d_model)
    m = batch * seq
    u = jnp.zeros((m, d_in), BF16)
    mixed = jnp.zeros((m, d_model), BF16)
    act = jnp.zeros((m, w_up.shape[2] // 2), BF16)
    for l in range(depth):
        lam_init = 0.8 - 0.6 * math.exp(-0.3 * l)
        h = rmsnorm(x, norm1_gain[l], BF16)
        u = in_proj(h, w_in, l, col_scale, u)
        t1, t2 = fourier_chan(u, chan)
        f = fourier_seq(cs, sn, t1, t2, seq)
        a = diff_attention(u, wb, lambdas[l], subln_gain[l], lam_init, seq, n_heads, tq)
        mixed = gated_mix(h, f, a, w_gate, w_fourier_out, w_attn_out, l, b_gate[l], mixed)
        x = proj_residual(mixed, w_o, l, x, head_rows=2 * HEAD_ROWS, bn_head=256, bm=1024, bn=1024)
        h2 = rmsnorm(x, norm2_gain[l], BF16)
        act = conv_glu_up(h2, w_up, l, conv_w[l], conv_b[l], seq, act)
        x = proj_residual(act, w_down, l, x, head_rows=HEAD_ROWS, bn_head=256, bm=1024, bn=256)
    out = rmsnorm(x, final_norm_gain, F32)
    return out.reshape(batch, seq, d_model)
```

```python
import functools
import math

import numpy as np
import jax
import jax.numpy as jnp
from jax import lax
from jax.experimental import pallas as pl
from jax.experimental.pallas import tpu as pltpu

F32 = jnp.float32
BF16 = jnp.bfloat16

EPS = 1e-6
N_FOURIER_GROUPS = 4
FOURIER_GROUP_DIM = 256
D_FOURIER = N_FOURIER_GROUPS * FOURIER_GROUP_DIM
ATTN_HEAD_DIM = 128
HEAD_WIDTH = 2 * ATTN_HEAD_DIM
N_REL_BUCKETS = 32
REL_MAX_DISTANCE = 128
LOG2_E = math.log2(math.e)
ATTN_KEY_CHUNK = 2048
ATTN_FINISH_LEAD = 1
CONV_WIDTH = 3
LANES = 128
BF16_SUBLANES = 16
F32_SUBLANES = 8
PIPELINE_CHUNKS = 16
MXU_DEPTH = 256
EPILOGUE_SPLIT = 1
TAIL_SUBTILES = 2

MIB = 1024 * 1024


def _params(vmem_mib, semantics):
    return pltpu.CompilerParams(dimension_semantics=semantics, vmem_limit_bytes=vmem_mib * MIB)


def _resident(shape, index_map):
    return pl.BlockSpec(shape, index_map, pipeline_mode=pl.Buffered(1))


def _rmsnorm_kernel(x_ref, g_ref, o_ref):
    x = x_ref[...]
    ms = jnp.mean(x * x, axis=-1, keepdims=True)
    o_ref[...] = (x * lax.rsqrt(ms + EPS) * g_ref[...]).astype(o_ref.dtype)


def rmsnorm(x, gain, out_dtype, bm=512):
    m, d = x.shape
    return pl.pallas_call(
        _rmsnorm_kernel,
        grid=(m // bm,),
        in_specs=[pl.BlockSpec((bm, d), lambda i: (i, 0)),
                  pl.BlockSpec((1, d), lambda i: (0, 0))],
        out_specs=pl.BlockSpec((bm, d), lambda i: (i, 0)),
        out_shape=jax.ShapeDtypeStruct((m, d), out_dtype),
        compiler_params=_params(40, ("parallel",)),
        name="rmsnorm",
    )(x, gain.reshape(1, d))


HEAD_ROWS = 1024
_ALIASED = pl.BlockSpec(memory_space=pl.ANY)


def _cast_dot(lhs_ref, w_ref, wb_ref):
    kc = MXU_DEPTH
    acc = None
    for c in range(w_ref.shape[0] // kc):
        ks = slice(c * kc, (c + 1) * kc)
        wb_ref[ks, :] = w_ref[ks, :].astype(BF16)
        part = jnp.dot(lhs_ref[:, ks], wb_ref[ks, :], preferred_element_type=F32)
        acc = part if c == 0 else acc + part
    return acc


def _in_proj_head_kernel(h_ref, w_ref, s_ref, buf_ref, o_ref, wb_ref):
    o_ref[...] = (_cast_dot(h_ref, w_ref, wb_ref) * s_ref[...]).astype(o_ref.dtype)


def _in_proj_tail_kernel(h_ref, wb_ref, s_ref, buf_ref, o_ref):
    acc = jnp.dot(h_ref[...], wb_ref[...], preferred_element_type=F32)
    o_ref[...] = (acc * s_ref[...]).astype(o_ref.dtype)


def in_proj(h, w, l, col_scale, buf, head_rows=2 * HEAD_ROWS, bm=1024, bn_head=512, bn=1024):
    m, k = h.shape
    n = w.shape[2]
    assert buf.shape == (m, n) and buf.dtype == BF16
    buf, wb = pl.pallas_call(
        _in_proj_head_kernel,
        grid=(n // bn_head,),
        in_specs=[_resident((head_rows, k), lambda j: (0, 0)),
                  pl.BlockSpec((None, k, bn_head), lambda j: (l, 0, j)),
                  pl.BlockSpec((1, bn_head), lambda j: (0, j)),
                  _ALIASED],
        out_specs=[pl.BlockSpec((head_rows, bn_head), lambda j: (0, j)),
                   pl.BlockSpec((k, bn_head), lambda j: (0, j))],
        out_shape=[jax.ShapeDtypeStruct((m, n), BF16), jax.ShapeDtypeStruct((k, n), BF16)],
        input_output_aliases={3: 0},
        compiler_params=_params(48, ("arbitrary",)),
        name="in_proj_head",
    )(h, w, col_scale, buf)
    skip = head_rows // bm
    return pl.pallas_call(
        _in_proj_tail_kernel,
        grid=(m // bm - skip, n // bn),
        in_specs=[pl.BlockSpec((bm, k), lambda i, j: (i + skip, 0)),
                  pl.BlockSpec((k, bn), lambda i, j: (0, j)),
                  pl.BlockSpec((1, bn), lambda i, j: (0, j)),
                  _ALIASED],
        out_specs=pl.BlockSpec((bm, bn), lambda i, j: (i + skip, j)),
        out_shape=jax.ShapeDtypeStruct((m, n), BF16),
        input_output_aliases={3: 0},
        compiler_params=_params(48, ("parallel", "arbitrary")),
        name="in_proj",
    )(h, wb, col_scale, buf)


def _dft_constants(seq, dim):
    def cos_sin(n):
        idx = np.arange(n, dtype=np.int64)
        ang = 2.0 * np.pi * ((idx[:, None] * idx[None, :]) % n).astype(np.float64) / n
        return np.cos(ang), np.sin(ang)
    cs, ss = cos_sin(seq)
    cc, sc = cos_sin(dim)
    chan = np.concatenate([cc, sc], axis=1)
    return (jnp.asarray(cs, BF16), jnp.asarray(-ss, BF16), jnp.asarray(chan, BF16))


def _fourier_chan_kernel(u_ref, c_ref, t1_ref, t2_ref):
    c = c_ref[...]
    gd = FOURIER_GROUP_DIM
    for g in range(N_FOURIER_GROUPS):
        t = jnp.dot(u_ref[:, g * gd:(g + 1) * gd], c, preferred_element_type=F32)
        t1_ref[:, g * gd:(g + 1) * gd] = t[:, :gd].astype(t1_ref.dtype)
        t2_ref[:, g * gd:(g + 1) * gd] = t[:, gd:].astype(t2_ref.dtype)


def fourier_chan(u, chan, bm=1024):
    m = u.shape[0]
    gd = FOURIER_GROUP_DIM
    out = jax.ShapeDtypeStruct((m, D_FOURIER), BF16)
    return pl.pallas_call(
        _fourier_chan_kernel,
        grid=(m // bm,),
        in_specs=[pl.BlockSpec((bm, D_FOURIER), lambda i: (i, 0)),
                  pl.BlockSpec((gd, 2 * gd), lambda i: (0, 0))],
        out_specs=[pl.BlockSpec((bm, D_FOURIER), lambda i: (i, 0)),
                   pl.BlockSpec((bm, D_FOURIER), lambda i: (i, 0))],
        out_shape=[out, out],
        compiler_params=_params(32, ("parallel",)),
        name="fourier_chan",
    )(u, chan)


def _fourier_seq_kernel(cs_ref, sn_ref, t1_ref, t2_ref, o_ref, *, scale):
    acc = jnp.dot(cs_ref[...], t1_ref[...], preferred_element_type=F32)
    acc += jnp.dot(sn_ref[...], t2_ref[...], preferred_element_type=F32)
    o_ref[...] = (acc * scale).astype(o_ref.dtype)


def fourier_seq(cs, sn, t1, t2, seq, bm=512):
    m, n = t1.shape
    nb = m // seq
    ni = seq // bm
    scale = 1.0 / math.sqrt(seq * FOURIER_GROUP_DIM)
    return pl.pallas_call(
        functools.partial(_fourier_seq_kernel, scale=scale),
        grid=(nb, ni),
        in_specs=[pl.BlockSpec((bm, seq), lambda b, i: (i, 0)),
                  pl.BlockSpec((bm, seq), lambda b, i: (i, 0)),
                  pl.BlockSpec((seq, n), lambda b, i: (b, 0)),
                  pl.BlockSpec((seq, n), lambda b, i: (b, 0))],
        out_specs=pl.BlockSpec((bm, n), lambda b, i: (b * ni + i, 0)),
        out_shape=jax.ShapeDtypeStruct((m, n), BF16),
        compiler_params=_params(40, ("parallel", "arbitrary")),
        name="fourier_seq",
    )(cs, sn, t1, t2)


def _bias_window_kernel(tab_ref, o_ref, *, tq, nq):
    h = pl.program_id(0)
    half = N_REL_BUCKETS // 2
    max_exact = half // 2
    width = o_ref.shape[2]
    lo = (nq - 1) * tq - REL_MAX_DISTANCE
    hi = nq * tq + REL_MAX_DISTANCE
    o_ref[0, :, :lo] = jnp.full((tq, lo), tab_ref[half - 1, h] * LOG2_E, F32)
    o_ref[0, :, hi:] = jnp.full((tq, width - hi), tab_ref[N_REL_BUCKETS - 1, h] * LOG2_E, F32)

    shape = (tq, hi - lo)
    i = lax.broadcasted_iota(jnp.int32, shape, 0)
    c = lax.broadcasted_iota(jnp.int32, shape, 1) + lo
    n = i + (nq - 1) * tq - c
    ret = jnp.where(n < 0, half, 0)
    n = jnp.abs(n)
    n2 = n * n
    large = jnp.full(shape, max_exact, jnp.int32)
    for j in range(1, half - max_exact):
        large += (n2 >= (max_exact * max_exact) * (2 ** j)).astype(jnp.int32)
    bucket = ret + jnp.where(n < max_exact, n, large)
    acc = jnp.zeros(shape, F32)
    for b in range(N_REL_BUCKETS):
        acc = jnp.where(bucket == b, tab_ref[b, h] * LOG2_E, acc)
    o_ref[0, :, lo:hi] = acc


def bias_window(rel_table, seq, tq):
    nq = seq // tq
    nh = rel_table.shape[1]
    width = 2 * seq - tq
    assert tq >= REL_MAX_DISTANCE and nq >= 2
    return pl.pallas_call(
        functools.partial(_bias_window_kernel, tq=tq, nq=nq),
        grid=(nh,),
        in_specs=[pl.BlockSpec(memory_space=pltpu.SMEM)],
        out_specs=pl.BlockSpec((1, tq, width), lambda h: (h, 0, 0)),
        out_shape=jax.ShapeDtypeStruct((nh, tq, width), F32),
        compiler_params=_params(40, ("parallel",)),
        name="bias_window",
    )(rel_table)


def _lane_tile_reduce(x, op):
    out = x[:, 0:LANES]
    for t in range(1, x.shape[1] // LANES):
        out = op(out, x[:, t * LANES:(t + 1) * LANES])
    return out


def _attn_kernel(q_ref, k_ref, v_ref, wb_ref, lam_ref, sg_ref, o_ref, s0_ref, s1_ref, mx0_ref, mx1_ref,
                 *, lam_init, tq, nq, seq):
    i = pl.program_id(2)
    d = ATTN_HEAD_DIM
    ck = ATTN_KEY_CHUNK
    nc = seq // ck
    s_scr = (s0_ref, s1_ref)
    mx_scr = (mx0_ref, mx1_ref)
    chunks = [(m, c) for m in range(2) for c in range(nc)]

    def logits_chunk(p, m, c, run_max):
        off = pl.multiple_of((nq - 1 - i) * tq + c * ck, tq)
        s = lax.dot_general(q_ref[:, m * d:(m + 1) * d], k_ref[c * ck:(c + 1) * ck, m * d:(m + 1) * d],
                            (((1,), (1,)), ((), ())), preferred_element_type=F32)
        s = s + wb_ref[0, :, pl.ds(off, ck)]
        s_scr[p][m, :, c * ck:(c + 1) * ck] = s
        part = _lane_tile_reduce(s, jnp.maximum)
        run_max = part if c == 0 else jnp.maximum(run_max, part)
        if c == nc - 1:
            mx_scr[p][m] = jnp.broadcast_to(jnp.max(run_max, axis=-1, keepdims=True), run_max.shape)
        return run_max

    def probs_chunk(p, m, c, state):
        row_max, row_sum, out = state
        if c == 0:
            row_max = jnp.concatenate([mx_scr[p][m]] * (ck // LANES), axis=1)
        e = jnp.exp2(s_scr[p][m, :, c * ck:(c + 1) * ck] - row_max)
        part = _lane_tile_reduce(e, jnp.add)
        pv = jnp.dot(e.astype(BF16), v_ref[c * ck:(c + 1) * ck, :], preferred_element_type=F32)
        return (row_max, part if c == 0 else row_sum + part, pv if c == 0 else out + pv)

    def finish(results):
        (_, sum0, out0), (_, sum1, out1) = results
        lam_p = lam_ref[...]
        lam = (jnp.exp(jnp.sum(lam_p[0:1] * lam_p[1:2], axis=-1, keepdims=True))
               - jnp.exp(jnp.sum(lam_p[2:3] * lam_p[3:4], axis=-1, keepdims=True)) + lam_init)
        l0 = jnp.sum(sum0, axis=-1, keepdims=True)
        l1 = jnp.sum(sum1, axis=-1, keepdims=True)
        o = out0 * (1.0 / l0) - out1 * (lam / l1)
        ms = jnp.mean(o * o, axis=-1, keepdims=True)
        o = o * lax.rsqrt(ms + EPS) * sg_ref[...] * (1.0 - lam_init)
        o_ref[...] = o.astype(o_ref.dtype)

    def stage(p, do_logits, do_probs):
        lead = ATTN_FINISH_LEAD if do_logits and do_probs else 0
        run_max = None
        state = (None, None, None)
        results = []
        for idx in range(len(chunks) + lead):
            if do_probs and idx < len(chunks):
                m, c = chunks[idx]
                state = probs_chunk(1 - p, m, c, state)
                if c == nc - 1:
                    results.append(state)
                if idx == len(chunks) - 1:
                    finish(results)
            if do_logits and idx >= lead:
                m, c = chunks[idx - lead]
                run_max = logits_chunk(p, m, c, run_max)

    pl.when(i == 0)(lambda: stage(0, True, False))
    steady = jnp.logical_and(i >= 1, i < nq)
    pl.when(jnp.logical_and(steady, i % 2 == 0))(lambda: stage(0, True, True))
    pl.when(jnp.logical_and(steady, i % 2 == 1))(lambda: stage(1, True, True))
    pl.when(i == nq)(lambda: stage(nq % 2, False, True))


def diff_attention(u, wb, lambdas, sub_gain, lam_init, seq, n_heads, tq):
    m = u.shape[0]
    nb = m // seq
    nq = seq // tq
    hw = HEAD_WIDTH
    q_blk = D_FOURIER // hw
    k_blk = q_blk + n_heads
    v_blk = k_blk + n_heads
    assert ATTN_KEY_CHUNK % tq == 0 and seq % ATTN_KEY_CHUNK == 0
    logits = pltpu.VMEM((2, tq, seq), F32)
    maxima = pltpu.VMEM((2, tq, LANES), F32)
    return pl.pallas_call(
        functools.partial(_attn_kernel, lam_init=lam_init, tq=tq, nq=nq, seq=seq),
        grid=(nb, n_heads, nq + 1),
        in_specs=[pl.BlockSpec((tq, hw), lambda b, h, i: (b * nq + jnp.minimum(i, nq - 1), q_blk + h)),
                  pl.BlockSpec((seq, hw), lambda b, h, i: (b, k_blk + h)),
                  pl.BlockSpec((seq, hw), lambda b, h, i: (b, v_blk + h)),
                  pl.BlockSpec((1, tq, 2 * seq - tq), lambda b, h, i: (h, 0, 0)),
                  pl.BlockSpec((4, ATTN_HEAD_DIM), lambda b, h, i: (0, 0)),
                  pl.BlockSpec((1, hw), lambda b, h, i: (0, 0))],
        out_specs=pl.BlockSpec((tq, hw), lambda b, h, i: (b * nq + jnp.maximum(i - 1, 0), h)),
        out_shape=jax.ShapeDtypeStruct((m, n_heads * hw), BF16),
        scratch_shapes=[logits, logits, maxima, maxima],
        compiler_params=_params(48, ("parallel", "parallel", "arbitrary")),
        name="diff_attention",
    )(u, u, u, wb, lambdas, sub_gain.reshape(1, hw))


def _mix_head_kernel(h_ref, f_ref, a_ref, wg0_ref, wg1_ref, wf_ref, wa_ref, b0_ref, b1_ref, buf_ref,
                     o_ref, wg0b_ref, wg1b_ref, wfb_ref, wab_ref):
    g0 = jax.nn.sigmoid(_cast_dot(h_ref, wg0_ref, wg0b_ref) + b0_ref[...])
    t0 = g0 * _cast_dot(f_ref, wf_ref, wfb_ref)
    g1 = jax.nn.sigmoid(_cast_dot(h_ref, wg1_ref, wg1b_ref) + b1_ref[...])
    o_ref[...] = (t0 + g1 * _cast_dot(a_ref, wa_ref, wab_ref)).astype(o_ref.dtype)


def _mix_tail_kernel(h_ref, f_ref, a_ref, wg0_ref, wg1_ref, wf_ref, wa_ref, b0_ref, b1_ref, buf_ref, o_ref):
    h = h_ref[...]
    dot = functools.partial(jnp.dot, preferred_element_type=F32)
    g0 = jax.nn.sigmoid(dot(h, wg0_ref[...]) + b0_ref[...])
    t0 = g0 * dot(f_ref[...], wf_ref[...])
    g1 = jax.nn.sigmoid(dot(h, wg1_ref[...]) + b1_ref[...])
    o_ref[...] = (t0 + g1 * dot(a_ref[...], wa_ref[...])).astype(o_ref.dtype)


def gated_mix(h, f, a, w_gate, w_f, w_a, l, b_gate, buf, bm=1024, bn=256):
    m, d = h.shape
    df, da = f.shape[1], a.shape[1]
    nj = d // bn
    b_gate = b_gate.reshape(1, 2 * d)
    assert buf.shape == (m, d) and buf.dtype == BF16
    buf, wg0b, wg1b, wfb, wab = pl.pallas_call(
        _mix_head_kernel,
        grid=(nj,),
        in_specs=[_resident((HEAD_ROWS, d), lambda j: (0, 0)),
                  _resident((HEAD_ROWS, df), lambda j: (0, 0)),
                  _resident((HEAD_ROWS, da), lambda j: (0, 0)),
                  pl.BlockSpec((None, d, bn), lambda j: (l, 0, j)),
                  pl.BlockSpec((None, d, bn), lambda j: (l, 0, j + nj)),
                  pl.BlockSpec((None, df, bn), lambda j: (l, 0, j)),
                  pl.BlockSpec((None, da, bn), lambda j: (l, 0, j)),
                  pl.BlockSpec((1, bn), lambda j: (0, j)),
                  pl.BlockSpec((1, bn), lambda j: (0, j + nj)),
                  _ALIASED],
        out_specs=[pl.BlockSpec((HEAD_ROWS, bn), lambda j: (0, j)),
                   pl.BlockSpec((d, bn), lambda j: (0, j)),
                   pl.BlockSpec((d, bn), lambda j: (0, j)),
                   pl.BlockSpec((df, bn), lambda j: (0, j)),
                   pl.BlockSpec((da, bn), lambda j: (0, j))],
        out_shape=[jax.ShapeDtypeStruct((m, d), BF16), jax.ShapeDtypeStruct((d, d), BF16),
                   jax.ShapeDtypeStruct((d, d), BF16), jax.ShapeDtypeStruct((df, d), BF16),
                   jax.ShapeDtypeStruct((da, d), BF16)],
        input_output_aliases={9: 0},
        compiler_params=_params(56, ("arbitrary",)),
        name="gated_mix_head",
    )(h, f, a, w_gate, w_gate, w_f, w_a, b_gate, b_gate, buf)
    assert bm == HEAD_ROWS
    wcol = lambda i, j: (0, j)
    return pl.pallas_call(
        _mix_tail_kernel,
        grid=(m // bm - 1, nj),
        in_specs=[pl.BlockSpec((bm, d), lambda i, j: (i + 1, 0)),
                  pl.BlockSpec((bm, df), lambda i, j: (i + 1, 0)),
                  pl.BlockSpec((bm, da), lambda i, j: (i + 1, 0)),
                  pl.BlockSpec((d, bn), wcol),
                  pl.BlockSpec((d, bn), wcol),
                  pl.BlockSpec((df, bn), wcol),
                  pl.BlockSpec((da, bn), wcol),
                  pl.BlockSpec((1, bn), lambda i, j: (0, j)),
                  pl.BlockSpec((1, bn), lambda i, j: (0, j + nj)),
                  _ALIASED],
        out_specs=pl.BlockSpec((bm, bn), lambda i, j: (i + 1, j)),
        out_shape=jax.ShapeDtypeStruct((m, d), BF16),
        input_output_aliases={9: 0},
        compiler_params=_params(56, ("parallel", "arbitrary")),
        name="gated_mix",
    )(h, f, a, wg0b, wg1b, wfb, wab, b_gate, b_gate, buf)


def _proj_residual_head_kernel(a_ref, w_ref, r_ref, o_ref, wb_ref):
    o_ref[...] = r_ref[...] + _cast_dot(a_ref, w_ref, wb_ref)


def _proj_residual_tail_kernel(a_ref, wb_ref, r_ref, o_ref):
    o_ref[...] = r_ref[...] + jnp.dot(a_ref[...], wb_ref[...], preferred_element_type=F32)


def proj_residual(a, w, l, res, head_rows, bn_head, bm, bn):
    m, k = a.shape
    n = w.shape[2]
    res, wb = pl.pallas_call(
        _proj_residual_head_kernel,
        grid=(n // bn_head,),
        in_specs=[_resident((head_rows, k), lambda j: (0, 0)),
                  pl.BlockSpec((None, k, bn_head), lambda j: (l, 0, j)),
                  pl.BlockSpec((head_rows, bn_head), lambda j: (0, j))],
        out_specs=[pl.BlockSpec((head_rows, bn_head), lambda j: (0, j)),
                   pl.BlockSpec((k, bn_head), lambda j: (0, j))],
        out_shape=[jax.ShapeDtypeStruct((m, n), F32), jax.ShapeDtypeStruct((k, n), BF16)],
        input_output_aliases={2: 0},
        compiler_params=_params(56, ("arbitrary",)),
        name="proj_residual_head",
    )(a, w, res)
    skip = head_rows // bm
    return pl.pallas_call(
        _proj_residual_tail_kernel,
        grid=(m // bm - skip, n // bn),
        in_specs=[pl.BlockSpec((bm, k), lambda i, j: (i + skip, 0)),
                  pl.BlockSpec((k, bn), lambda i, j: (0, j)),
                  pl.BlockSpec((bm, bn), lambda i, j: (i + skip, j))],
        out_specs=pl.BlockSpec((bm, bn), lambda i, j: (i + skip, j)),
        out_shape=jax.ShapeDtypeStruct((m, n), F32),
        input_output_aliases={2: 0},
        compiler_params=_params(56, ("parallel", "arbitrary")),
        name="proj_residual",
    )(a, wb, res)


def _gelu_exact(x):
    return 0.5 * x * (1.0 + lax.erf(x * math.sqrt(0.5)))


def _build_halo_tile(ext_ref, h_ref, hp_ref, hn_ref):
    bm = h_ref.shape[0]
    halo = BF16_SUBLANES
    ext_ref[0:halo, :] = hp_ref[...]
    ext_ref[halo:halo + bm, :] = h_ref[...]
    ext_ref[halo + bm:, :] = hn_ref[...]


def _conv_glu_matmul_chunk(ext_ref, wg_ref, wv_ref, c, acc):
    halo = BF16_SUBLANES
    bm = ext_ref.shape[0] - 2 * halo
    kc = ext_ref.shape[1] // PIPELINE_CHUNKS
    ks = slice(c * kc, (c + 1) * kc)
    gd = jnp.dot(ext_ref[:, ks], wg_ref[ks, :], preferred_element_type=F32)
    vd = jnp.dot(ext_ref[halo:halo + bm, ks], wv_ref[ks, :], preferred_element_type=F32)
    return (gd, vd) if c == 0 else (acc[0] + gd, acc[1] + vd)


def _conv_glu_epilogue_chunk(gate_ref, val_ref, cw_ref, cb_ref, o_ref, c, row0, seq):
    halo = BF16_SUBLANES
    pad = F32_SUBLANES
    bm = val_ref.shape[0]
    rc = bm // PIPELINE_CHUNKS // EPILOGUE_SPLIT
    cw = cw_ref[...]
    tiles_align = seq % bm == 0
    for part in range(EPILOGUE_SPLIT):
        first = (c * EPILOGUE_SPLIT + part) * rc
        r0 = halo + first
        g = gate_ref[r0 - pad:r0 + rc + pad, :]
        g_prev = pltpu.roll(g, 1, axis=0)[pad:pad + rc]
        g_next = pltpu.roll(g, rc + 2 * pad - 1, axis=0)[pad:pad + rc]
        pos = (row0 + first + lax.broadcasted_iota(jnp.int32, (rc, 1), 0)) % seq
        if not tiles_align or first == 0:
            g_prev = jnp.where(pos == 0, 0.0, g_prev)
        if not tiles_align or first + rc == bm:
            g_next = jnp.where(pos == seq - 1, 0.0, g_next)
        conv = g_prev * cw[0:1] + g[pad:pad + rc] * cw[1:2] + g_next * cw[2:3] + cb_ref[...]
        rs = slice(first, first + rc)
        o_ref[rs, :] = (_gelu_exact(conv) * val_ref[rs, :]).astype(o_ref.dtype)


def _conv_glu_head_kernel(h_ref, hp_ref, hn_ref, wg_ref, wv_ref, cw_ref, cb_ref, buf_ref,
                          o_ref, wgo_ref, wvo_ref, ext_ref, wgb0, wgb1, wvb0, wvb1, gate0, gate1, val0, val1, *, seq, nj):
    j = pl.program_id(0)
    kc = h_ref.shape[1] // PIPELINE_CHUNKS
    wgb, wvb, gate_s, val_s = (wgb0, wgb1), (wvb0, wvb1), (gate0, gate1), (val0, val1)

    pl.when(j == 0)(lambda: _build_halo_tile(ext_ref, h_ref, hp_ref, hn_ref))

    def stage(p, do_cast, do_matmul, do_epilogue):
        q = 1 - p
        acc = None
        for c in range(PIPELINE_CHUNKS):
            if do_matmul:
                acc = _conv_glu_matmul_chunk(ext_ref, wgb[q], wvb[q], c, acc)
            if do_cast:
                ks = slice(c * kc, (c + 1) * kc)
                wg = wg_ref[ks, :].astype(BF16)
                wv = wv_ref[ks, :].astype(BF16)
                wgb[p][ks, :] = wg
                wvb[p][ks, :] = wv
                wgo_ref[ks, :] = wg
                wvo_ref[ks, :] = wv
            if do_epilogue:
                _conv_glu_epilogue_chunk(gate_s[p], val_s[p], cw_ref, cb_ref, o_ref, c, 0, seq)
        if do_matmul:
            gate_s[q][...], val_s[q][...] = acc

    pl.when(j == 0)(lambda: stage(0, True, False, False))
    pl.when(j == 1)(lambda: stage(1, True, True, False))
    steady = jnp.logical_and(j >= 2, j < nj)
    pl.when(jnp.logical_and(steady, j % 2 == 0))(lambda: stage(0, True, True, True))
    pl.when(jnp.logical_and(steady, j % 2 == 1))(lambda: stage(1, True, True, True))
    pl.when(j == nj)(lambda: stage(nj % 2, False, True, True))
    pl.when(j == nj + 1)(lambda: stage((nj + 1) % 2, False, False, True))


def _conv_glu_tail_kernel(h_ref, hp_ref, hn_ref, wg_ref, wv_ref, cw_ref, cb_ref, buf_ref, o_ref,
                          ext_ref, gate0, gate1, val0, val1, *, seq, nj):
    row0 = (pl.program_id(0) + 1) * h_ref.shape[0]
    j = pl.program_id(1)
    gate_s, val_s = (gate0, gate1), (val0, val1)

    pl.when(j == 0)(lambda: _build_halo_tile(ext_ref, h_ref, hp_ref, hn_ref))

    def stage(p, do_matmul, do_epilogue):
        q = 1 - p
        width = o_ref.shape[1] // TAIL_SUBTILES
        for t in range(TAIL_SUBTILES):
            cols = pl.ds(t * width, width)
            acc = None
            for c in range(PIPELINE_CHUNKS):
                if do_matmul:
                    acc = _conv_glu_matmul_chunk(ext_ref, wg_ref.at[:, cols], wv_ref.at[:, cols], c, acc)
                if do_epilogue:
                    _conv_glu_epilogue_chunk(gate_s[q].at[:, cols], val_s[q].at[:, cols], cw_ref.at[:, cols],
                                             cb_ref.at[:, cols], o_ref.at[:, cols], c, row0, seq)
            if do_matmul:
                gate_s[p][:, cols], val_s[p][:, cols] = acc

    pl.when(j == 0)(lambda: stage(0, True, False))
    steady = jnp.logical_and(j >= 1, j < nj)
    pl.when(jnp.logical_and(steady, j % 2 == 0))(lambda: stage(0, True, True))
    pl.when(jnp.logical_and(steady, j % 2 == 1))(lambda: stage(1, True, True))
    pl.when(j == nj)(lambda: stage(nj % 2, False, True))


def conv_glu_up(h, w_up, l, conv_w, conv_b, seq, buf, bm=1024, bn_head=256, bn=256 * TAIL_SUBTILES):
    m, k = h.shape
    dff = w_up.shape[2] // 2
    assert buf.shape == (m, dff) and buf.dtype == BF16
    halo = BF16_SUBLANES
    rb = bm // halo
    last = m // halo - 1
    conv_b = conv_b.reshape(1, dff)
    assert bm == HEAD_ROWS
    prev_tile = lambda i: jnp.maximum(i * rb - 1, 0)
    next_tile = lambda i: jnp.minimum((i + 1) * rb, last)
    ext = pltpu.VMEM((bm + 2 * halo, k), BF16)

    nj = dff // bn_head
    assert nj >= 2
    w_tile = lambda j: jnp.minimum(j, nj - 1)
    o_tile = lambda j: jnp.clip(j - 2, 0, nj - 1)
    wb = pltpu.VMEM((k, bn_head), BF16)
    gate = pltpu.VMEM((bm + 2 * halo, bn_head), F32)
    val = pltpu.VMEM((bm, bn_head), F32)
    wb_out = jax.ShapeDtypeStruct((k, dff), BF16)
    buf, wgb, wvb = pl.pallas_call(
        functools.partial(_conv_glu_head_kernel, seq=seq, nj=nj),
        grid=(nj + 2,),
        in_specs=[_resident((bm, k), lambda j: (0, 0)),
                  _resident((halo, k), lambda j: (prev_tile(0), 0)),
                  _resident((halo, k), lambda j: (next_tile(0), 0)),
                  pl.BlockSpec((None, k, bn_head), lambda j: (l, 0, w_tile(j))),
                  pl.BlockSpec((None, k, bn_head), lambda j: (l, 0, w_tile(j) + nj)),
                  pl.BlockSpec((CONV_WIDTH, bn_head), lambda j: (0, o_tile(j))),
                  pl.BlockSpec((1, bn_head), lambda j: (0, o_tile(j))),
                  _ALIASED],
        out_specs=[pl.BlockSpec((bm, bn_head), lambda j: (0, o_tile(j))),
                   pl.BlockSpec((k, bn_head), lambda j: (0, w_tile(j))),
                   pl.BlockSpec((k, bn_head), lambda j: (0, w_tile(j)))],
        out_shape=[jax.ShapeDtypeStruct((m, dff), BF16), wb_out, wb_out],
        input_output_aliases={7: 0},
        scratch_shapes=[ext, wb, wb, wb, wb, gate, gate, val, val],
        compiler_params=_params(56, ("arbitrary",)),
        name="conv_glu_up_head",
    )(h, h, h, w_up, w_up, conv_w, conv_b, buf)

    nj = dff // bn
    w_tile = lambda j: jnp.minimum(j, nj - 1)
    o_tile = lambda j: jnp.clip(j - 1, 0, nj - 1)
    gate = pltpu.VMEM((bm + 2 * halo, bn), F32)
    val = pltpu.VMEM((bm, bn), F32)
    return pl.pallas_call(
        functools.partial(_conv_glu_tail_kernel, seq=seq, nj=nj),
        grid=(m // bm - 1, nj + 1),
        in_specs=[_resident((bm, k), lambda i, j: (i + 1, 0)),
                  _resident((halo, k), lambda i, j: (prev_tile(i + 1), 0)),
                  _resident((halo, k), lambda i, j: (next_tile(i + 1), 0)),
                  pl.BlockSpec((k, bn), lambda i, j: (0, w_tile(j))),
                  pl.BlockSpec((k, bn), lambda i, j: (0, w_tile(j))),
                  pl.BlockSpec((CONV_WIDTH, bn), lambda i, j: (0, o_tile(j))),
                  pl.BlockSpec((1, bn), lambda i, j: (0, o_tile(j))),
                  _ALIASED],
        out_specs=pl.BlockSpec((bm, bn), lambda i, j: (i + 1, o_tile(j))),
        out_shape=jax.ShapeDtypeStruct((m, dff), BF16),
        input_output_aliases={7: 0},
        scratch_shapes=[ext, gate, gate, val, val],
        compiler_params=_params(56, ("parallel", "arbitrary")),
        name="conv_glu_up",
    )(h, h, h, wgb, wvb, conv_w, conv_b, buf)


def kernel(x, norm1_gain, w_in, w_fourier_out, lambdas, subln_gain, rel_bias_table, w_attn_out,
           w_gate, b_gate, w_o, norm2_gain, w_up, conv_w, conv_b, w_down, final_norm_gain):
    batch, seq, d_model = x.shape
    depth = w_in.shape[0]
    n_heads = rel_bias_table.shape[1]
    d_attn = n_heads * HEAD_WIDTH
    d_in = w_in.shape[2]
    tq = 512

    cs, sn, chan = _dft_constants(seq, FOURIER_GROUP_DIM)
    col = np.ones((1, d_in), np.float32)
    col[:, D_FOURIER:D_FOURIER + d_attn] = ATTN_HEAD_DIM ** -0.5 * LOG2_E
    col_scale = jnp.asarray(col)
    wb = bias_window(rel_bias_table, seq, tq)

    x = x.reshape(batch * seq, d_model)
    m = batch * seq
    u = jnp.zeros((m, d_in), BF16)
    mixed = jnp.zeros((m, d_model), BF16)
    act = jnp.zeros((m, w_up.shape[2] // 2), BF16)
    for l in range(depth):
        lam_init = 0.8 - 0.6 * math.exp(-0.3 * l)
        h = rmsnorm(x, norm1_gain[l], BF16)
        u = in_proj(h, w_in, l, col_scale, u)
        t1, t2 = fourier_chan(u, chan)
        f = fourier_seq(cs, sn, t1, t2, seq)
        a = diff_attention(u, wb, lambdas[l], subln_gain[l], lam_init, seq, n_heads, tq)
        mixed = gated_mix(h, f, a, w_gate, w_fourier_out, w_attn_out, l, b_gate[l], mixed)
        x = proj_residual(mixed, w_o, l, x, head_rows=2 * HEAD_ROWS, bn_head=256, bm=1024, bn=1024)
        h2 = rmsnorm(x, norm2_gain[l], BF16)
        act = conv_glu_up(h2, w_up, l, conv_w[l], conv_b[l], seq, act)
        x = proj_residual(act, w_down, l, x, head_rows=HEAD_ROWS, bn_head=256, bm=1024, bn=256)
    out = rmsnorm(x, final_norm_gain, F32)
    return out.reshape(batch, seq, d_model)
```

```python
import functools
import math

import numpy as np
import jax
import jax.numpy as jnp
from jax import lax
from jax.experimental import pallas as pl
from jax.experimental.pallas import tpu as pltpu

F32 = jnp.float32
BF16 = jnp.bfloat16

EPS = 1e-6
N_FOURIER_GROUPS = 4
FOURIER_GROUP_DIM = 256
D_FOURIER = N_FOURIER_GROUPS * FOURIER_GROUP_DIM
ATTN_HEAD_DIM = 128
HEAD_WIDTH = 2 * ATTN_HEAD_DIM
N_REL_BUCKETS = 32
REL_MAX_DISTANCE = 128
LOG2_E = math.log2(math.e)
ATTN_KEY_CHUNK = 512
ATTN_FINISH_LEAD = 6
CONV_WIDTH = 3
LANES = 128
BF16_SUBLANES = 16
F32_SUBLANES = 8
PIPELINE_CHUNKS = 16
MXU_DEPTH = 256
EPILOGUE_SPLIT = 2
TAIL_SUBTILES = 2

MIB = 1024 * 1024


def _params(vmem_mib, semantics):
    return pltpu.CompilerParams(dimension_semantics=semantics, vmem_limit_bytes=vmem_mib * MIB)


def _resident(shape, index_map):
    return pl.BlockSpec(shape, index_map, pipeline_mode=pl.Buffered(1))


def _rmsnorm_kernel(x_ref, g_ref, o_ref):
    x = x_ref[...]
    ms = jnp.mean(x * x, axis=-1, keepdims=True)
    o_ref[...] = (x * lax.rsqrt(ms + EPS) * g_ref[...]).astype(o_ref.dtype)


def rmsnorm(x, gain, out_dtype, bm=512):
    m, d = x.shape
    return pl.pallas_call(
        _rmsnorm_kernel,
        grid=(m // bm,),
        in_specs=[pl.BlockSpec((bm, d), lambda i: (i, 0)),
                  pl.BlockSpec((1, d), lambda i: (0, 0))],
        out_specs=pl.BlockSpec((bm, d), lambda i: (i, 0)),
        out_shape=jax.ShapeDtypeStruct((m, d), out_dtype),
        compiler_params=_params(40, ("parallel",)),
        name="rmsnorm",
    )(x, gain.reshape(1, d))


HEAD_ROWS = 1024
_ALIASED = pl.BlockSpec(memory_space=pl.ANY)


def _cast_dot(lhs_ref, w_ref, wb_ref):
    kc = MXU_DEPTH
    acc = None
    for c in range(w_ref.shape[0] // kc):
        ks = slice(c * kc, (c + 1) * kc)
        wb_ref[ks, :] = w_ref[ks, :].astype(BF16)
        part = jnp.dot(lhs_ref[:, ks], wb_ref[ks, :], preferred_element_type=F32)
        acc = part if c == 0 else acc + part
    return acc


def _in_proj_head_kernel(h_ref, w_ref, s_ref, buf_ref, o_ref, wb_ref):
    o_ref[...] = (_cast_dot(h_ref, w_ref, wb_ref) * s_ref[...]).astype(o_ref.dtype)


def _in_proj_tail_kernel(h_ref, wb_ref, s_ref, buf_ref, o_ref):
    acc = jnp.dot(h_ref[...], wb_ref[...], preferred_element_type=F32)
    o_ref[...] = (acc * s_ref[...]).astype(o_ref.dtype)


def in_proj(h, w, l, col_scale, buf, head_rows=2 * HEAD_ROWS, bm=1024, bn_head=512, bn=1024):
    m, k = h.shape
    n = w.shape[2]
    assert buf.shape == (m, n) and buf.dtype == BF16
    buf, wb = pl.pallas_call(
        _in_proj_head_kernel,
        grid=(n // bn_head,),
        in_specs=[_resident((head_rows, k), lambda j: (0, 0)),
                  pl.BlockSpec((None, k, bn_head), lambda j: (l, 0, j)),
                  pl.BlockSpec((1, bn_head), lambda j: (0, j)),
                  _ALIASED],
        out_specs=[pl.BlockSpec((head_rows, bn_head), lambda j: (0, j)),
                   pl.BlockSpec((k, bn_head), lambda j: (0, j))],
        out_shape=[jax.ShapeDtypeStruct((m, n), BF16), jax.ShapeDtypeStruct((k, n), BF16)],
        input_output_aliases={3: 0},
        compiler_params=_params(48, ("arbitrary",)),
        name="in_proj_head",
    )(h, w, col_scale, buf)
    skip = head_rows // bm
    return pl.pallas_call(
        _in_proj_tail_kernel,
        grid=(m // bm - skip, n // bn),
        in_specs=[pl.BlockSpec((bm, k), lambda i, j: (i + skip, 0)),
                  pl.BlockSpec((k, bn), lambda i, j: (0, j)),
                  pl.BlockSpec((1, bn), lambda i, j: (0, j)),
                  _ALIASED],
        out_specs=pl.BlockSpec((bm, bn), lambda i, j: (i + skip, j)),
        out_shape=jax.ShapeDtypeStruct((m, n), BF16),
        input_output_aliases={3: 0},
        compiler_params=_params(48, ("parallel", "arbitrary")),
        name="in_proj",
    )(h, wb, col_scale, buf)


def _dft_constants(seq, dim):
    def cos_sin(n):
        idx = np.arange(n, dtype=np.int64)
        ang = 2.0 * np.pi * ((idx[:, None] * idx[None, :]) % n).astype(np.float64) / n
        return np.cos(ang), np.sin(ang)
    cs, ss = cos_sin(seq)
    cc, sc = cos_sin(dim)
    chan = np.concatenate([cc, sc], axis=1)
    return (jnp.asarray(cs, BF16), jnp.asarray(-ss, BF16), jnp.asarray(chan, BF16))


def _fourier_kernel(u_ref, c_ref, cs_ref, sn_ref, o_ref, t1_ref, t2_ref, *, scale):
    gd = FOURIER_GROUP_DIM

    @pl.when(pl.program_id(1) == 0)
    def _():
        c = c_ref[...]
        for g in range(N_FOURIER_GROUPS):
            t = jnp.dot(u_ref[:, g * gd:(g + 1) * gd], c, preferred_element_type=F32)
            t1_ref[:, g * gd:(g + 1) * gd] = t[:, :gd].astype(t1_ref.dtype)
            t2_ref[:, g * gd:(g + 1) * gd] = t[:, gd:].astype(t2_ref.dtype)

    acc = jnp.dot(cs_ref[...], t1_ref[...], preferred_element_type=F32)
    acc += jnp.dot(sn_ref[...], t2_ref[...], preferred_element_type=F32)
    o_ref[...] = (acc * scale).astype(o_ref.dtype)


def fourier_mix(u, chan, cs, sn, seq, bm=512):
    m = u.shape[0]
    n = D_FOURIER
    gd = FOURIER_GROUP_DIM
    nb = m // seq
    ni = seq // bm
    scale = 1.0 / math.sqrt(seq * gd)
    spectrum = pltpu.VMEM((seq, n), BF16)
    return pl.pallas_call(
        functools.partial(_fourier_kernel, scale=scale),
        grid=(nb, ni),
        in_specs=[pl.BlockSpec((seq, n), lambda b, i: (b, 0)),
                  pl.BlockSpec((gd, 2 * gd), lambda b, i: (0, 0)),
                  pl.BlockSpec((bm, seq), lambda b, i: (i, 0)),
                  pl.BlockSpec((bm, seq), lambda b, i: (i, 0))],
        out_specs=pl.BlockSpec((bm, n), lambda b, i: (b * ni + i, 0)),
        out_shape=jax.ShapeDtypeStruct((m, n), BF16),
        scratch_shapes=[spectrum, spectrum],
        compiler_params=_params(40, ("parallel", "arbitrary")),
        name="fourier_mix",
    )(u, chan, cs, sn)


def _bias_window_kernel(tab_ref, o_ref, *, tq, nq):
    h = pl.program_id(0)
    half = N_REL_BUCKETS // 2
    max_exact = half // 2
    width = o_ref.shape[2]
    lo = (nq - 1) * tq - REL_MAX_DISTANCE
    hi = nq * tq + REL_MAX_DISTANCE
    o_ref[0, :, :lo] = jnp.full((tq, lo), tab_ref[half - 1, h] * LOG2_E, F32)
    o_ref[0, :, hi:] = jnp.full((tq, width - hi), tab_ref[N_REL_BUCKETS - 1, h] * LOG2_E, F32)

    shape = (tq, hi - lo)
    i = lax.broadcasted_iota(jnp.int32, shape, 0)
    c = lax.broadcasted_iota(jnp.int32, shape, 1) + lo
    n = i + (nq - 1) * tq - c
    ret = jnp.where(n < 0, half, 0)
    n = jnp.abs(n)
    n2 = n * n
    large = jnp.full(shape, max_exact, jnp.int32)
    for j in range(1, half - max_exact):
        large += (n2 >= (max_exact * max_exact) * (2 ** j)).astype(jnp.int32)
    bucket = ret + jnp.where(n < max_exact, n, large)
    acc = jnp.zeros(shape, F32)
    for b in range(N_REL_BUCKETS):
        acc = jnp.where(bucket == b, tab_ref[b, h] * LOG2_E, acc)
    o_ref[0, :, lo:hi] = acc


def bias_window(rel_table, seq, tq):
    nq = seq // tq
    nh = rel_table.shape[1]
    width = 2 * seq - tq
    assert tq >= REL_MAX_DISTANCE and nq >= 2
    return pl.pallas_call(
        functools.partial(_bias_window_kernel, tq=tq, nq=nq),
        grid=(nh,),
        in_specs=[pl.BlockSpec(memory_space=pltpu.SMEM)],
        out_specs=pl.BlockSpec((1, tq, width), lambda h: (h, 0, 0)),
        out_shape=jax.ShapeDtypeStruct((nh, tq, width), F32),
        compiler_params=_params(40, ("parallel",)),
        name="bias_window",
    )(rel_table)


def _lane_tile_reduce(x, op):
    out = x[:, 0:LANES]
    for t in range(1, x.shape[1] // LANES):
        out = op(out, x[:, t * LANES:(t + 1) * LANES])
    return out


def _attn_kernel(q_ref, k_ref, v_ref, wb_ref, lam_ref, sg_ref, o_ref, s0_ref, s1_ref, mx0_ref, mx1_ref,
                 *, lam_init, tq, nq, seq):
    i = pl.program_id(2)
    d = ATTN_HEAD_DIM
    ck = ATTN_KEY_CHUNK
    nc = seq // ck
    s_scr = (s0_ref, s1_ref)
    mx_scr = (mx0_ref, mx1_ref)
    chunks = [(m, c) for m in range(2) for c in range(nc)]

    def logits_chunk(p, m, c, run_max):
        off = pl.multiple_of((nq - 1 - i) * tq + c * ck, tq)
        s = lax.dot_general(q_ref[:, m * d:(m + 1) * d], k_ref[c * ck:(c + 1) * ck, m * d:(m + 1) * d],
                            (((1,), (1,)), ((), ())), preferred_element_type=F32)
        s = s + wb_ref[0, :, pl.ds(off, ck)]
        s_scr[p][m, :, c * ck:(c + 1) * ck] = s
        part = _lane_tile_reduce(s, jnp.maximum)
        run_max = part if c == 0 else jnp.maximum(run_max, part)
        if c == nc - 1:
            mx_scr[p][m] = jnp.broadcast_to(jnp.max(run_max, axis=-1, keepdims=True), run_max.shape)
        return run_max

    def probs_chunk(p, m, c, state):
        row_max, row_sum, out = state
        if c == 0:
            row_max = jnp.concatenate([mx_scr[p][m]] * (ck // LANES), axis=1)
        e = jnp.exp2(s_scr[p][m, :, c * ck:(c + 1) * ck] - row_max)
        part = _lane_tile_reduce(e, jnp.add)
        pv = jnp.dot(e.astype(BF16), v_ref[c * ck:(c + 1) * ck, :], preferred_element_type=F32)
        return (row_max, part if c == 0 else row_sum + part, pv if c == 0 else out + pv)

    def finish(results):
        (_, sum0, out0), (_, sum1, out1) = results
        lam_p = lam_ref[...]
        lam = (jnp.exp(jnp.sum(lam_p[0:1] * lam_p[1:2], axis=-1, keepdims=True))
               - jnp.exp(jnp.sum(lam_p[2:3] * lam_p[3:4], axis=-1, keepdims=True)) + lam_init)
        l0 = jnp.sum(sum0, axis=-1, keepdims=True)
        l1 = jnp.sum(sum1, axis=-1, keepdims=True)
        o = out0 * (1.0 / l0) - out1 * (lam / l1)
        ms = jnp.mean(o * o, axis=-1, keepdims=True)
        o = o * lax.rsqrt(ms + EPS) * sg_ref[...] * (1.0 - lam_init)
        o_ref[...] = o.astype(o_ref.dtype)

    def stage(p, do_logits, do_probs):
        lead = ATTN_FINISH_LEAD if do_logits and do_probs else 0
        run_max = None
        state = (None, None, None)
        results = []
        for idx in range(len(chunks) + lead):
            if do_probs and idx < len(chunks):
                m, c = chunks[idx]
                state = probs_chunk(1 - p, m, c, state)
                if c == nc - 1:
                    results.append(state)
                if idx == len(chunks) - 1:
                    finish(results)
            if do_logits and idx >= lead:
                m, c = chunks[idx - lead]
                run_max = logits_chunk(p, m, c, run_max)

    pl.when(i == 0)(lambda: stage(0, True, False))
    steady = jnp.logical_and(i >= 1, i < nq)
    pl.when(jnp.logical_and(steady, i % 2 == 0))(lambda: stage(0, True, True))
    pl.when(jnp.logical_and(steady, i % 2 == 1))(lambda: stage(1, True, True))
    pl.when(i == nq)(lambda: stage(nq % 2, False, True))


def diff_attention(u, wb, lambdas, sub_gain, lam_init, seq, n_heads, tq):
    m = u.shape[0]
    nb = m // seq
    nq = seq // tq
    hw = HEAD_WIDTH
    q_blk = D_FOURIER // hw
    k_blk = q_blk + n_heads
    v_blk = k_blk + n_heads
    assert ATTN_KEY_CHUNK % tq == 0 and seq % ATTN_KEY_CHUNK == 0
    logits = pltpu.VMEM((2, tq, seq), F32)
    maxima = pltpu.VMEM((2, tq, LANES), F32)
    return pl.pallas_call(
        functools.partial(_attn_kernel, lam_init=lam_init, tq=tq, nq=nq, seq=seq),
        grid=(nb, n_heads, nq + 1),
        in_specs=[pl.BlockSpec((tq, hw), lambda b, h, i: (b * nq + jnp.minimum(i, nq - 1), q_blk + h)),
                  pl.BlockSpec((seq, hw), lambda b, h, i: (b, k_blk + h)),
                  pl.BlockSpec((seq, hw), lambda b, h, i: (b, v_blk + h)),
                  pl.BlockSpec((1, tq, 2 * seq - tq), lambda b, h, i: (h, 0, 0)),
                  pl.BlockSpec((4, ATTN_HEAD_DIM), lambda b, h, i: (0, 0)),
                  pl.BlockSpec((1, hw), lambda b, h, i: (0, 0))],
        out_specs=pl.BlockSpec((tq, hw), lambda b, h, i: (b * nq + jnp.maximum(i - 1, 0), h)),
        out_shape=jax.ShapeDtypeStruct((m, n_heads * hw), BF16),
        scratch_shapes=[logits, logits, maxima, maxima],
        compiler_params=_params(48, ("parallel", "parallel", "arbitrary")),
        name="diff_attention",
    )(u, u, u, wb, lambdas, sub_gain.reshape(1, hw))


def _mix_head_kernel(h_ref, f_ref, a_ref, wg0_ref, wg1_ref, wf_ref, wa_ref, b0_ref, b1_ref, buf_ref,
                     o_ref, wg0b_ref, wg1b_ref, wfb_ref, wab_ref):
    g0 = jax.nn.sigmoid(_cast_dot(h_ref, wg0_ref, wg0b_ref) + b0_ref[...])
    t0 = g0 * _cast_dot(f_ref, wf_ref, wfb_ref)
    g1 = jax.nn.sigmoid(_cast_dot(h_ref, wg1_ref, wg1b_ref) + b1_ref[...])
    o_ref[...] = (t0 + g1 * _cast_dot(a_ref, wa_ref, wab_ref)).astype(o_ref.dtype)


def _mix_tail_kernel(h_ref, f_ref, a_ref, wg0_ref, wg1_ref, wf_ref, wa_ref, b0_ref, b1_ref, buf_ref, o_ref):
    h = h_ref[...]
    dot = functools.partial(jnp.dot, preferred_element_type=F32)
    g0 = jax.nn.sigmoid(dot(h, wg0_ref[...]) + b0_ref[...])
    t0 = g0 * dot(f_ref[...], wf_ref[...])
    g1 = jax.nn.sigmoid(dot(h, wg1_ref[...]) + b1_ref[...])
    o_ref[...] = (t0 + g1 * dot(a_ref[...], wa_ref[...])).astype(o_ref.dtype)


def gated_mix(h, f, a, w_gate, w_f, w_a, l, b_gate, buf, bm=1024, bn=256):
    m, d = h.shape
    df, da = f.shape[1], a.shape[1]
    nj = d // bn
    b_gate = b_gate.reshape(1, 2 * d)
    assert buf.shape == (m, d) and buf.dtype == BF16
    buf, wg0b, wg1b, wfb, wab = pl.pallas_call(
        _mix_head_kernel,
        grid=(nj,),
        in_specs=[_resident((HEAD_ROWS, d), lambda j: (0, 0)),
                  _resident((HEAD_ROWS, df), lambda j: (0, 0)),
                  _resident((HEAD_ROWS, da), lambda j: (0, 0)),
                  pl.BlockSpec((None, d, bn), lambda j: (l, 0, j)),
                  pl.BlockSpec((None, d, bn), lambda j: (l, 0, j + nj)),
                  pl.BlockSpec((None, df, bn), lambda j: (l, 0, j)),
                  pl.BlockSpec((None, da, bn), lambda j: (l, 0, j)),
                  pl.BlockSpec((1, bn), lambda j: (0, j)),
                  pl.BlockSpec((1, bn), lambda j: (0, j + nj)),
                  _ALIASED],
        out_specs=[pl.BlockSpec((HEAD_ROWS, bn), lambda j: (0, j)),
                   pl.BlockSpec((d, bn), lambda j: (0, j)),
                   pl.BlockSpec((d, bn), lambda j: (0, j)),
                   pl.BlockSpec((df, bn), lambda j: (0, j)),
                   pl.BlockSpec((da, bn), lambda j: (0, j))],
        out_shape=[jax.ShapeDtypeStruct((m, d), BF16), jax.ShapeDtypeStruct((d, d), BF16),
                   jax.ShapeDtypeStruct((d, d), BF16), jax.ShapeDtypeStruct((df, d), BF16),
                   jax.ShapeDtypeStruct((da, d), BF16)],
        input_output_aliases={9: 0},
        compiler_params=_params(56, ("arbitrary",)),
        name="gated_mix_head",
    )(h, f, a, w_gate, w_gate, w_f, w_a, b_gate, b_gate, buf)
    assert bm == HEAD_ROWS
    wcol = lambda i, j: (0, j)
    return pl.pallas_call(
        _mix_tail_kernel,
        grid=(m // bm - 1, nj),
        in_specs=[pl.BlockSpec((bm, d), lambda i, j: (i + 1, 0)),
                  pl.BlockSpec((bm, df), lambda i, j: (i + 1, 0)),
                  pl.BlockSpec((bm, da), lambda i, j: (i + 1, 0)),
                  pl.BlockSpec((d, bn), wcol),
                  pl.BlockSpec((d, bn), wcol),
                  pl.BlockSpec((df, bn), wcol),
                  pl.BlockSpec((da, bn), wcol),
                  pl.BlockSpec((1, bn), lambda i, j: (0, j)),
                  pl.BlockSpec((1, bn), lambda i, j: (0, j + nj)),
                  _ALIASED],
        out_specs=pl.BlockSpec((bm, bn), lambda i, j: (i + 1, j)),
        out_shape=jax.ShapeDtypeStruct((m, d), BF16),
        input_output_aliases={9: 0},
        compiler_params=_params(56, ("parallel", "arbitrary")),
        name="gated_mix",
    )(h, f, a, wg0b, wg1b, wfb, wab, b_gate, b_gate, buf)


def _proj_residual_head_kernel(a_ref, w_ref, r_ref, o_ref, wb_ref):
    o_ref[...] = r_ref[...] + _cast_dot(a_ref, w_ref, wb_ref)


def _proj_residual_tail_kernel(a_ref, wb_ref, r_ref, o_ref):
    o_ref[...] = r_ref[...] + jnp.dot(a_ref[...], wb_ref[...], preferred_element_type=F32)


def proj_residual(a, w, l, res, head_rows, bn_head, bm, bn):
    m, k = a.shape
    n = w.shape[2]
    res, wb = pl.pallas_call(
        _proj_residual_head_kernel,
        grid=(n // bn_head,),
        in_specs=[_resident((head_rows, k), lambda j: (0, 0)),
                  pl.BlockSpec((None, k, bn_head), lambda j: (l, 0, j)),
                  pl.BlockSpec((head_rows, bn_head), lambda j: (0, j))],
        out_specs=[pl.BlockSpec((head_rows, bn_head), lambda j: (0, j)),
                   pl.BlockSpec((k, bn_head), lambda j: (0, j))],
        out_shape=[jax.ShapeDtypeStruct((m, n), F32), jax.ShapeDtypeStruct((k, n), BF16)],
        input_output_aliases={2: 0},
        compiler_params=_params(56, ("arbitrary",)),
        name="proj_residual_head",
    )(a, w, res)
    skip = head_rows // bm
    return pl.pallas_call(
        _proj_residual_tail_kernel,
        grid=(m // bm - skip, n // bn),
        in_specs=[pl.BlockSpec((bm, k), lambda i, j: (i + skip, 0)),
                  pl.BlockSpec((k, bn), lambda i, j: (0, j)),
                  pl.BlockSpec((bm, bn), lambda i, j: (i + skip, j))],
        out_specs=pl.BlockSpec((bm, bn), lambda i, j: (i + skip, j)),
        out_shape=jax.ShapeDtypeStruct((m, n), F32),
        input_output_aliases={2: 0},
        compiler_params=_params(56, ("parallel", "arbitrary")),
        name="proj_residual",
    )(a, wb, res)


def _gelu_exact(x):
    return 0.5 * x * (1.0 + lax.erf(x * math.sqrt(0.5)))


def _build_halo_tile(ext_ref, h_ref, hp_ref, hn_ref):
    bm = h_ref.shape[0]
    halo = BF16_SUBLANES
    ext_ref[0:halo, :] = hp_ref[...]
    ext_ref[halo:halo + bm, :] = h_ref[...]
    ext_ref[halo + bm:, :] = hn_ref[...]


def _conv_glu_matmul_chunk(ext_ref, wg_ref, wv_ref, c, acc):
    halo = BF16_SUBLANES
    bm = ext_ref.shape[0] - 2 * halo
    kc = ext_ref.shape[1] // PIPELINE_CHUNKS
    ks = slice(c * kc, (c + 1) * kc)
    gd = jnp.dot(ext_ref[:, ks], wg_ref[ks, :], preferred_element_type=F32)
    vd = jnp.dot(ext_ref[halo:halo + bm, ks], wv_ref[ks, :], preferred_element_type=F32)
    return (gd, vd) if c == 0 else (acc[0] + gd, acc[1] + vd)


def _conv_glu_epilogue_chunk(gate_ref, val_ref, cw_ref, cb_ref, o_ref, c, row0, seq):
    halo = BF16_SUBLANES
    pad = F32_SUBLANES
    bm = val_ref.shape[0]
    rc = bm // PIPELINE_CHUNKS // EPILOGUE_SPLIT
    cw = cw_ref[...]
    tiles_align = seq % bm == 0
    for part in range(EPILOGUE_SPLIT):
        first = (c * EPILOGUE_SPLIT + part) * rc
        r0 = halo + first
        g = gate_ref[r0 - pad:r0 + rc + pad, :]
        g_prev = pltpu.roll(g, 1, axis=0)[pad:pad + rc]
        g_next = pltpu.roll(g, rc + 2 * pad - 1, axis=0)[pad:pad + rc]
        pos = (row0 + first + lax.broadcasted_iota(jnp.int32, (rc, 1), 0)) % seq
        if not tiles_align or first == 0:
            g_prev = jnp.where(pos == 0, 0.0, g_prev)
        if not tiles_align or first + rc == bm:
            g_next = jnp.where(pos == seq - 1, 0.0, g_next)
        conv = g_prev * cw[0:1] + g[pad:pad + rc] * cw[1:2] + g_next * cw[2:3] + cb_ref[...]
        rs = slice(first, first + rc)
        o_ref[rs, :] = (_gelu_exact(conv) * val_ref[rs, :]).astype(o_ref.dtype)


def _conv_glu_head_kernel(h_ref, hp_ref, hn_ref, wg_ref, wv_ref, cw_ref, cb_ref, buf_ref,
                          o_ref, wgo_ref, wvo_ref, ext_ref, wgb0, wgb1, wvb0, wvb1, gate0, gate1, val0, val1, *, seq, nj):
    j = pl.program_id(0)
    kc = h_ref.shape[1] // PIPELINE_CHUNKS
    wgb, wvb, gate_s, val_s = (wgb0, wgb1), (wvb0, wvb1), (gate0, gate1), (val0, val1)

    pl.when(j == 0)(lambda: _build_halo_tile(ext_ref, h_ref, hp_ref, hn_ref))

    def stage(p, do_cast, do_matmul, do_epilogue):
        q = 1 - p
        acc = None
        for c in range(PIPELINE_CHUNKS):
            if do_matmul:
                acc = _conv_glu_matmul_chunk(ext_ref, wgb[q], wvb[q], c, acc)
            if do_cast:
                ks = slice(c * kc, (c + 1) * kc)
                wg = wg_ref[ks, :].astype(BF16)
                wv = wv_ref[ks, :].astype(BF16)
                wgb[p][ks, :] = wg
                wvb[p][ks, :] = wv
                wgo_ref[ks, :] = wg
                wvo_ref[ks, :] = wv
            if do_epilogue:
                _conv_glu_epilogue_chunk(gate_s[p], val_s[p], cw_ref, cb_ref, o_ref, c, 0, seq)
        if do_matmul:
            gate_s[q][...], val_s[q][...] = acc

    pl.when(j == 0)(lambda: stage(0, True, False, False))
    pl.when(j == 1)(lambda: stage(1, True, True, False))
    steady = jnp.logical_and(j >= 2, j < nj)
    pl.when(jnp.logical_and(steady, j % 2 == 0))(lambda: stage(0, True, True, True))
    pl.when(jnp.logical_and(steady, j % 2 == 1))(lambda: stage(1, True, True, True))
    pl.when(j == nj)(lambda: stage(nj % 2, False, True, True))
    pl.when(j == nj + 1)(lambda: stage((nj + 1) % 2, False, False, True))


def _conv_glu_tail_kernel(h_ref, hp_ref, hn_ref, wg_ref, wv_ref, cw_ref, cb_ref, buf_ref, o_ref,
                          ext_ref, gate0, gate1, val0, val1, *, seq, nj):
    row0 = (pl.program_id(0) + 1) * h_ref.shape[0]
    j = pl.program_id(1)
    gate_s, val_s = (gate0, gate1), (val0, val1)

    pl.when(j == 0)(lambda: _build_halo_tile(ext_ref, h_ref, hp_ref, hn_ref))

    def stage(p, do_matmul, do_epilogue):
        q = 1 - p
        width = o_ref.shape[1] // TAIL_SUBTILES
        for t in range(TAIL_SUBTILES):
            cols = pl.ds(t * width, width)
            acc = None
            for c in range(PIPELINE_CHUNKS):
                if do_matmul:
                    acc = _conv_glu_matmul_chunk(ext_ref, wg_ref.at[:, cols], wv_ref.at[:, cols], c, acc)
                if do_epilogue:
                    _conv_glu_epilogue_chunk(gate_s[q].at[:, cols], val_s[q].at[:, cols], cw_ref.at[:, cols],
                                             cb_ref.at[:, cols], o_ref.at[:, cols], c, row0, seq)
            if do_matmul:
                gate_s[p][:, cols], val_s[p][:, cols] = acc

    pl.when(j == 0)(lambda: stage(0, True, False))
    steady = jnp.logical_and(j >= 1, j < nj)
    pl.when(jnp.logical_and(steady, j % 2 == 0))(lambda: stage(0, True, True))
    pl.when(jnp.logical_and(steady, j % 2 == 1))(lambda: stage(1, True, True))
    pl.when(j == nj)(lambda: stage(nj % 2, False, True))


def conv_glu_up(h, w_up, l, conv_w, conv_b, seq, buf, bm=1024, bn_head=256, bn=256 * TAIL_SUBTILES):
    m, k = h.shape
    dff = w_up.shape[2] // 2
    assert buf.shape == (m, dff) and buf.dtype == BF16
    halo = BF16_SUBLANES
    rb = bm // halo
    last = m // halo - 1
    conv_b = conv_b.reshape(1, dff)
    assert bm == HEAD_ROWS
    prev_tile = lambda i: jnp.maximum(i * rb - 1, 0)
    next_tile = lambda i: jnp.minimum((i + 1) * rb, last)
    ext = pltpu.VMEM((bm + 2 * halo, k), BF16)

    nj = dff // bn_head
    assert nj >= 2
    w_tile = lambda j: jnp.minimum(j, nj - 1)
    o_tile = lambda j: jnp.clip(j - 2, 0, nj - 1)
    wb = pltpu.VMEM((k, bn_head), BF16)
    gate = pltpu.VMEM((bm + 2 * halo, bn_head), F32)
    val = pltpu.VMEM((bm, bn_head), F32)
    wb_out = jax.ShapeDtypeStruct((k, dff), BF16)
    buf, wgb, wvb = pl.pallas_call(
        functools.partial(_conv_glu_head_kernel, seq=seq, nj=nj),
        grid=(nj + 2,),
        in_specs=[_resident((bm, k), lambda j: (0, 0)),
                  _resident((halo, k), lambda j: (prev_tile(0), 0)),
                  _resident((halo, k), lambda j: (next_tile(0), 0)),
                  pl.BlockSpec((None, k, bn_head), lambda j: (l, 0, w_tile(j))),
                  pl.BlockSpec((None, k, bn_head), lambda j: (l, 0, w_tile(j) + nj)),
                  pl.BlockSpec((CONV_WIDTH, bn_head), lambda j: (0, o_tile(j))),
                  pl.BlockSpec((1, bn_head), lambda j: (0, o_tile(j))),
                  _ALIASED],
        out_specs=[pl.BlockSpec((bm, bn_head), lambda j: (0, o_tile(j))),
                   pl.BlockSpec((k, bn_head), lambda j: (0, w_tile(j))),
                   pl.BlockSpec((k, bn_head), lambda j: (0, w_tile(j)))],
        out_shape=[jax.ShapeDtypeStruct((m, dff), BF16), wb_out, wb_out],
        input_output_aliases={7: 0},
        scratch_shapes=[ext, wb, wb, wb, wb, gate, gate, val, val],
        compiler_params=_params(56, ("arbitrary",)),
        name="conv_glu_up_head",
    )(h, h, h, w_up, w_up, conv_w, conv_b, buf)

    nj = dff // bn
    w_tile = lambda j: jnp.minimum(j, nj - 1)
    o_tile = lambda j: jnp.clip(j - 1, 0, nj - 1)
    gate = pltpu.VMEM((bm + 2 * halo, bn), F32)
    val = pltpu.VMEM((bm, bn), F32)
    return pl.pallas_call(
        functools.partial(_conv_glu_tail_kernel, seq=seq, nj=nj),
        grid=(m // bm - 1, nj + 1),
        in_specs=[_resident((bm, k), lambda i, j: (i + 1, 0)),
                  _resident((halo, k), lambda i, j: (prev_tile(i + 1), 0)),
                  _resident((halo, k), lambda i, j: (next_tile(i + 1), 0)),
                  pl.BlockSpec((k, bn), lambda i, j: (0, w_tile(j))),
                  pl.BlockSpec((k, bn), lambda i, j: (0, w_tile(j))),
                  pl.BlockSpec((CONV_WIDTH, bn), lambda i, j: (0, o_tile(j))),
                  pl.BlockSpec((1, bn), lambda i, j: (0, o_tile(j))),
                  _ALIASED],
        out_specs=pl.BlockSpec((bm, bn), lambda i, j: (i + 1, o_tile(j))),
        out_shape=jax.ShapeDtypeStruct((m, dff), BF16),
        input_output_aliases={7: 0},
        scratch_shapes=[ext, gate, gate, val, val],
        compiler_params=_params(56, ("parallel", "arbitrary")),
        name="conv_glu_up",
    )(h, h, h, wgb, wvb, conv_w, conv_b, buf)


def kernel(x, norm1_gain, w_in, w_fourier_out, lambdas, subln_gain, rel_bias_table, w_attn_out,
           w_gate, b_gate, w_o, norm2_gain, w_up, conv_w, conv_b, w_down, final_norm_gain):
    batch, seq, d_model = x.shape
    depth = w_in.shape[0]
    n_heads = rel_bias_table.shape[1]
    d_attn = n_heads * HEAD_WIDTH
    d_in = w_in.shape[2]
    tq = 512

    cs, sn, chan = _dft_constants(seq, FOURIER_GROUP_DIM)
    col = np.ones((1, d_in), np.float32)
    col[:, D_FOURIER:D_FOURIER + d_attn] = ATTN_HEAD_DIM ** -0.5 * LOG2_E
    col_scale = jnp.asarray(col)
    wb = bias_window(rel_bias_table, seq, tq)

    x = x.reshape(batch * seq, d_model)
    m = batch * seq
    u = jnp.zeros((m, d_in), BF16)
    mixed = jnp.zeros((m, d_model), BF16)
    act = jnp.zeros((m, w_up.shape[2] // 2), BF16)
    for l in range(depth):
        lam_init = 0.8 - 0.6 * math.exp(-0.3 * l)
        h = rmsnorm(x, norm1_gain[l], BF16)
        u = in_proj(h, w_in, l, col_scale, u)
        f = fourier_mix(u, chan, cs, sn, seq)
        a = diff_attention(u, wb, lambdas[l], subln_gain[l], lam_init, seq, n_heads, tq)
        mixed = gated_mix(h, f, a, w_gate, w_fourier_out, w_attn_out, l, b_gate[l], mixed)
        x = proj_residual(mixed, w_o, l, x, head_rows=2 * HEAD_ROWS, bn_head=256, bm=1024, bn=1024)
        h2 = rmsnorm(x, norm2_gain[l], BF16)
        act = conv_glu_up(h2, w_up, l, conv_w[l], conv_b[l], seq, act)
        x = proj_residual(act, w_down, l, x, head_rows=HEAD_ROWS, bn_head=256, bm=1024, bn=256)
    out = rmsnorm(x, final_norm_gain, F32)
    return out.reshape(batch, seq, d_model)
```
